```python
import math
import jax, jax.numpy as jnp
from jax import lax
import numpy as np

D_MODEL = 1024
BATCH = 8
SEQ = 8192
DEPTH = 2
DEC_BATCH = 16
DEC_SEQ = 2048
PAST_LEN = 128

GRID_W = 64
N_META = 16
WIN_R = 8
WIN_C = 16
NA_HEADS = 6
NA_HD = 64
DA_HEADS = 6
DA_HD = 32
DA_VD = 64
DA_ROT = DA_HD // 4
MLA_HEADS = 4
MLA_NOPE = 64
MLA_ROPE = 32
MLA_VD = 64
Q_LORA = 256
KV_LORA = 128
ROPE_THETA = 500000.0
D_FF = 2816
Q_BLOCK = 128
EPS = 1e-6
NA_W = NA_HEADS * NA_HD
DA_W = DA_HEADS * DA_VD
MLA_W = MLA_HEADS * MLA_VD
MIX_W = NA_W + DA_W + MLA_W
DA_QK_W = DA_HEADS * 2 * DA_HD
IN_W = 3 * NA_W + 2 * DA_QK_W + DA_W + Q_LORA + KV_LORA + MLA_ROPE

kernel_name = 'hybrid_bidir_encoder_na_diff_mla'


def rms_norm(x, g):
    xf = x.astype(jnp.float32)
    y = xf * lax.rsqrt(jnp.mean(xf * xf, axis=-1, keepdims=True) + EPS)
    return (y * g.astype(jnp.float32)).astype(x.dtype)


def swiglu(x, w_gate, w_up, w_down):
    return (jax.nn.silu(x @ w_gate) * (x @ w_up)) @ w_down


def rope_tables(pos, dim, dtype):
    inv = ROPE_THETA ** (-(jnp.arange(0, dim, 2, dtype=jnp.float32) / dim))
    ang = pos[:, None] * inv[None, :]
    return jnp.cos(ang).astype(dtype), jnp.sin(ang).astype(dtype)


def apply_rope(x, cos, sin):
    half = x.shape[-1] // 2
    x1, x2 = x[..., :half], x[..., half:]
    return jnp.concatenate([x1 * cos - x2 * sin, x2 * cos + x1 * sin], axis=-1)


def sweep_queries(attend, qs):
    meta_out = attend(*[q[:, :N_META] for q in qs])
    real = [q[:, N_META:] for q in qs]
    b, n = real[0].shape[0], real[0].shape[1]
    nb = n // Q_BLOCK
    blocks = tuple(jnp.moveaxis(q.reshape((b, nb, Q_BLOCK) + q.shape[2:]), 1, 0) for q in real)
    out = lax.map(lambda xs: attend(*xs), blocks)
    out = jnp.moveaxis(out, 0, 1).reshape((b, n) + out.shape[3:])
    return jnp.concatenate([meta_out, out], axis=1)


def neighbourhood_attention(q, k, v, rel_bias, rows):
    b = q.shape[0]
    n = rows * GRID_W
    wr = min(WIN_R, rows)
    scale = NA_HD ** -0.5
    q = q.reshape(b, N_META + n, NA_HEADS, NA_HD)
    k = k.reshape(b, N_META + n, NA_HEADS, NA_HD)
    v = v.reshape(b, N_META + n, NA_HEADS, NA_HD)
    qm, km, vm = q[:, :N_META], k[:, :N_META], v[:, :N_META]
    sm = jnp.einsum('bqhd,bkhd->bhqk', qm, km, preferred_element_type=jnp.float32) * scale
    om = jnp.einsum('bhqk,bkhd->bqhd', jax.nn.softmax(sm, axis=-1).astype(v.dtype), vm)
    qg = q[:, N_META:].reshape(b, rows, GRID_W, NA_HEADS, NA_HD)
    kg = k[:, N_META:].reshape(b, rows, GRID_W, NA_HEADS, NA_HD)
    vg = v[:, N_META:].reshape(b, rows, GRID_W, NA_HEADS, NA_HD)
    row_start = jnp.clip(jnp.arange(rows) - WIN_R // 2, 0, rows - wr)
    col_idx = jnp.clip(jnp.arange(GRID_W) - WIN_C // 2, 0, GRID_W - WIN_C)[:, None] + jnp.arange(WIN_C)[None, :]
    col_off = col_idx - jnp.arange(GRID_W)[:, None] + (WIN_C - 1)
    bias_c = rel_bias[:, :, col_off]

    def row_block(r):
        rs = row_start[r]
        q_row = lax.dynamic_index_in_dim(qg, r, axis=1, keepdims=False)
        k_win = lax.dynamic_slice_in_dim(kg, rs, wr, axis=1)[:, :, col_idx]
        v_win = lax.dynamic_slice_in_dim(vg, rs, wr, axis=1)[:, :, col_idx]
        bias = bias_c[:, rs + jnp.arange(wr) - r + (WIN_R - 1)]
        s_win = jnp.einsum('bchd,bwckhd->bhcwk', q_row, k_win, preferred_element_type=jnp.float32) * scale
        s_win = s_win + jnp.transpose(bias, (0, 2, 1, 3)).astype(jnp.float32)
        s_meta = jnp.einsum('bchd,bmhd->bhcm', q_row, km, preferred_element_type=jnp.float32) * scale
        s = jnp.concatenate([s_win.reshape(b, NA_HEADS, GRID_W, wr * WIN_C), s_meta], axis=-1)
        p = jax.nn.softmax(s, axis=-1).astype(v.dtype)
        p_win = p[..., :wr * WIN_C].reshape(b, NA_HEADS, GRID_W, wr, WIN_C)
        p_meta = p[..., wr * WIN_C:]
        return (jnp.einsum('bhcwk,bwckhd->bchd', p_win, v_win)
                + jnp.einsum('bhcm,bmhd->bchd', p_meta, vm))

    og = lax.map(row_block, jnp.arange(rows))
    og = jnp.moveaxis(og, 0, 1).reshape(b, n, NA_W)
    return jnp.concatenate([om.reshape(b, N_META, NA_W), og], axis=1)


def differential_attention(q, k, v, lam_params, subln_g, layer, cos, sin):
    b, t = q.shape[0], q.shape[1]
    q = q.reshape(b, t, DA_HEADS, 2, DA_HD)
    k = k.reshape(b, t, DA_HEADS, 2, DA_HD)
    v = v.reshape(b, t, DA_HEADS, DA_VD)
    c, s_ = cos[:, None, None, :], sin[:, None, None, :]
    q = jnp.concatenate([apply_rope(q[..., :DA_ROT], c, s_), q[..., DA_ROT:]], axis=-1)
    k = jnp.concatenate([apply_rope(k[..., :DA_ROT], c, s_), k[..., DA_ROT:]], axis=-1)
    lp = lam_params.astype(jnp.float32)
    lam_init = 0.8 - 0.6 * math.exp(-0.3 * layer)
    lam = jnp.exp(jnp.sum(lp[0] * lp[1])) - jnp.exp(jnp.sum(lp[2] * lp[3])) + lam_init
    scale = DA_HD ** -0.5

    def attend(qb):
        tq = qb.shape[1]
        s = jnp.einsum('bqhmd,bkhmd->bhmqk', qb, k, preferred_element_type=jnp.float32) * scale
        p = jax.nn.softmax(s, axis=-1)
        a = (p[:, :, 0] - lam * p[:, :, 1]).astype(v.dtype)
        o = jnp.einsum('bhqk,bkhe->bqhe', a, v)
        o = rms_norm(o, subln_g) * (1.0 - lam_init)
        return o.reshape(b, tq, DA_W)

    return sweep_queries(attend, (q,))


def latent_attention(cq, ckv, krope, q_norm_g, kv_norm_g, w_uq, w_ukv, cos, sin):
    b, t = cq.shape[0], cq.shape[1]
    q = (rms_norm(cq, q_norm_g) @ w_uq).reshape(b, t, MLA_HEADS, MLA_NOPE + MLA_ROPE)
    q_nope = q[..., :MLA_NOPE]
    q_rope = apply_rope(q[..., MLA_NOPE:], cos[:, None, :], sin[:, None, :])
    kv = (rms_norm(ckv, kv_norm_g) @ w_ukv).reshape(b, t, MLA_HEADS, MLA_NOPE + MLA_VD)
    k_nope, v = kv[..., :MLA_NOPE], kv[..., MLA_NOPE:]
    k_rope = apply_rope(krope, cos, sin)
    scale = (MLA_NOPE + MLA_ROPE) ** -0.5

    def attend(qn, qr):
        tq = qn.shape[1]
        s = (jnp.einsum('bqhd,bkhd->bhqk', qn, k_nope, preferred_element_type=jnp.float32)
             + jnp.einsum('bqhr,bkr->bhqk', qr, k_rope, preferred_element_type=jnp.float32)) * scale
        p = jax.nn.softmax(s, axis=-1).astype(v.dtype)
        return jnp.einsum('bhqk,bkhe->bqhe', p, v).reshape(b, tq, MLA_W)

    return sweep_queries(attend, (q_nope, q_rope))


def trunk(x, meta_tokens, norm_g, final_norm_g, ffn_w_gate, ffn_w_up, ffn_w_down, w_in, w_out,
          na_rel_bias, da_lambda, da_subln_g, mla_q_norm_g, mla_kv_norm_g, mla_w_uq, mla_w_ukv):
    b, n, _ = x.shape
    rows = n // GRID_W
    meta = jnp.broadcast_to(meta_tokens[None].astype(x.dtype), (b, N_META, D_MODEL))
    h = jnp.concatenate([meta, x], axis=1)
    pos = jnp.arange(N_META + n, dtype=jnp.float32)
    cos_da, sin_da = rope_tables(pos, DA_ROT, x.dtype)
    cos_mla, sin_mla = rope_tables(pos, MLA_ROPE, x.dtype)
    sizes = (NA_W, NA_W, NA_W, DA_QK_W, DA_QK_W, DA_W, Q_LORA, KV_LORA, MLA_ROPE)
    split_idx = np.cumsum(sizes)[:-1].tolist()
    for l in range(DEPTH):
        h = h + 0.5 * swiglu(rms_norm(h, norm_g[l, 0]), ffn_w_gate[l, 0], ffn_w_up[l, 0], ffn_w_down[l, 0])
        u = rms_norm(h, norm_g[l, 1]) @ w_in[l]
        na_q, na_k, na_v, da_q, da_k, da_v, mla_cq, mla_ckv, mla_kr = jnp.split(u, split_idx, axis=-1)
        o_na = neighbourhood_attention(na_q, na_k, na_v, na_rel_bias[l], rows)
        o_da = differential_attention(da_q, da_k, da_v, da_lambda[l], da_subln_g[l], l, cos_da, sin_da)
        o_mla = latent_attention(mla_cq, mla_ckv, mla_kr, mla_q_norm_g[l], mla_kv_norm_g[l],
                                 mla_w_uq[l], mla_w_ukv[l], cos_mla, sin_mla)
        h = h + jnp.concatenate([o_na, o_da, o_mla], axis=-1) @ w_out[l]
        h = h + 0.5 * swiglu(rms_norm(h, norm_g[l, 2]), ffn_w_gate[l, 1], ffn_w_up[l, 1], ffn_w_down[l, 1])
    return rms_norm(h, final_norm_g)[:, N_META:]


def setup_inputs(seed: int = 0) -> dict:
    key = jax.random.key(seed)
    ks = jax.random.split(key, 17)
    nrm = lambda k, shape, s: jax.random.normal(k, shape, jnp.float32) * s
    return {
        'x_prompt': nrm(ks[0], (BATCH, SEQ, D_MODEL), 1.0),
        'x_sample': nrm(ks[1], (DEC_BATCH, DEC_SEQ, D_MODEL), 1.0),
        'meta_tokens': nrm(ks[2], (N_META, D_MODEL), 1.0),
        'norm_g': 1.0 + nrm(ks[3], (DEPTH, 3, D_MODEL), 0.05),
        'final_norm_g': 1.0 + nrm(ks[4], (D_MODEL,), 0.05),
        'ffn_w_gate': nrm(ks[5], (DEPTH, 2, D_MODEL, D_FF), D_MODEL ** -0.5),
        'ffn_w_up': nrm(ks[6], (DEPTH, 2, D_MODEL, D_FF), D_MODEL ** -0.5),
        'ffn_w_down': nrm(ks[7], (DEPTH, 2, D_FF, D_MODEL), D_FF ** -0.5),
        'w_in': nrm(ks[8], (DEPTH, D_MODEL, IN_W), D_MODEL ** -0.5),
        'w_out': nrm(ks[9], (DEPTH, MIX_W, D_MODEL), MIX_W ** -0.5),
        'na_rel_bias': nrm(ks[10], (DEPTH, NA_HEADS, 2 * WIN_R - 1, 2 * WIN_C - 1), 0.1),
        'da_lambda': nrm(ks[11], (DEPTH, 4, DA_HD), 0.1),
        'da_subln_g': 1.0 + nrm(ks[12], (DEPTH, DA_VD), 0.05),
        'mla_q_norm_g': 1.0 + nrm(ks[13], (DEPTH, Q_LORA), 0.05),
        'mla_kv_norm_g': 1.0 + nrm(ks[14], (DEPTH, KV_LORA), 0.05),
        'mla_w_uq': nrm(ks[15], (DEPTH, Q_LORA, MLA_HEADS * (MLA_NOPE + MLA_ROPE)), Q_LORA ** -0.5),
        'mla_w_ukv': nrm(ks[16], (DEPTH, KV_LORA, MLA_HEADS * (MLA_NOPE + MLA_VD)), KV_LORA ** -0.5),
    }


def reference(x_prompt, x_sample, meta_tokens, norm_g, final_norm_g, ffn_w_gate, ffn_w_up, ffn_w_down,
              w_in, w_out, na_rel_bias, da_lambda, da_subln_g, mla_q_norm_g, mla_kv_norm_g, mla_w_uq, mla_w_ukv):
    y_prompt = trunk(x_prompt, meta_tokens, norm_g, final_norm_g, ffn_w_gate, ffn_w_up, ffn_w_down, w_in, w_out,
                     na_rel_bias, da_lambda, da_subln_g, mla_q_norm_g, mla_kv_norm_g, mla_w_uq, mla_w_ukv)
    y_sample = trunk(x_sample, meta_tokens, norm_g, final_norm_g, ffn_w_gate, ffn_w_up, ffn_w_down, w_in, w_out,
                     na_rel_bias, da_lambda, da_subln_g, mla_q_norm_g, mla_kv_norm_g, mla_w_uq, mla_w_ukv)
    return (y_prompt, y_sample)
```

```python
import functools
import math
from typing import NamedTuple

import jax
import jax.numpy as jnp
import numpy as np
from jax import lax
from jax.experimental import pallas as pl
from jax.experimental.pallas import tpu as pltpu

D_MODEL = 1024
DEPTH = 2
GRID_W = 64
N_META = 16
WIN_R = 8
WIN_C = 16
NA_HEADS = 6
NA_HD = 64
DA_HEADS = 6
DA_HD = 32
DA_VD = 64
DA_ROT = DA_HD // 4
MLA_HEADS = 4
MLA_NOPE = 64
MLA_ROPE = 32
MLA_VD = 64
Q_LORA = 256
KV_LORA = 128
ROPE_THETA = 500000.0
D_FF = 2816
EPS = 1e-6
NA_W = NA_HEADS * NA_HD
DA_W = DA_HEADS * DA_VD
MLA_W = MLA_HEADS * MLA_VD
DA_QK_W = DA_HEADS * 2 * DA_HD
IN_W = 3 * NA_W + 2 * DA_QK_W + DA_W + Q_LORA + KV_LORA + MLA_ROPE

LANES = 128
MXU_DIM = 256
VMEM_LIMIT_BYTES = 58 * 1024 * 1024

TM = 512
TK = 256
NA_QROWS = 4
NA_KROWS = NA_QROWS + WIN_R
LOOKAHEAD = 2
FF_SPLIT = 1536
IN_W_PAD = 2816

LOG2E = 1.4426950408889634
NEG = -1e30
ONES_ROWS = 16


def _round_up(x, m):
    return (x + m - 1) // m * m


class Group(NamedTuple):
    batch: int
    n: int
    real0: int
    meta0: int


class Layout(NamedTuple):
    groups: tuple
    rows: int


def _make_layout(shapes):
    row = 0
    real0 = []
    for b, n in shapes:
        assert n % TM == 0 and row % n == 0 and n % GRID_W == 0
        assert (n // GRID_W) % NA_QROWS == 0 and n // GRID_W >= NA_KROWS
        real0.append(row)
        row += b * n
    groups = []
    for (b, n), r0 in zip(shapes, real0):
        groups.append(Group(b, n, r0, row))
        row += _round_up(b * N_META, LANES)
    return Layout(tuple(groups), _round_up(row, TM))


def _rms(x, g):
    return x * lax.rsqrt(jnp.mean(x * x, axis=-1, keepdims=True) + EPS) * g


def _swiglu_half(xn, wg_ref, wu_ref, wd_ref):
    acc = None
    for lo, hi in ((0, FF_SPLIT), (FF_SPLIT, D_FF)):
        gate = jnp.dot(xn, wg_ref[:, lo:hi], preferred_element_type=jnp.float32)
        up = jnp.dot(xn, wu_ref[:, lo:hi], preferred_element_type=jnp.float32)
        hm = (gate * jax.nn.sigmoid(gate) * up).astype(jnp.bfloat16)
        part = jnp.dot(hm, wd_ref[lo:hi, :], preferred_element_type=jnp.float32)
        acc = part if acc is None else acc + part
    return 0.5 * acc


def _rope_chunk(x, c, s, half, period):
    lane = lax.broadcasted_iota(jnp.int32, x.shape, 1)
    lo = (lane & (period - 1)) < half
    partner = jnp.where(lo, pltpu.roll(x, LANES - half, 1), pltpu.roll(x, half, 1))
    return x * c + partner * s


def _ffn_inproj_kernel(h_ref, cda_ref, sda_ref, cm_ref, sm_ref, ga_ref, gb_ref,
                       wg_ref, wu_ref, wd_ref, win_ref, gq_ref, gkv_ref, wuq_ref, wukv_ref,
                       h1_ref, naq_ref, nak_ref, nav_ref, daq_ref, dak_ref, dav_ref,
                       mq_ref, mk_ref, mv_ref):
    x = h_ref[...]
    xn = _rms(x, ga_ref[...]).astype(jnp.bfloat16)
    h1 = x + _swiglu_half(xn, wg_ref, wu_ref, wd_ref)
    h1_ref[...] = h1
    xn2 = _rms(h1, gb_ref[...]).astype(jnp.bfloat16)

    u = jnp.dot(xn2, win_ref[:, 0:3 * NA_W], preferred_element_type=jnp.float32)
    naq_ref[0] = (u[:, 0:NA_W] * (NA_HD ** -0.5 * LOG2E)).T.astype(jnp.bfloat16)
    nak_ref[...] = u[:, NA_W:2 * NA_W].astype(jnp.bfloat16)
    vt = u[:, 2 * NA_W:3 * NA_W].T.astype(jnp.bfloat16)
    for c in range(TM // LANES):
        nav_ref[c] = vt[:, c * LANES:(c + 1) * LANES]

    o0 = 3 * NA_W
    u = jnp.dot(xn2, win_ref[:, o0:o0 + 2 * DA_QK_W + DA_W], preferred_element_type=jnp.float32)
    cda, sda = cda_ref[...], sda_ref[...]
    q = jnp.concatenate([_rope_chunk(u[:, c * LANES:(c + 1) * LANES], cda, sda, DA_ROT // 2, DA_HD)
                         for c in range(DA_QK_W // LANES)], axis=1)
    k = jnp.concatenate([_rope_chunk(u[:, DA_QK_W + c * LANES:DA_QK_W + (c + 1) * LANES], cda, sda, DA_ROT // 2, DA_HD)
                         for c in range(DA_QK_W // LANES)], axis=1)
    daq_ref[0] = (q * (DA_HD ** -0.5 * LOG2E)).T.astype(jnp.bfloat16)
    dak_ref[...] = k.astype(jnp.bfloat16)
    vt = u[:, 2 * DA_QK_W:2 * DA_QK_W + DA_W].T.astype(jnp.bfloat16)
    for c in range(TM // TK):
        dav_ref[c] = vt[:, c * TK:(c + 1) * TK]

    o1 = o0 + 2 * DA_QK_W + DA_W
    u = jnp.dot(xn2, win_ref[:, o1:IN_W_PAD], preferred_element_type=jnp.float32)
    cm, sm = cm_ref[...], sm_ref[...]
    cq = _rms(u[:, 0:Q_LORA], gq_ref[...]).astype(jnp.bfloat16)
    qm = jnp.dot(cq, wuq_ref[...], preferred_element_type=jnp.float32)
    nope_w = MLA_HEADS * MLA_NOPE
    q_rope = _rope_chunk(qm[:, nope_w:nope_w + LANES], cm, sm, MLA_ROPE // 2, MLA_ROPE)
    qm = jnp.concatenate([qm[:, 0:nope_w], q_rope], axis=1) * ((MLA_NOPE + MLA_ROPE) ** -0.5 * LOG2E)
    mq_ref[0] = qm.T.astype(jnp.bfloat16)
    ckv = _rms(u[:, Q_LORA:Q_LORA + KV_LORA], gkv_ref[...]).astype(jnp.bfloat16)
    kv = jnp.dot(ckv, wukv_ref[...], preferred_element_type=jnp.float32)
    kr = _rope_chunk(u[:, Q_LORA + KV_LORA:Q_LORA + KV_LORA + LANES], cm, sm, MLA_ROPE // 2, MLA_ROPE)
    kf = jnp.concatenate([kv[:, 0:LANES], kr, kv[:, LANES:2 * LANES], kr], axis=1)
    mk_ref[...] = kf.astype(jnp.bfloat16)
    vt = kv[:, nope_w:nope_w + MLA_W].T.astype(jnp.bfloat16)
    for c in range(TM // TK):
        mv_ref[c] = vt[:, c * TK:(c + 1) * TK]


def _outproj_ffn_kernel(h_ref, ona_ref, oda_ref, om_ref, wo1_ref, wo2_ref, wo3_ref, g_ref,
                        wg_ref, wu_ref, wd_ref, gf_ref, o_ref, *, final_norm):
    h = h_ref[...]
    h = h + jnp.dot(ona_ref[...], wo1_ref[...], preferred_element_type=jnp.float32)
    h = h + jnp.dot(oda_ref[...], wo2_ref[...], preferred_element_type=jnp.float32)
    h = h + jnp.dot(om_ref[...], wo3_ref[...], preferred_element_type=jnp.float32)
    xn = _rms(h, g_ref[...]).astype(jnp.bfloat16)
    h = h + _swiglu_half(xn, wg_ref, wu_ref, wd_ref)
    if final_norm:
        h = _rms(h, gf_ref[...])
    o_ref[...] = h


def _const_spec(shape):
    nd = len(shape)
    return pl.BlockSpec(shape, lambda i: (0,) * nd, pipeline_mode=pl.Buffered(1))


def _ffn_inproj(h, tabs, ga, gb, wg, wu, wd, win, gq, gkv, wuq, wukv):
    rows = h.shape[0]
    nt = rows // TM
    row_spec = lambda w: pl.BlockSpec((TM, w), lambda i: (i, 0))
    tile_spec = lambda r: pl.BlockSpec((1, r, TM), lambda i: (i, 0, 0))
    bf = jnp.bfloat16
    out_shape = (
        jax.ShapeDtypeStruct((rows, D_MODEL), jnp.float32),
        jax.ShapeDtypeStruct((nt, NA_W, TM), bf),
        jax.ShapeDtypeStruct((rows, NA_W), bf),
        jax.ShapeDtypeStruct((rows // LANES, NA_W, LANES), bf),
        jax.ShapeDtypeStruct((nt, DA_QK_W, TM), bf),
        jax.ShapeDtypeStruct((rows, DA_QK_W), bf),
        jax.ShapeDtypeStruct((rows // TK, DA_W, TK), bf),
        jax.ShapeDtypeStruct((nt, MLA_HEADS * (MLA_NOPE + MLA_ROPE), TM), bf),
        jax.ShapeDtypeStruct((rows, 2 * MXU_DIM), bf),
        jax.ShapeDtypeStruct((rows // TK, MLA_W, TK), bf),
    )
    out_specs = (
        row_spec(D_MODEL),
        tile_spec(NA_W),
        row_spec(NA_W),
        pl.BlockSpec((TM // LANES, NA_W, LANES), lambda i: (i, 0, 0)),
        tile_spec(DA_QK_W),
        row_spec(DA_QK_W),
        pl.BlockSpec((TM // TK, DA_W, TK), lambda i: (i, 0, 0)),
        tile_spec(MLA_HEADS * (MLA_NOPE + MLA_ROPE)),
        row_spec(2 * MXU_DIM),
        pl.BlockSpec((TM // TK, MLA_W, TK), lambda i: (i, 0, 0)),
    )
    in_specs = [row_spec(D_MODEL)] + [row_spec(LANES)] * 4 + [
        _const_spec(a.shape) for a in (ga, gb, wg, wu, wd, win, gq, gkv, wuq, wukv)]
    return pl.pallas_call(
        _ffn_inproj_kernel,
        grid=(nt,),
        in_specs=in_specs,
        out_specs=out_specs,
        out_shape=out_shape,
        compiler_params=pltpu.CompilerParams(dimension_semantics=("arbitrary",),
                                             vmem_limit_bytes=VMEM_LIMIT_BYTES),
        name="ffn_inproj",
    )(h, *tabs, ga, gb, wg, wu, wd, win, gq, gkv, wuq, wukv)


def _outproj_ffn(h, ona, oda, om, wo1, wo2, wo3, g, wg, wu, wd, gf, final_norm):
    rows = h.shape[0]
    row_spec = lambda w: pl.BlockSpec((TM, w), lambda i: (i, 0))
    in_specs = [row_spec(D_MODEL), row_spec(NA_W), row_spec(DA_W), row_spec(MLA_W)] + [
        _const_spec(a.shape) for a in (wo1, wo2, wo3, g, wg, wu, wd, gf)]
    return pl.pallas_call(
        functools.partial(_outproj_ffn_kernel, final_norm=final_norm),
        grid=(rows // TM,),
        in_specs=in_specs,
        out_specs=row_spec(D_MODEL),
        out_shape=jax.ShapeDtypeStruct((rows, D_MODEL), jnp.float32),
        compiler_params=pltpu.CompilerParams(dimension_semantics=("arbitrary",),
                                             vmem_limit_bytes=VMEM_LIMIT_BYTES),
        name="outproj_ffn",
    )(h, ona, oda, om, wo1, wo2, wo3, g, wg, wu, wd, gf)


def _row_band(block, lo, hi):
    row = lax.broadcasted_iota(jnp.int32, block.shape, 0)
    return jnp.where((row >= lo) & (row < hi), block, jnp.zeros_like(block))


def _with_ones(vt):
    return jnp.concatenate([vt, jnp.ones((ONES_ROWS, vt.shape[1]), vt.dtype)], axis=0)


def _meta_key_mask(b):
    row = lax.broadcasted_iota(jnp.int32, (LANES, 1), 0)
    lo = (b % (LANES // N_META)) * N_META
    return jnp.where((row >= lo) & (row < lo + N_META), 0.0, NEG).astype(jnp.float32)


class FlashCfg(NamedTuple):
    n_soft: int
    group_w: int
    soft_per_group: int
    v_of_soft: tuple
    out_w: int


DA_CFG = FlashCfg(2 * DA_HEADS, LANES, 4, tuple(i // 2 for i in range(2 * DA_HEADS)), DA_W)
MLA_CFG = FlashCfg(MLA_HEADS, MXU_DIM, 2, tuple(range(MLA_HEADS)), MLA_W)


def _build_rhs(cfg, qt_ref, rhs_ref):
    for i in range(cfg.n_soft):
        g, j = i // cfg.soft_per_group, i % cfg.soft_per_group
        if cfg is DA_CFG:
            rhs_ref[i] = _row_band(qt_ref[0, g * LANES:(g + 1) * LANES, :], j * DA_HD, (j + 1) * DA_HD)
        else:
            nope = _row_band(qt_ref[0, g * LANES:(g + 1) * LANES, :], j * MLA_NOPE, (j + 1) * MLA_NOPE)
            rope0 = MLA_HEADS * MLA_NOPE
            rope = _row_band(qt_ref[0, rope0:rope0 + LANES, :], i * MLA_ROPE, (i + 1) * MLA_ROPE)
            rhs_ref[i] = jnp.concatenate([nope, rope], axis=0)


def _flash_tile(cfg, kblk_of, vt_of, mask, rhs_ref, m_ref, acc_ref):
    def score(i):
        s = jnp.dot(kblk_of(i // cfg.soft_per_group), rhs_ref[i], preferred_element_type=jnp.float32)
        return s if mask is None else s + mask

    pending = [score(i) for i in range(LOOKAHEAD)]
    for i in range(cfg.n_soft):
        s = pending.pop(0)
        if i + LOOKAHEAD < cfg.n_soft:
            pending.append(score(i + LOOKAHEAD))
        m_prev = m_ref[i]
        m_new = jnp.maximum(m_prev, jnp.max(s, axis=0, keepdims=True))
        alpha = jnp.exp2(m_prev - m_new)
        p = jnp.exp2(s - m_new).astype(jnp.bfloat16)
        pv = jnp.dot(_with_ones(vt_of(cfg.v_of_soft[i])), p, preferred_element_type=jnp.float32)
        acc_ref[i] = alpha * acc_ref[i] + pv
        m_ref[i] = m_new


def _flash_kernel(*refs, cfg, n_keys, lam_init, meta_queries):
    if cfg is DA_CFG:
        (qt_ref, kf_ref, vt_ref, kfm_ref, vtm_ref, lam_ref, gs_ref, o_ref,
         rhs_ref, m_ref, acc_ref, *rest) = refs
    else:
        qt_ref, kf_ref, vt_ref, kfm_ref, vtm_ref, o_ref, rhs_ref, m_ref, acc_ref, *rest = refs
    b = pl.program_id(0)
    tq = qt_ref.shape[2]
    vd = DA_VD
    _build_rhs(cfg, qt_ref, rhs_ref)
    m_ref[...] = jnp.full(m_ref.shape, NEG, jnp.float32)
    acc_ref[...] = jnp.zeros(acc_ref.shape, jnp.float32)

    def body(kt, carry):
        k0 = pl.multiple_of(kt * TK, TK)
        _flash_tile(
            cfg,
            lambda g: kf_ref[pl.ds(k0, TK), g * cfg.group_w:(g + 1) * cfg.group_w],
            lambda h: vt_ref[kt, h * vd:(h + 1) * vd, :],
            None, rhs_ref, m_ref, acc_ref)
        return carry

    lax.fori_loop(0, n_keys // TK, body, 0)
    _flash_tile(
        cfg,
        lambda g: kfm_ref[:, g * cfg.group_w:(g + 1) * cfg.group_w],
        lambda h: vtm_ref[0, h * vd:(h + 1) * vd, :],
        _meta_key_mask(b), rhs_ref, m_ref, acc_ref)

    heads = []
    if cfg is DA_CFG:
        lp = lam_ref[...]
        lam = (jnp.exp(jnp.sum(lp[0:1] * lp[1:2], axis=1, keepdims=True))
               - jnp.exp(jnp.sum(lp[2:3] * lp[3:4], axis=1, keepdims=True)) + lam_init)
        for h in range(DA_HEADS):
            a1, a2 = acc_ref[2 * h], acc_ref[2 * h + 1]
            o = a1[0:vd] / a1[vd:vd + 1] - lam * (a2[0:vd] / a2[vd:vd + 1])
            o = o * lax.rsqrt(jnp.mean(o * o, axis=0, keepdims=True) + EPS) * gs_ref[...]
            heads.append(o * (1.0 - lam_init))
    else:
        for h in range(MLA_HEADS):
            a = acc_ref[h]
            heads.append(a[0:vd] / a[vd:vd + 1])
    pairs = [jnp.concatenate(heads[2 * g:2 * g + 2], axis=0).T for g in range(len(heads) // 2)]
    out = jnp.concatenate(pairs, axis=1)
    if meta_queries:
        stage_ref = rest[0]
        stage_ref[...] = out
        r0 = pl.multiple_of((b % (LANES // N_META)) * N_META, N_META)
        o_ref[...] = stage_ref[pl.ds(r0, N_META), :].astype(o_ref.dtype)
    else:
        o_ref[...] = out.astype(o_ref.dtype)


def _flash_call(cfg, grp, qt, kf, vt, extra, o_prev, lam_init, meta_queries, name):
    n, bsz = grp.n, grp.batch
    qw, kw, vw = qt.shape[1], kf.shape[1], vt.shape[1]
    tiles_per_seq = n // TM
    meta_blk0 = grp.meta0 // LANES
    per_blk = LANES // N_META
    if meta_queries:
        grid = (bsz,)
        tq = LANES
        q_map = lambda b: ((meta_blk0 + b // per_blk) // (TM // LANES), 0, (meta_blk0 + b // per_blk) % (TM // LANES))
        o_spec = pl.BlockSpec((N_META, cfg.out_w), lambda b: (grp.meta0 // N_META + b, 0))
        fix = lambda f: (lambda b: f(b))
        sem = ("arbitrary",)
    else:
        grid = (bsz, tiles_per_seq)
        tq = TM
        q_map = lambda b, i: (grp.real0 // TM + b * tiles_per_seq + i, 0, 0)
        o_spec = pl.BlockSpec((TM, cfg.out_w), lambda b, i: (grp.real0 // TM + b * tiles_per_seq + i, 0))
        fix = lambda f: (lambda b, i: f(b))
        sem = ("arbitrary", "arbitrary")
    in_specs = [
        pl.BlockSpec((1, qw, tq), q_map),
        pl.BlockSpec((n, kw), fix(lambda b: (grp.real0 // n + b, 0))),
        pl.BlockSpec((n // TK, vw, TK), fix(lambda b: (grp.real0 // n + b, 0, 0))),
        pl.BlockSpec((LANES, kw), fix(lambda b: (meta_blk0 + b // per_blk, 0))),
        pl.BlockSpec((1, vw, LANES), fix(lambda b: ((meta_blk0 + b // per_blk) // (TK // LANES), 0,
                                                     (meta_blk0 + b // per_blk) % (TK // LANES)))),
    ]
    args = [qt, kf, vt, kf, vt]
    for a in extra:
        in_specs.append(pl.BlockSpec(a.shape, fix(lambda b, nd=a.ndim: (0,) * nd)))
        args.append(a)
    in_specs.append(pl.BlockSpec(memory_space=pl.ANY))
    args.append(o_prev)
    scratch = [
        pltpu.VMEM((cfg.n_soft, cfg.group_w, tq), jnp.bfloat16),
        pltpu.VMEM((cfg.n_soft, 1, tq), jnp.float32),
        pltpu.VMEM((cfg.n_soft, DA_VD + ONES_ROWS, tq), jnp.float32),
    ]
    if meta_queries:
        scratch.append(pltpu.VMEM((LANES, cfg.out_w), jnp.float32))

    def body(*refs):
        n_in = len(args)
        ins, rest = refs[:n_in - 1], refs[n_in:]
        _flash_kernel(*ins, *rest, cfg=cfg, n_keys=n, lam_init=lam_init, meta_queries=meta_queries)

    return pl.pallas_call(
        body,
        grid=grid,
        in_specs=in_specs,
        out_specs=o_spec,
        out_shape=jax.ShapeDtypeStruct(o_prev.shape, o_prev.dtype),
        scratch_shapes=scratch,
        input_output_aliases={len(args) - 1: 0},
        compiler_params=pltpu.CompilerParams(dimension_semantics=sem, vmem_limit_bytes=VMEM_LIMIT_BYTES),
        name=name,
    )(*args)


def _na_kernel(qt_ref, kf_ref, vt_ref, kfm_ref, vtm_ref, bias_ref, prev_ref, o_ref, *, grid_rows):
    del prev_ref
    b, i = pl.program_id(0), pl.program_id(1)
    ws = jnp.clip(i * NA_QROWS - WIN_R // 2, 0, grid_rows - NA_KROWS)
    kwin = kf_ref[pl.ds(pl.multiple_of(ws * GRID_W, 2 * GRID_W), NA_KROWS * GRID_W), :]
    vwin = vt_ref[pl.ds(ws // 2, NA_KROWS // 2)]
    kmeta = kfm_ref[...]
    mmask = _meta_key_mask(b)
    heads = []
    for h in range(NA_HEADS):
        g, j = h // 2, h % 2
        rhs = _row_band(qt_ref[0, g * LANES:(g + 1) * LANES, :], j * NA_HD, (j + 1) * NA_HD)
        s1 = jnp.dot(kwin[:, g * LANES:(g + 1) * LANES], rhs, preferred_element_type=jnp.float32) + bias_ref[0, h]
        s2 = jnp.dot(kmeta[:, g * LANES:(g + 1) * LANES], rhs, preferred_element_type=jnp.float32) + mmask
        m = jnp.maximum(jnp.max(s1, axis=0, keepdims=True), jnp.max(s2, axis=0, keepdims=True))
        p1 = jnp.exp2(s1 - m).astype(jnp.bfloat16)
        p2 = jnp.exp2(s2 - m).astype(jnp.bfloat16)
        v1 = jnp.concatenate([vwin[c, h * NA_HD:(h + 1) * NA_HD, :] for c in range(NA_KROWS // 2)], axis=1)
        acc = (jnp.dot(_with_ones(v1), p1, preferred_element_type=jnp.float32)
               + jnp.dot(_with_ones(vtm_ref[0, h * NA_HD:(h + 1) * NA_HD, :]), p2,
                         preferred_element_type=jnp.float32))
        heads.append(acc[0:NA_HD] / acc[NA_HD:NA_HD + 1])
    pairs = [jnp.concatenate(heads[2 * g:2 * g + 2], axis=0).T for g in range(NA_HEADS // 2)]
    o_ref[...] = jnp.concatenate(pairs, axis=1).astype(o_ref.dtype)


def _na_meta_kernel(qt_ref, kf_ref, vt_ref, prev_ref, o_ref):
    del prev_ref
    kr = lax.broadcasted_iota(jnp.int32, (LANES, LANES), 0) // N_META
    qc = lax.broadcasted_iota(jnp.int32, (LANES, LANES), 1) // N_META
    mask = jnp.where(kr == qc, 0.0, NEG).astype(jnp.float32)
    kf = kf_ref[...]
    heads = []
    for h in range(NA_HEADS):
        g, j = h // 2, h % 2
        rhs = _row_band(qt_ref[0, g * LANES:(g + 1) * LANES, :], j * NA_HD, (j + 1) * NA_HD)
        s = jnp.dot(kf[:, g * LANES:(g + 1) * LANES], rhs, preferred_element_type=jnp.float32) + mask
        p = jnp.exp2(s - jnp.max(s, axis=0, keepdims=True)).astype(jnp.bfloat16)
        acc = jnp.dot(_with_ones(vt_ref[0, h * NA_HD:(h + 1) * NA_HD, :]), p, preferred_element_type=jnp.float32)
        heads.append(acc[0:NA_HD] / acc[NA_HD:NA_HD + 1])
    pairs = [jnp.concatenate(heads[2 * g:2 * g + 2], axis=0).T for g in range(NA_HEADS // 2)]
    o_ref[...] = jnp.concatenate(pairs, axis=1).astype(o_ref.dtype)


def _na_call(grp, qt, kf, vt, bias, o_prev):
    n, bsz = grp.n, grp.batch
    grid_rows = n // GRID_W
    steps = grid_rows // NA_QROWS
    tq = NA_QROWS * GRID_W
    meta_blk0 = grp.meta0 // LANES
    per_blk = LANES // N_META
    in_specs = [
        pl.BlockSpec((1, NA_W, tq), lambda b, i: (grp.real0 // TM + b * (n // TM) + i // (TM // tq), 0, i % (TM // tq))),
        pl.BlockSpec((n, NA_W), lambda b, i: (grp.real0 // n + b, 0)),
        pl.BlockSpec((n // LANES, NA_W, LANES), lambda b, i: (grp.real0 // n + b, 0, 0)),
        pl.BlockSpec((LANES, NA_W), lambda b, i: (meta_blk0 + b // per_blk, 0)),
        pl.BlockSpec((1, NA_W, LANES), lambda b, i: (meta_blk0 + b // per_blk, 0, 0)),
        pl.BlockSpec((1, NA_HEADS, NA_KROWS * GRID_W, tq),
                     lambda b, i: (jnp.where(i == 0, 0, jnp.where(i == steps - 1, 2, 1)), 0, 0, 0)),
        pl.BlockSpec(memory_space=pl.ANY),
    ]
    return pl.pallas_call(
        functools.partial(_na_kernel, grid_rows=grid_rows),
        grid=(bsz, steps),
        in_specs=in_specs,
        out_specs=pl.BlockSpec((tq, NA_W), lambda b, i: (grp.real0 // tq + b * steps + i, 0)),
        out_shape=jax.ShapeDtypeStruct(o_prev.shape, o_prev.dtype),
        input_output_aliases={6: 0},
        compiler_params=pltpu.CompilerParams(dimension_semantics=("arbitrary", "arbitrary"),
                                             vmem_limit_bytes=VMEM_LIMIT_BYTES),
        name="na_real",
    )(qt, kf, vt, kf, vt, bias, o_prev)


def _na_meta_call(layout, qt, kf, vt, o_prev):
    blk0 = layout.groups[0].meta0 // LANES
    nblk = sum(_round_up(g.batch * N_META, LANES) for g in layout.groups) // LANES
    sub = TM // LANES
    return pl.pallas_call(
        _na_meta_kernel,
        grid=(nblk,),
        in_specs=[
            pl.BlockSpec((1, NA_W, LANES), lambda i: ((blk0 + i) // sub, 0, (blk0 + i) % sub)),
            pl.BlockSpec((LANES, NA_W), lambda i: (blk0 + i, 0)),
            pl.BlockSpec((1, NA_W, LANES), lambda i: (blk0 + i, 0, 0)),
            pl.BlockSpec(memory_space=pl.ANY),
        ],
        out_specs=pl.BlockSpec((LANES, NA_W), lambda i: (blk0 + i, 0)),
        out_shape=jax.ShapeDtypeStruct(o_prev.shape, o_prev.dtype),
        input_output_aliases={3: 0},
        compiler_params=pltpu.CompilerParams(dimension_semantics=("arbitrary",),
                                             vmem_limit_bytes=VMEM_LIMIT_BYTES),
        name="na_meta",
    )(qt, kf, vt, o_prev)


def _na_bias_tables(rel_bias):
    kr = np.arange(NA_KROWS)[:, None, None, None]
    kc = np.arange(GRID_W)[None, :, None, None]
    qr = np.arange(NA_QROWS)[None, None, :, None]
    qc = np.arange(GRID_W)[None, None, None, :]
    cs = np.clip(qc - WIN_C // 2, 0, GRID_W - WIN_C)
    col_ok = (kc >= cs) & (kc < cs + WIN_C)
    cidx = np.clip(kc - qc + WIN_C - 1, 0, 2 * WIN_C - 2)
    tables = []
    for win_lo, q_off in ((0 * qr, qr), (qr, qr + WIN_R // 2), (0 * qr + NA_KROWS - WIN_R, qr + NA_KROWS - NA_QROWS)):
        row_ok = (kr >= win_lo) & (kr < win_lo + WIN_R)
        ridx = np.clip(kr - q_off + WIN_R - 1, 0, 2 * WIN_R - 2)
        ok = np.broadcast_to(row_ok & col_ok, (NA_KROWS, GRID_W, NA_QROWS, GRID_W))
        ridx_b = np.broadcast_to(ridx, ok.shape)
        cidx_b = np.broadcast_to(cidx, ok.shape)
        vals = rel_bias[:, ridx_b, cidx_b] * LOG2E
        t = jnp.where(ok[None], vals, NEG)
        tables.append(t.reshape(NA_HEADS, NA_KROWS * GRID_W, NA_QROWS * GRID_W))
    return jnp.stack(tables).astype(jnp.float32)


def _rope_tables(layout):
    pos = np.zeros((layout.rows,), np.float32)
    for g in layout.groups:
        r = np.arange(g.batch * g.n)
        pos[g.real0:g.real0 + g.batch * g.n] = N_META + r % g.n
        m = np.arange(g.batch * N_META)
        pos[g.meta0:g.meta0 + g.batch * N_META] = m % N_META
    pos = jnp.asarray(pos)

    def tables(dim, period):
        inv = ROPE_THETA ** (-(jnp.arange(0, dim, 2, dtype=jnp.float32) / dim))
        ang = pos[:, None] * inv[None, :]
        cos, sin = jnp.cos(ang), jnp.sin(ang)
        rest = period - dim
        c = jnp.concatenate([cos, cos, jnp.ones((layout.rows, rest), jnp.float32)], axis=1)
        s = jnp.concatenate([-sin, sin, jnp.zeros((layout.rows, rest), jnp.float32)], axis=1)
        reps = LANES // period
        return jnp.tile(c, (1, reps)), jnp.tile(s, (1, reps))

    cda, sda = tables(DA_ROT, DA_HD)
    cm, sm = tables(MLA_ROPE, MLA_ROPE)
    return cda, sda, cm, sm


def _prep_layer(l, norm_g, ffn_w_gate, ffn_w_up, ffn_w_down, w_in, w_out, mla_q_norm_g, mla_kv_norm_g,
                mla_w_uq, mla_w_ukv, da_subln_g):
    bf = jnp.bfloat16
    kr0 = IN_W - MLA_ROPE
    win = jnp.concatenate([w_in[l, :, :kr0]] + [w_in[l, :, kr0:]] * (LANES // MLA_ROPE), axis=1).astype(bf)
    uq = mla_w_uq[l].reshape(Q_LORA, MLA_HEADS, MLA_NOPE + MLA_ROPE)
    uq = jnp.concatenate([uq[:, :, :MLA_NOPE].reshape(Q_LORA, -1), uq[:, :, MLA_NOPE:].reshape(Q_LORA, -1)], axis=1)
    ukv = mla_w_ukv[l].reshape(KV_LORA, MLA_HEADS, MLA_NOPE + MLA_VD)
    ukv = jnp.concatenate([ukv[:, :, :MLA_NOPE].reshape(KV_LORA, -1), ukv[:, :, MLA_NOPE:].reshape(KV_LORA, -1)], axis=1)
    row = lambda v: v.reshape(1, -1)
    return dict(
        g=[row(norm_g[l, i]) for i in range(3)],
        ffn=[(ffn_w_gate[l, i].astype(bf), ffn_w_up[l, i].astype(bf), ffn_w_down[l, i].astype(bf)) for i in range(2)],
        win=win, uq=uq.astype(bf), ukv=ukv.astype(bf),
        gq=row(mla_q_norm_g[l]), gkv=row(mla_kv_norm_g[l]),
        wo=(w_out[l, :NA_W].astype(bf), w_out[l, NA_W:NA_W + DA_W].astype(bf), w_out[l, NA_W + DA_W:].astype(bf)),
        gs=da_subln_g[l].reshape(DA_VD, 1),
    )


def kernel(x_prompt, x_sample, meta_tokens, norm_g, final_norm_g, ffn_w_gate, ffn_w_up, ffn_w_down, w_in, w_out, na_rel_bias, da_lambda, da_subln_g, mla_q_norm_g, mla_kv_norm_g, mla_w_uq, mla_w_ukv):
    xs = (x_prompt, x_sample)
    layout = _make_layout([(x.shape[0], x.shape[1]) for x in xs])
    parts = [x.reshape(-1, D_MODEL) for x in xs]
    for g in layout.groups:
        blk = jnp.tile(meta_tokens.astype(jnp.float32), (g.batch, 1))
        parts.append(jnp.pad(blk, ((0, _round_up(g.batch * N_META, LANES) - g.batch * N_META), (0, 0))))
    used = sum(p.shape[0] for p in parts)
    parts.append(jnp.zeros((layout.rows - used, D_MODEL), jnp.float32))
    h = jnp.concatenate(parts, axis=0)
    tabs = _rope_tables(layout)
    gf = final_norm_g.reshape(1, -1)

    for l in range(DEPTH):
        p = _prep_layer(l, norm_g, ffn_w_gate, ffn_w_up, ffn_w_down, w_in, w_out, mla_q_norm_g, mla_kv_norm_g,
                        mla_w_uq, mla_w_ukv, da_subln_g)
        lam_init = 0.8 - 0.6 * math.exp(-0.3 * l)
        (h, naq, nak, nav, daq, dak, dav, mq, mk, mv) = _ffn_inproj(
            h, tabs, p["g"][0], p["g"][1], *p["ffn"][0], p["win"], p["gq"], p["gkv"], p["uq"], p["ukv"])
        bias = _na_bias_tables(na_rel_bias[l])
        o_na = jnp.zeros((layout.rows, NA_W), jnp.bfloat16)
        o_da = jnp.zeros((layout.rows, DA_W), jnp.bfloat16)
        o_m = jnp.zeros((layout.rows, MLA_W), jnp.bfloat16)
        o_na = _na_meta_call(layout, naq, nak, nav, o_na)
        for gi, grp in enumerate(layout.groups):
            o_na = _na_call(grp, naq, nak, nav, bias, o_na)
            for meta_q in (False, True):
                tag = f"g{gi}_{'meta' if meta_q else 'real'}"
                o_da = _flash_call(DA_CFG, grp, daq, dak, dav, (da_lambda[l], p["gs"]), o_da, lam_init, meta_q,
                                   "da_" + tag)
                o_m = _flash_call(MLA_CFG, grp, mq, mk, mv, (), o_m, lam_init, meta_q, "mla_" + tag)
        h = _outproj_ffn(h, o_na, o_da, o_m, *p["wo"], p["g"][2], *p["ffn"][1], gf, final_norm=(l == DEPTH - 1))

    outs = []
    for x, g in zip(xs, layout.groups):
        outs.append(h[g.real0:g.real0 + g.batch * g.n].reshape(x.shape))
    return tuple(outs)
```

```python
import functools
import math
from typing import NamedTuple

import jax
import jax.numpy as jnp
import numpy as np
from jax import lax
from jax.experimental import pallas as pl
from jax.experimental.pallas import tpu as pltpu

D_MODEL = 1024
DEPTH = 2
GRID_W = 64
N_META = 16
WIN_R = 8
WIN_C = 16
NA_HEADS = 6
NA_HD = 64
DA_HEADS = 6
DA_HD = 32
DA_VD = 64
DA_ROT = DA_HD // 4
MLA_HEADS = 4
MLA_NOPE = 64
MLA_ROPE = 32
MLA_VD = 64
Q_LORA = 256
KV_LORA = 128
ROPE_THETA = 500000.0
D_FF = 2816
EPS = 1e-6
NA_W = NA_HEADS * NA_HD
DA_W = DA_HEADS * DA_VD
MLA_W = MLA_HEADS * MLA_VD
DA_QK_W = DA_HEADS * 2 * DA_HD
IN_W = 3 * NA_W + 2 * DA_QK_W + DA_W + Q_LORA + KV_LORA + MLA_ROPE

LANES = 128
MXU_DIM = 256
VMEM_LIMIT_BYTES = 58 * 1024 * 1024

TM = 512
TK = 256
NA_QROWS = 4
NA_KROWS = NA_QROWS + WIN_R
LOOKAHEAD = 2
FF_SPLIT = 1536
IN_W_PAD = 2816

LOG2E = 1.4426950408889634
NEG = -1e30
ONES_ROWS = 16


def _round_up(x, m):
    return (x + m - 1) // m * m


class Group(NamedTuple):
    batch: int
    n: int
    real0: int
    meta0: int


class Layout(NamedTuple):
    groups: tuple
    rows: int


def _make_layout(shapes):
    row = 0
    real0 = []
    for b, n in shapes:
        assert n % TM == 0 and row % n == 0 and n % GRID_W == 0
        assert (n // GRID_W) % NA_QROWS == 0 and n // GRID_W >= NA_KROWS
        real0.append(row)
        row += b * n
    groups = []
    for (b, n), r0 in zip(shapes, real0):
        groups.append(Group(b, n, r0, row))
        row += _round_up(b * N_META, LANES)
    return Layout(tuple(groups), _round_up(row, TM))


def _rms(x, g):
    return x * lax.rsqrt(jnp.mean(x * x, axis=-1, keepdims=True) + EPS) * g


def _swiglu_half(xn, wg_ref, wu_ref, wd_ref):
    acc = None
    for lo, hi in ((0, FF_SPLIT), (FF_SPLIT, D_FF)):
        gate = jnp.dot(xn, wg_ref[:, lo:hi], preferred_element_type=jnp.float32)
        up = jnp.dot(xn, wu_ref[:, lo:hi], preferred_element_type=jnp.float32)
        hm = (gate * jax.nn.sigmoid(gate) * up).astype(jnp.bfloat16)
        part = jnp.dot(hm, wd_ref[lo:hi, :], preferred_element_type=jnp.float32)
        acc = part if acc is None else acc + part
    return 0.5 * acc


def _rope_chunk(x, c, s, half, period):
    lane = lax.broadcasted_iota(jnp.int32, x.shape, 1)
    lo = (lane & (period - 1)) < half
    partner = jnp.where(lo, pltpu.roll(x, LANES - half, 1), pltpu.roll(x, half, 1))
    return x * c + partner * s


def _segment_specs(bounds):
    return [pl.BlockSpec((TM, D_MODEL), lambda i, lo=lo, hi=hi: (jnp.clip(i - lo, 0, hi - lo - 1), 0))
            for lo, hi in zip(bounds[:-1], bounds[1:])]


def _ffn_inproj_kernel(*refs, bounds):
    n_src = len(bounds) - 1
    srcs = refs[:n_src]
    (cda_ref, sda_ref, cm_ref, sm_ref, ga_ref, gb_ref,
     wg_ref, wu_ref, wd_ref, win_ref, gq_ref, gkv_ref, wuq_ref, wukv_ref,
     h1_ref, naq_ref, nak_ref, nav_ref, daq_ref, dak_ref, dav_ref,
     mq_ref, mk_ref, mv_ref) = refs[n_src:]
    i = pl.program_id(0)
    x = srcs[-1][...]
    for sgm in range(n_src - 2, -1, -1):
        x = jnp.where(i < bounds[sgm + 1], srcs[sgm][...], x)
    xn = _rms(x, ga_ref[...]).astype(jnp.bfloat16)
    h1 = x + _swiglu_half(xn, wg_ref, wu_ref, wd_ref)
    h1_ref[...] = h1
    xn2 = _rms(h1, gb_ref[...]).astype(jnp.bfloat16)

    u = jnp.dot(xn2, win_ref[:, 0:3 * NA_W], preferred_element_type=jnp.float32)
    naq_ref[0] = (u[:, 0:NA_W] * (NA_HD ** -0.5 * LOG2E)).T.astype(jnp.bfloat16)
    nak_ref[...] = u[:, NA_W:2 * NA_W].astype(jnp.bfloat16)
    vt = u[:, 2 * NA_W:3 * NA_W].T.astype(jnp.bfloat16)
    for c in range(TM // LANES):
        nav_ref[c] = vt[:, c * LANES:(c + 1) * LANES]

    o0 = 3 * NA_W
    u = jnp.dot(xn2, win_ref[:, o0:o0 + 2 * DA_QK_W + DA_W], preferred_element_type=jnp.float32)
    cda, sda = cda_ref[...], sda_ref[...]
    q = jnp.concatenate([_rope_chunk(u[:, c * LANES:(c + 1) * LANES], cda, sda, DA_ROT // 2, DA_HD)
                         for c in range(DA_QK_W // LANES)], axis=1)
    k = jnp.concatenate([_rope_chunk(u[:, DA_QK_W + c * LANES:DA_QK_W + (c + 1) * LANES], cda, sda, DA_ROT // 2, DA_HD)
                         for c in range(DA_QK_W // LANES)], axis=1)
    daq_ref[0] = (q * (DA_HD ** -0.5 * LOG2E)).T.astype(jnp.bfloat16)
    dak_ref[...] = k.astype(jnp.bfloat16)
    vt = u[:, 2 * DA_QK_W:2 * DA_QK_W + DA_W].T.astype(jnp.bfloat16)
    for c in range(TM // TK):
        dav_ref[c] = vt[:, c * TK:(c + 1) * TK]

    o1 = o0 + 2 * DA_QK_W + DA_W
    u = jnp.dot(xn2, win_ref[:, o1:IN_W_PAD], preferred_element_type=jnp.float32)
    cm, sm = cm_ref[...], sm_ref[...]
    cq = _rms(u[:, 0:Q_LORA], gq_ref[...]).astype(jnp.bfloat16)
    qm = jnp.dot(cq, wuq_ref[...], preferred_element_type=jnp.float32)
    nope_w = MLA_HEADS * MLA_NOPE
    q_rope = _rope_chunk(qm[:, nope_w:nope_w + LANES], cm, sm, MLA_ROPE // 2, MLA_ROPE)
    qm = jnp.concatenate([qm[:, 0:nope_w], q_rope], axis=1) * ((MLA_NOPE + MLA_ROPE) ** -0.5 * LOG2E)
    mq_ref[0] = qm.T.astype(jnp.bfloat16)
    ckv = _rms(u[:, Q_LORA:Q_LORA + KV_LORA], gkv_ref[...]).astype(jnp.bfloat16)
    kv = jnp.dot(ckv, wukv_ref[...], preferred_element_type=jnp.float32)
    kr = _rope_chunk(u[:, Q_LORA + KV_LORA:Q_LORA + KV_LORA + LANES], cm, sm, MLA_ROPE // 2, MLA_ROPE)
    kf = jnp.concatenate([kv[:, 0:LANES], kr, kv[:, LANES:2 * LANES], kr], axis=1)
    mk_ref[...] = kf.astype(jnp.bfloat16)
    vt = kv[:, nope_w:nope_w + MLA_W].T.astype(jnp.bfloat16)
    for c in range(TM // TK):
        mv_ref[c] = vt[:, c * TK:(c + 1) * TK]


def _outproj_ffn_kernel(h_ref, ona_ref, oda_ref, om_ref, wo1_ref, wo2_ref, wo3_ref, g_ref,
                        wg_ref, wu_ref, wd_ref, gf_ref, *o_refs, out_bounds):
    h = h_ref[...]
    h = h + jnp.dot(ona_ref[...], wo1_ref[...], preferred_element_type=jnp.float32)
    h = h + jnp.dot(oda_ref[...], wo2_ref[...], preferred_element_type=jnp.float32)
    h = h + jnp.dot(om_ref[...], wo3_ref[...], preferred_element_type=jnp.float32)
    xn = _rms(h, g_ref[...]).astype(jnp.bfloat16)
    h = h + _swiglu_half(xn, wg_ref, wu_ref, wd_ref)
    if out_bounds is None:
        o_refs[0][...] = h
    else:
        h = _rms(h, gf_ref[...])
        i = pl.program_id(0)
        for o_ref, lo, hi in zip(o_refs, out_bounds[:-1], out_bounds[1:]):
            @pl.when((i >= lo) & (i < hi))
            def _(o_ref=o_ref):
                o_ref[...] = h


def _const_spec(shape):
    nd = len(shape)
    return pl.BlockSpec(shape, lambda i: (0,) * nd, pipeline_mode=pl.Buffered(1))


def _ffn_inproj(srcs, bounds, tabs, ga, gb, wg, wu, wd, win, gq, gkv, wuq, wukv):
    nt = bounds[-1]
    rows = nt * TM
    row_spec = lambda w: pl.BlockSpec((TM, w), lambda i: (i, 0))
    tile_spec = lambda r: pl.BlockSpec((1, r, TM), lambda i: (i, 0, 0))
    bf = jnp.bfloat16
    out_shape = (
        jax.ShapeDtypeStruct((rows, D_MODEL), jnp.float32),
        jax.ShapeDtypeStruct((nt, NA_W, TM), bf),
        jax.ShapeDtypeStruct((rows, NA_W), bf),
        jax.ShapeDtypeStruct((rows // LANES, NA_W, LANES), bf),
        jax.ShapeDtypeStruct((nt, DA_QK_W, TM), bf),
        jax.ShapeDtypeStruct((rows, DA_QK_W), bf),
        jax.ShapeDtypeStruct((rows // TK, DA_W, TK), bf),
        jax.ShapeDtypeStruct((nt, MLA_HEADS * (MLA_NOPE + MLA_ROPE), TM), bf),
        jax.ShapeDtypeStruct((rows, 2 * MXU_DIM), bf),
        jax.ShapeDtypeStruct((rows // TK, MLA_W, TK), bf),
    )
    out_specs = (
        row_spec(D_MODEL),
        tile_spec(NA_W),
        row_spec(NA_W),
        pl.BlockSpec((TM // LANES, NA_W, LANES), lambda i: (i, 0, 0)),
        tile_spec(DA_QK_W),
        row_spec(DA_QK_W),
        pl.BlockSpec((TM // TK, DA_W, TK), lambda i: (i, 0, 0)),
        tile_spec(MLA_HEADS * (MLA_NOPE + MLA_ROPE)),
        row_spec(2 * MXU_DIM),
        pl.BlockSpec((TM // TK, MLA_W, TK), lambda i: (i, 0, 0)),
    )
    in_specs = _segment_specs(bounds) + [row_spec(LANES)] * 4 + [
        _const_spec(a.shape) for a in (ga, gb, wg, wu, wd, win, gq, gkv, wuq, wukv)]
    return pl.pallas_call(
        functools.partial(_ffn_inproj_kernel, bounds=bounds),
        grid=(nt,),
        in_specs=in_specs,
        out_specs=out_specs,
        out_shape=out_shape,
        compiler_params=pltpu.CompilerParams(dimension_semantics=("arbitrary",),
                                             vmem_limit_bytes=VMEM_LIMIT_BYTES),
        name="ffn_inproj",
    )(*srcs, *tabs, ga, gb, wg, wu, wd, win, gq, gkv, wuq, wukv)


def _outproj_ffn(h, ona, oda, om, wo1, wo2, wo3, g, wg, wu, wd, gf, out_bounds):
    rows = h.shape[0]
    row_spec = lambda w: pl.BlockSpec((TM, w), lambda i: (i, 0))
    if out_bounds is None:
        out_specs = row_spec(D_MODEL)
        out_shape = jax.ShapeDtypeStruct((rows, D_MODEL), jnp.float32)
    else:
        out_specs = tuple(_segment_specs(out_bounds))
        out_shape = tuple(jax.ShapeDtypeStruct(((hi - lo) * TM, D_MODEL), jnp.float32)
                          for lo, hi in zip(out_bounds[:-1], out_bounds[1:]))
    in_specs = [row_spec(D_MODEL), row_spec(NA_W), row_spec(DA_W), row_spec(MLA_W)] + [
        _const_spec(a.shape) for a in (wo1, wo2, wo3, g, wg, wu, wd, gf)]
    return pl.pallas_call(
        functools.partial(_outproj_ffn_kernel, out_bounds=out_bounds),
        grid=(rows // TM,),
        in_specs=in_specs,
        out_specs=out_specs,
        out_shape=out_shape,
        compiler_params=pltpu.CompilerParams(dimension_semantics=("arbitrary",),
                                             vmem_limit_bytes=VMEM_LIMIT_BYTES),
        name="outproj_ffn",
    )(h, ona, oda, om, wo1, wo2, wo3, g, wg, wu, wd, gf)


def _row_band(block, lo, hi):
    row = lax.broadcasted_iota(jnp.int32, block.shape, 0)
    return jnp.where((row >= lo) & (row < hi), block, jnp.zeros_like(block))


def _with_ones(vt):
    return jnp.concatenate([vt, jnp.ones((ONES_ROWS, vt.shape[1]), vt.dtype)], axis=0)


def _meta_key_mask(b):
    row = lax.broadcasted_iota(jnp.int32, (LANES, 1), 0)
    lo = (b % (LANES // N_META)) * N_META
    return jnp.where((row >= lo) & (row < lo + N_META), 0.0, NEG).astype(jnp.float32)


class FlashCfg(NamedTuple):
    n_soft: int
    group_w: int
    soft_per_group: int
    v_of_soft: tuple
    out_w: int
    unroll: int


DA_CFG = FlashCfg(2 * DA_HEADS, LANES, 4, tuple(i // 2 for i in range(2 * DA_HEADS)), DA_W, 4)
MLA_CFG = FlashCfg(MLA_HEADS, MXU_DIM, 2, tuple(range(MLA_HEADS)), MLA_W, 8)


def _build_rhs(cfg, qt_ref, rhs_ref):
    for i in range(cfg.n_soft):
        g, j = i // cfg.soft_per_group, i % cfg.soft_per_group
        if cfg is DA_CFG:
            rhs_ref[i] = _row_band(qt_ref[0, g * LANES:(g + 1) * LANES, :], j * DA_HD, (j + 1) * DA_HD)
        else:
            nope = _row_band(qt_ref[0, g * LANES:(g + 1) * LANES, :], j * MLA_NOPE, (j + 1) * MLA_NOPE)
            rope0 = MLA_HEADS * MLA_NOPE
            rope = _row_band(qt_ref[0, rope0:rope0 + LANES, :], i * MLA_ROPE, (i + 1) * MLA_ROPE)
            rhs_ref[i] = jnp.concatenate([nope, rope], axis=0)


def _flash_tiles(cfg, tiles, rhs_ref, m_ref, acc_ref):
    work = [(t, i) for t in range(len(tiles)) for i in range(cfg.n_soft)]

    def score(w):
        kblk_of, _, mask = tiles[w[0]]
        s = jnp.dot(kblk_of(w[1] // cfg.soft_per_group), rhs_ref[w[1]], preferred_element_type=jnp.float32)
        return s if mask is None else s + mask

    pending = [score(w) for w in work[:LOOKAHEAD]]
    for n, (t, i) in enumerate(work):
        vt_of = tiles[t][1]
        s = pending.pop(0)
        if n + LOOKAHEAD < len(work):
            pending.append(score(work[n + LOOKAHEAD]))
        m_prev = m_ref[i]
        m_new = jnp.maximum(m_prev, jnp.max(s, axis=0, keepdims=True))
        alpha = jnp.exp2(m_prev - m_new)
        p = jnp.exp2(s - m_new).astype(jnp.bfloat16)
        pv = jnp.dot(_with_ones(vt_of(cfg.v_of_soft[i])), p, preferred_element_type=jnp.float32)
        acc_ref[i] = alpha * acc_ref[i] + pv
        m_ref[i] = m_new


def _flash_kernel(*refs, cfg, n_keys, lam_init, meta_queries):
    if cfg is DA_CFG:
        (qt_ref, kf_ref, vt_ref, kfm_ref, vtm_ref, lam_ref, gs_ref, o_ref,
         rhs_ref, m_ref, acc_ref, *rest) = refs
    else:
        qt_ref, kf_ref, vt_ref, kfm_ref, vtm_ref, o_ref, rhs_ref, m_ref, acc_ref, *rest = refs
    b = pl.program_id(0)
    tq = qt_ref.shape[2]
    vd = DA_VD
    _build_rhs(cfg, qt_ref, rhs_ref)
    m_ref[...] = jnp.full(m_ref.shape, NEG, jnp.float32)
    acc_ref[...] = jnp.zeros(acc_ref.shape, jnp.float32)

    unroll = math.gcd(cfg.unroll, n_keys // TK)

    def body(it, carry):
        tiles = []
        for u in range(unroll):
            kt = it * unroll + u
            k0 = pl.multiple_of(kt * TK, TK)
            tiles.append((
                lambda g, k0=k0: kf_ref[pl.ds(k0, TK), g * cfg.group_w:(g + 1) * cfg.group_w],
                lambda h, kt=kt: vt_ref[kt, h * vd:(h + 1) * vd, :],
                None))
        _flash_tiles(cfg, tiles, rhs_ref, m_ref, acc_ref)
        return carry

    lax.fori_loop(0, n_keys // (TK * unroll), body, 0)
    _flash_tiles(
        cfg,
        [(lambda g: kfm_ref[:, g * cfg.group_w:(g + 1) * cfg.group_w],
          lambda h: vtm_ref[0, h * vd:(h + 1) * vd, :],
          _meta_key_mask(b))],
        rhs_ref, m_ref, acc_ref)

    heads = []
    if cfg is DA_CFG:
        lp = lam_ref[...]
        lam = (jnp.exp(jnp.sum(lp[0:1] * lp[1:2], axis=1, keepdims=True))
               - jnp.exp(jnp.sum(lp[2:3] * lp[3:4], axis=1, keepdims=True)) + lam_init)
        for h in range(DA_HEADS):
            a1, a2 = acc_ref[2 * h], acc_ref[2 * h + 1]
            o = a1[0:vd] / a1[vd:vd + 1] - lam * (a2[0:vd] / a2[vd:vd + 1])
            o = o * lax.rsqrt(jnp.mean(o * o, axis=0, keepdims=True) + EPS) * gs_ref[...]
            heads.append(o * (1.0 - lam_init))
    else:
        for h in range(MLA_HEADS):
            a = acc_ref[h]
            heads.append(a[0:vd] / a[vd:vd + 1])
    pairs = [jnp.concatenate(heads[2 * g:2 * g + 2], axis=0).T for g in range(len(heads) // 2)]
    out = jnp.concatenate(pairs, axis=1)
    if meta_queries:
        stage_ref = rest[0]
        stage_ref[...] = out
        r0 = pl.multiple_of((b % (LANES // N_META)) * N_META, N_META)
        o_ref[...] = stage_ref[pl.ds(r0, N_META), :].astype(o_ref.dtype)
    else:
        o_ref[...] = out.astype(o_ref.dtype)


def _flash_call(cfg, grp, qt, kf, vt, extra, o_prev, lam_init, meta_queries, name):
    n, bsz = grp.n, grp.batch
    qw, kw, vw = qt.shape[1], kf.shape[1], vt.shape[1]
    tiles_per_seq = n // TM
    meta_blk0 = grp.meta0 // LANES
    per_blk = LANES // N_META
    if meta_queries:
        grid = (bsz,)
        tq = LANES
        q_map = lambda b: ((meta_blk0 + b // per_blk) // (TM // LANES), 0, (meta_blk0 + b // per_blk) % (TM // LANES))
        o_spec = pl.BlockSpec((N_META, cfg.out_w), lambda b: (grp.meta0 // N_META + b, 0))
        fix = lambda f: (lambda b: f(b))
        sem = ("arbitrary",)
    else:
        grid = (bsz, tiles_per_seq)
        tq = TM
        q_map = lambda b, i: (grp.real0 // TM + b * tiles_per_seq + i, 0, 0)
        o_spec = pl.BlockSpec((TM, cfg.out_w), lambda b, i: (grp.real0 // TM + b * tiles_per_seq + i, 0))
        fix = lambda f: (lambda b, i: f(b))
        sem = ("arbitrary", "arbitrary")
    in_specs = [
        pl.BlockSpec((1, qw, tq), q_map),
        pl.BlockSpec((n, kw), fix(lambda b: (grp.real0 // n + b, 0))),
        pl.BlockSpec((n // TK, vw, TK), fix(lambda b: (grp.real0 // n + b, 0, 0))),
        pl.BlockSpec((LANES, kw), fix(lambda b: (meta_blk0 + b // per_blk, 0))),
        pl.BlockSpec((1, vw, LANES), fix(lambda b: ((meta_blk0 + b // per_blk) // (TK // LANES), 0,
                                                     (meta_blk0 + b // per_blk) % (TK // LANES)))),
    ]
    args = [qt, kf, vt, kf, vt]
    for a in extra:
        in_specs.append(pl.BlockSpec(a.shape, fix(lambda b, nd=a.ndim: (0,) * nd)))
        args.append(a)
    in_specs.append(pl.BlockSpec(memory_space=pl.ANY))
    args.append(o_prev)
    scratch = [
        pltpu.VMEM((cfg.n_soft, cfg.group_w, tq), jnp.bfloat16),
        pltpu.VMEM((cfg.n_soft, 1, tq), jnp.float32),
        pltpu.VMEM((cfg.n_soft, DA_VD + ONES_ROWS, tq), jnp.float32),
    ]
    if meta_queries:
        scratch.append(pltpu.VMEM((LANES, cfg.out_w), jnp.float32))

    def body(*refs):
        n_in = len(args)
        ins, rest = refs[:n_in - 1], refs[n_in:]
        _flash_kernel(*ins, *rest, cfg=cfg, n_keys=n, lam_init=lam_init, meta_queries=meta_queries)

    return pl.pallas_call(
        body,
        grid=grid,
        in_specs=in_specs,
        out_specs=o_spec,
        out_shape=jax.ShapeDtypeStruct(o_prev.shape, o_prev.dtype),
        scratch_shapes=scratch,
        input_output_aliases={len(args) - 1: 0},
        compiler_params=pltpu.CompilerParams(dimension_semantics=sem, vmem_limit_bytes=VMEM_LIMIT_BYTES),
        name=name,
    )(*args)


def _na_kernel(qt_ref, kf_ref, vt_ref, kfm_ref, vtm_ref, bias_ref, prev_ref, o_ref, *, grid_rows):
    del prev_ref
    b, i = pl.program_id(0), pl.program_id(1)
    ws = jnp.clip(i * NA_QROWS - WIN_R // 2, 0, grid_rows - NA_KROWS)
    kwin = kf_ref[pl.ds(pl.multiple_of(ws * GRID_W, 2 * GRID_W), NA_KROWS * GRID_W), :]
    vwin = vt_ref[pl.ds(ws // 2, NA_KROWS // 2)]
    kmeta = kfm_ref[...]
    mmask = _meta_key_mask(b)
    heads = []
    for h in range(NA_HEADS):
        g, j = h // 2, h % 2
        rhs = _row_band(qt_ref[0, g * LANES:(g + 1) * LANES, :], j * NA_HD, (j + 1) * NA_HD)
        s1 = jnp.dot(kwin[:, g * LANES:(g + 1) * LANES], rhs, preferred_element_type=jnp.float32) + bias_ref[0, h]
        s2 = jnp.dot(kmeta[:, g * LANES:(g + 1) * LANES], rhs, preferred_element_type=jnp.float32) + mmask
        m = jnp.maximum(jnp.max(s1, axis=0, keepdims=True), jnp.max(s2, axis=0, keepdims=True))
        p1 = jnp.exp2(s1 - m).astype(jnp.bfloat16)
        p2 = jnp.exp2(s2 - m).astype(jnp.bfloat16)
        v1 = jnp.concatenate([vwin[c, h * NA_HD:(h + 1) * NA_HD, :] for c in range(NA_KROWS // 2)], axis=1)
        acc = (jnp.dot(_with_ones(v1), p1, preferred_element_type=jnp.float32)
               + jnp.dot(_with_ones(vtm_ref[0, h * NA_HD:(h + 1) * NA_HD, :]), p2,
                         preferred_element_type=jnp.float32))
        heads.append(acc[0:NA_HD] / acc[NA_HD:NA_HD + 1])
    pairs = [jnp.concatenate(heads[2 * g:2 * g + 2], axis=0).T for g in range(NA_HEADS // 2)]
    o_ref[...] = jnp.concatenate(pairs, axis=1).astype(o_ref.dtype)


def _na_meta_kernel(qt_ref, kf_ref, vt_ref, prev_ref, o_ref):
    del prev_ref
    kr = lax.broadcasted_iota(jnp.int32, (LANES, LANES), 0) // N_META
    qc = lax.broadcasted_iota(jnp.int32, (LANES, LANES), 1) // N_META
    mask = jnp.where(kr == qc, 0.0, NEG).astype(jnp.float32)
    kf = kf_ref[...]
    heads = []
    for h in range(NA_HEADS):
        g, j = h // 2, h % 2
        rhs = _row_band(qt_ref[0, g * LANES:(g + 1) * LANES, :], j * NA_HD, (j + 1) * NA_HD)
        s = jnp.dot(kf[:, g * LANES:(g + 1) * LANES], rhs, preferred_element_type=jnp.float32) + mask
        p = jnp.exp2(s - jnp.max(s, axis=0, keepdims=True)).astype(jnp.bfloat16)
        acc = jnp.dot(_with_ones(vt_ref[0, h * NA_HD:(h + 1) * NA_HD, :]), p, preferred_element_type=jnp.float32)
        heads.append(acc[0:NA_HD] / acc[NA_HD:NA_HD + 1])
    pairs = [jnp.concatenate(heads[2 * g:2 * g + 2], axis=0).T for g in range(NA_HEADS // 2)]
    o_ref[...] = jnp.concatenate(pairs, axis=1).astype(o_ref.dtype)


def _na_call(grp, qt, kf, vt, bias, o_prev):
    n, bsz = grp.n, grp.batch
    grid_rows = n // GRID_W
    steps = grid_rows // NA_QROWS
    tq = NA_QROWS * GRID_W
    meta_blk0 = grp.meta0 // LANES
    per_blk = LANES // N_META
    in_specs = [
        pl.BlockSpec((1, NA_W, tq), lambda b, i: (grp.real0 // TM + b * (n // TM) + i // (TM // tq), 0, i % (TM // tq))),
        pl.BlockSpec((n, NA_W), lambda b, i: (grp.real0 // n + b, 0)),
        pl.BlockSpec((n // LANES, NA_W, LANES), lambda b, i: (grp.real0 // n + b, 0, 0)),
        pl.BlockSpec((LANES, NA_W), lambda b, i: (meta_blk0 + b // per_blk, 0)),
        pl.BlockSpec((1, NA_W, LANES), lambda b, i: (meta_blk0 + b // per_blk, 0, 0)),
        pl.BlockSpec((1, NA_HEADS, NA_KROWS * GRID_W, tq),
                     lambda b, i: (jnp.where(i == 0, 0, jnp.where(i == steps - 1, 2, 1)), 0, 0, 0)),
        pl.BlockSpec(memory_space=pl.ANY),
    ]
    return pl.pallas_call(
        functools.partial(_na_kernel, grid_rows=grid_rows),
        grid=(bsz, steps),
        in_specs=in_specs,
        out_specs=pl.BlockSpec((tq, NA_W), lambda b, i: (grp.real0 // tq + b * steps + i, 0)),
        out_shape=jax.ShapeDtypeStruct(o_prev.shape, o_prev.dtype),
        input_output_aliases={6: 0},
        compiler_params=pltpu.CompilerParams(dimension_semantics=("arbitrary", "arbitrary"),
                                             vmem_limit_bytes=VMEM_LIMIT_BYTES),
        name="na_real",
    )(qt, kf, vt, kf, vt, bias, o_prev)


def _na_meta_call(layout, qt, kf, vt, o_prev):
    blk0 = layout.groups[0].meta0 // LANES
    nblk = sum(_round_up(g.batch * N_META, LANES) for g in layout.groups) // LANES
    sub = TM // LANES
    return pl.pallas_call(
        _na_meta_kernel,
        grid=(nblk,),
        in_specs=[
            pl.BlockSpec((1, NA_W, LANES), lambda i: ((blk0 + i) // sub, 0, (blk0 + i) % sub)),
            pl.BlockSpec((LANES, NA_W), lambda i: (blk0 + i, 0)),
            pl.BlockSpec((1, NA_W, LANES), lambda i: (blk0 + i, 0, 0)),
            pl.BlockSpec(memory_space=pl.ANY),
        ],
        out_specs=pl.BlockSpec((LANES, NA_W), lambda i: (blk0 + i, 0)),
        out_shape=jax.ShapeDtypeStruct(o_prev.shape, o_prev.dtype),
        input_output_aliases={3: 0},
        compiler_params=pltpu.CompilerParams(dimension_semantics=("arbitrary",),
                                             vmem_limit_bytes=VMEM_LIMIT_BYTES),
        name="na_meta",
    )(qt, kf, vt, o_prev)


def _na_bias_tables(rel_bias):
    kc = np.arange(GRID_W)[:, None]
    qc = np.arange(GRID_W)[None, :]
    cs = np.clip(qc - WIN_C // 2, 0, GRID_W - WIN_C)
    col_ok = (kc >= cs) & (kc < cs + WIN_C)
    cidx = np.clip(kc - qc + WIN_C - 1, 0, 2 * WIN_C - 2)
    planes = jnp.where(col_ok[None, None], rel_bias[:, :, cidx] * LOG2E, NEG)
    masked = 2 * WIN_R - 1
    planes = jnp.concatenate([planes, jnp.full((NA_HEADS, 1, GRID_W, GRID_W), NEG, planes.dtype)], axis=1)
    kr = np.arange(NA_KROWS)[:, None]
    qr = np.arange(NA_QROWS)[None, :]
    ridx = []
    for win_lo, q_off in ((0 * qr, qr), (qr, qr + WIN_R // 2), (0 * qr + NA_KROWS - WIN_R, qr + NA_KROWS - NA_QROWS)):
        row_ok = (kr >= win_lo) & (kr < win_lo + WIN_R)
        ridx.append(np.where(row_ok, kr - q_off + WIN_R - 1, masked))
    ridx = np.stack(ridx)
    assert ridx.min() >= 0 and ridx.max() <= masked
    t = planes[:, ridx]
    t = jnp.transpose(t, (1, 0, 2, 4, 3, 5))
    return t.reshape(3, NA_HEADS, NA_KROWS * GRID_W, NA_QROWS * GRID_W).astype(jnp.float32)


def _rope_tables(layout):
    pos = np.zeros((layout.rows,), np.float32)
    for g in layout.groups:
        r = np.arange(g.batch * g.n)
        pos[g.real0:g.real0 + g.batch * g.n] = N_META + r % g.n
        m = np.arange(g.batch * N_META)
        pos[g.meta0:g.meta0 + g.batch * N_META] = m % N_META
    pos = jnp.asarray(pos)

    def tables(dim, period):
        inv = ROPE_THETA ** (-(jnp.arange(0, dim, 2, dtype=jnp.float32) / dim))
        ang = pos[:, None] * inv[None, :]
        cos, sin = jnp.cos(ang), jnp.sin(ang)
        rest = period - dim
        c = jnp.concatenate([cos, cos, jnp.ones((layout.rows, rest), jnp.float32)], axis=1)
        s = jnp.concatenate([-sin, sin, jnp.zeros((layout.rows, rest), jnp.float32)], axis=1)
        reps = LANES // period
        return jnp.tile(c, (1, reps)), jnp.tile(s, (1, reps))

    cda, sda = tables(DA_ROT, DA_HD)
    cm, sm = tables(MLA_ROPE, MLA_ROPE)
    return cda, sda, cm, sm


def _prep_layer(l, norm_g, ffn_w_gate, ffn_w_up, ffn_w_down, w_in, w_out, mla_q_norm_g, mla_kv_norm_g,
                mla_w_uq, mla_w_ukv, da_subln_g):
    bf = jnp.bfloat16
    kr0 = IN_W - MLA_ROPE
    win = jnp.concatenate([w_in[l, :, :kr0]] + [w_in[l, :, kr0:]] * (LANES // MLA_ROPE), axis=1).astype(bf)
    uq = mla_w_uq[l].reshape(Q_LORA, MLA_HEADS, MLA_NOPE + MLA_ROPE)
    uq = jnp.concatenate([uq[:, :, :MLA_NOPE].reshape(Q_LORA, -1), uq[:, :, MLA_NOPE:].reshape(Q_LORA, -1)], axis=1)
    ukv = mla_w_ukv[l].reshape(KV_LORA, MLA_HEADS, MLA_NOPE + MLA_VD)
    ukv = jnp.concatenate([ukv[:, :, :MLA_NOPE].reshape(KV_LORA, -1), ukv[:, :, MLA_NOPE:].reshape(KV_LORA, -1)], axis=1)
    row = lambda v: v.reshape(1, -1)
    return dict(
        g=[row(norm_g[l, i]) for i in range(3)],
        ffn=[(ffn_w_gate[l, i].astype(bf), ffn_w_up[l, i].astype(bf), ffn_w_down[l, i].astype(bf)) for i in range(2)],
        win=win, uq=uq.astype(bf), ukv=ukv.astype(bf),
        gq=row(mla_q_norm_g[l]), gkv=row(mla_kv_norm_g[l]),
        wo=(w_out[l, :NA_W].astype(bf), w_out[l, NA_W:NA_W + DA_W].astype(bf), w_out[l, NA_W + DA_W:].astype(bf)),
        gs=da_subln_g[l].reshape(DA_VD, 1),
    )


def kernel(x_prompt, x_sample, meta_tokens, norm_g, final_norm_g, ffn_w_gate, ffn_w_up, ffn_w_down, w_in, w_out, na_rel_bias, da_lambda, da_subln_g, mla_q_norm_g, mla_kv_norm_g, mla_w_uq, mla_w_ukv):
    xs = (x_prompt, x_sample)
    layout = _make_layout([(x.shape[0], x.shape[1]) for x in xs])
    real_rows = sum(g.batch * g.n for g in layout.groups)
    tail = []
    for g in layout.groups:
        blk = jnp.tile(meta_tokens.astype(jnp.float32), (g.batch, 1))
        tail.append(jnp.pad(blk, ((0, _round_up(g.batch * N_META, LANES) - g.batch * N_META), (0, 0))))
    tail = jnp.concatenate(tail, axis=0)
    tail = jnp.pad(tail, ((0, layout.rows - real_rows - tail.shape[0]), (0, 0)))
    srcs = [x.reshape(-1, D_MODEL) for x in xs] + [tail]
    real_bounds = tuple(g.real0 // TM for g in layout.groups) + (real_rows // TM,)
    bounds = real_bounds + (layout.rows // TM,)
    tabs = _rope_tables(layout)
    gf = final_norm_g.reshape(1, -1)

    for l in range(DEPTH):
        p = _prep_layer(l, norm_g, ffn_w_gate, ffn_w_up, ffn_w_down, w_in, w_out, mla_q_norm_g, mla_kv_norm_g,
                        mla_w_uq, mla_w_ukv, da_subln_g)
        lam_init = 0.8 - 0.6 * math.exp(-0.3 * l)
        (h, naq, nak, nav, daq, dak, dav, mq, mk, mv) = _ffn_inproj(
            srcs, bounds, tabs, p["g"][0], p["g"][1], *p["ffn"][0], p["win"], p["gq"], p["gkv"], p["uq"], p["ukv"])
        bias = _na_bias_tables(na_rel_bias[l])
        o_na = jnp.zeros((layout.rows, NA_W), jnp.bfloat16)
        o_da = jnp.zeros((layout.rows, DA_W), jnp.bfloat16)
        o_m = jnp.zeros((layout.rows, MLA_W), jnp.bfloat16)
        o_na = _na_meta_call(layout, naq, nak, nav, o_na)
        for gi, grp in enumerate(layout.groups):
            o_na = _na_call(grp, naq, nak, nav, bias, o_na)
            for meta_q in (False, True):
                tag = f"g{gi}_{'meta' if meta_q else 'real'}"
                o_da = _flash_call(DA_CFG, grp, daq, dak, dav, (da_lambda[l], p["gs"]), o_da, lam_init, meta_q,
                                   "da_" + tag)
                o_m = _flash_call(MLA_CFG, grp, mq, mk, mv, (), o_m, lam_init, meta_q, "mla_" + tag)
        last = l == DEPTH - 1
        h = _outproj_ffn(h, o_na, o_da, o_m, *p["wo"], p["g"][2], *p["ffn"][1], gf, real_bounds if last else None)
        srcs, bounds = [h], (0, layout.rows // TM)

    return tuple(y.reshape(x.shape) for x, y in zip(xs, h))
```

```python
import functools
import math
from typing import NamedTuple

import jax
import jax.numpy as jnp
import numpy as np
from jax import lax
from jax.experimental import pallas as pl
from jax.experimental.pallas import tpu as pltpu

D_MODEL = 1024
DEPTH = 2
GRID_W = 64
N_META = 16
WIN_R = 8
WIN_C = 16
NA_HEADS = 6
NA_HD = 64
DA_HEADS = 6
DA_HD = 32
DA_VD = 64
DA_ROT = DA_HD // 4
MLA_HEADS = 4
MLA_NOPE = 64
MLA_ROPE = 32
MLA_VD = 64
Q_LORA = 256
KV_LORA = 128
ROPE_THETA = 500000.0
D_FF = 2816
EPS = 1e-6
NA_W = NA_HEADS * NA_HD
DA_W = DA_HEADS * DA_VD
MLA_W = MLA_HEADS * MLA_VD
DA_QK_W = DA_HEADS * 2 * DA_HD
IN_W = 3 * NA_W + 2 * DA_QK_W + DA_W + Q_LORA + KV_LORA + MLA_ROPE

LANES = 128
MXU_DIM = 256
VMEM_LIMIT_BYTES = 58 * 1024 * 1024

TM = 512
TK = 256
NA_QROWS = 4
NA_KROWS = NA_QROWS + WIN_R
LOOKAHEAD = 2
FF_SPLIT = 1536
IN_W_PAD = 2816

LOG2E = 1.4426950408889634
NEG = -1e30
ONES_ROWS = 16
STALE_MARGIN = 60.0
NORM_SLACK = 1.05


def _round_up(x, m):
    return (x + m - 1) // m * m


class Group(NamedTuple):
    batch: int
    n: int
    real0: int
    meta0: int


class Layout(NamedTuple):
    groups: tuple
    rows: int


def _make_layout(shapes):
    row = 0
    real0 = []
    for b, n in shapes:
        assert n % TM == 0 and row % n == 0 and n % GRID_W == 0
        assert (n // GRID_W) % NA_QROWS == 0 and n // GRID_W >= NA_KROWS
        real0.append(row)
        row += b * n
    groups = []
    for (b, n), r0 in zip(shapes, real0):
        groups.append(Group(b, n, r0, row))
        row += _round_up(b * N_META, LANES)
    return Layout(tuple(groups), _round_up(row, TM))


def _rms(x, g):
    return x * lax.rsqrt(jnp.mean(x * x, axis=-1, keepdims=True) + EPS) * g


def _swiglu_half(xn, wg_ref, wu_ref, wd_ref):
    acc = None
    for lo, hi in ((0, FF_SPLIT), (FF_SPLIT, D_FF)):
        gate = jnp.dot(xn, wg_ref[:, lo:hi], preferred_element_type=jnp.float32)
        up = jnp.dot(xn, wu_ref[:, lo:hi], preferred_element_type=jnp.float32)
        hm = (gate * jax.nn.sigmoid(gate) * up).astype(jnp.bfloat16)
        part = jnp.dot(hm, wd_ref[lo:hi, :], preferred_element_type=jnp.float32)
        acc = part if acc is None else acc + part
    return 0.5 * acc


def _rope_chunk(x, c, s, half, period):
    lane = lax.broadcasted_iota(jnp.int32, x.shape, 1)
    lo = (lane & (period - 1)) < half
    partner = jnp.where(lo, pltpu.roll(x, LANES - half, 1), pltpu.roll(x, half, 1))
    return x * c + partner * s


def _segment_specs(bounds):
    return [pl.BlockSpec((TM, D_MODEL), lambda i, lo=lo, hi=hi: (jnp.clip(i - lo, 0, hi - lo - 1), 0))
            for lo, hi in zip(bounds[:-1], bounds[1:])]


def _ffn_inproj_kernel(*refs, bounds):
    n_src = len(bounds) - 1
    srcs = refs[:n_src]
    (cda_ref, sda_ref, cm_ref, sm_ref, ga_ref, gb_ref,
     wg_ref, wu_ref, wd_ref, win_ref, gq_ref, gkv_ref, wuq_ref, wukv_ref,
     h1_ref, naq_ref, nak_ref, nav_ref, daq_ref, dak_ref, dav_ref,
     mq_ref, mk_ref, mv_ref) = refs[n_src:]
    i = pl.program_id(0)
    x = srcs[-1][...]
    for sgm in range(n_src - 2, -1, -1):
        x = jnp.where(i < bounds[sgm + 1], srcs[sgm][...], x)
    xn = _rms(x, ga_ref[...]).astype(jnp.bfloat16)
    h1 = x + _swiglu_half(xn, wg_ref, wu_ref, wd_ref)
    h1_ref[...] = h1
    xn2 = _rms(h1, gb_ref[...]).astype(jnp.bfloat16)

    u = jnp.dot(xn2, win_ref[:, 0:3 * NA_W], preferred_element_type=jnp.float32)
    naq_ref[0] = (u[:, 0:NA_W] * (NA_HD ** -0.5 * LOG2E)).T.astype(jnp.bfloat16)
    nak_ref[...] = u[:, NA_W:2 * NA_W].astype(jnp.bfloat16)
    vt = u[:, 2 * NA_W:3 * NA_W].T.astype(jnp.bfloat16)
    for c in range(TM // LANES):
        nav_ref[c] = vt[:, c * LANES:(c + 1) * LANES]

    o0 = 3 * NA_W
    u = jnp.dot(xn2, win_ref[:, o0:o0 + 2 * DA_QK_W + DA_W], preferred_element_type=jnp.float32)
    cda, sda = cda_ref[...], sda_ref[...]
    q = jnp.concatenate([_rope_chunk(u[:, c * LANES:(c + 1) * LANES], cda, sda, DA_ROT // 2, DA_HD)
                         for c in range(DA_QK_W // LANES)], axis=1)
    k = jnp.concatenate([_rope_chunk(u[:, DA_QK_W + c * LANES:DA_QK_W + (c + 1) * LANES], cda, sda, DA_ROT // 2, DA_HD)
                         for c in range(DA_QK_W // LANES)], axis=1)
    daq_ref[0] = (q * (DA_HD ** -0.5 * LOG2E)).T.astype(jnp.bfloat16)
    dak_ref[...] = k.astype(jnp.bfloat16)
    vt = u[:, 2 * DA_QK_W:2 * DA_QK_W + DA_W].T.astype(jnp.bfloat16)
    for c in range(TM // TK):
        dav_ref[c] = vt[:, c * TK:(c + 1) * TK]

    o1 = o0 + 2 * DA_QK_W + DA_W
    u = jnp.dot(xn2, win_ref[:, o1:IN_W_PAD], preferred_element_type=jnp.float32)
    cm, sm = cm_ref[...], sm_ref[...]
    cq = _rms(u[:, 0:Q_LORA], gq_ref[...]).astype(jnp.bfloat16)
    qm = jnp.dot(cq, wuq_ref[...], preferred_element_type=jnp.float32)
    nope_w = MLA_HEADS * MLA_NOPE
    q_rope = _rope_chunk(qm[:, nope_w:nope_w + LANES], cm, sm, MLA_ROPE // 2, MLA_ROPE)
    qm = jnp.concatenate([qm[:, 0:nope_w], q_rope], axis=1) * ((MLA_NOPE + MLA_ROPE) ** -0.5 * LOG2E)
    mq_ref[0] = qm.T.astype(jnp.bfloat16)
    ckv = _rms(u[:, Q_LORA:Q_LORA + KV_LORA], gkv_ref[...]).astype(jnp.bfloat16)
    kv = jnp.dot(ckv, wukv_ref[...], preferred_element_type=jnp.float32)
    kr = _rope_chunk(u[:, Q_LORA + KV_LORA:Q_LORA + KV_LORA + LANES], cm, sm, MLA_ROPE // 2, MLA_ROPE)
    kf = jnp.concatenate([kv[:, 0:LANES], kr, kv[:, LANES:2 * LANES], kr], axis=1)
    mk_ref[...] = kf.astype(jnp.bfloat16)
    vt = kv[:, nope_w:nope_w + MLA_W].T.astype(jnp.bfloat16)
    for c in range(TM // TK):
        mv_ref[c] = vt[:, c * TK:(c + 1) * TK]


def _outproj_ffn_kernel(h_ref, ona_ref, oda_ref, om_ref, wo1_ref, wo2_ref, wo3_ref, g_ref,
                        wg_ref, wu_ref, wd_ref, gf_ref, *o_refs, out_bounds):
    h = h_ref[...]
    h = h + jnp.dot(ona_ref[...], wo1_ref[...], preferred_element_type=jnp.float32)
    h = h + jnp.dot(oda_ref[...], wo2_ref[...], preferred_element_type=jnp.float32)
    h = h + jnp.dot(om_ref[...], wo3_ref[...], preferred_element_type=jnp.float32)
    xn = _rms(h, g_ref[...]).astype(jnp.bfloat16)
    h = h + _swiglu_half(xn, wg_ref, wu_ref, wd_ref)
    if out_bounds is None:
        o_refs[0][...] = h
    else:
        h = _rms(h, gf_ref[...])
        i = pl.program_id(0)
        for o_ref, lo, hi in zip(o_refs, out_bounds[:-1], out_bounds[1:]):
            @pl.when((i >= lo) & (i < hi))
            def _(o_ref=o_ref):
                o_ref[...] = h


def _const_spec(shape):
    nd = len(shape)
    return pl.BlockSpec(shape, lambda i: (0,) * nd, pipeline_mode=pl.Buffered(1))


def _ffn_inproj(srcs, bounds, tabs, ga, gb, wg, wu, wd, win, gq, gkv, wuq, wukv):
    nt = bounds[-1]
    rows = nt * TM
    row_spec = lambda w: pl.BlockSpec((TM, w), lambda i: (i, 0))
    tile_spec = lambda r: pl.BlockSpec((1, r, TM), lambda i: (i, 0, 0))
    bf = jnp.bfloat16
    out_shape = (
        jax.ShapeDtypeStruct((rows, D_MODEL), jnp.float32),
        jax.ShapeDtypeStruct((nt, NA_W, TM), bf),
        jax.ShapeDtypeStruct((rows, NA_W), bf),
        jax.ShapeDtypeStruct((rows // LANES, NA_W, LANES), bf),
        jax.ShapeDtypeStruct((nt, DA_QK_W, TM), bf),
        jax.ShapeDtypeStruct((rows, DA_QK_W), bf),
        jax.ShapeDtypeStruct((rows // TK, DA_W, TK), bf),
        jax.ShapeDtypeStruct((nt, MLA_HEADS * (MLA_NOPE + MLA_ROPE), TM), bf),
        jax.ShapeDtypeStruct((rows, 2 * MXU_DIM), bf),
        jax.ShapeDtypeStruct((rows // TK, MLA_W, TK), bf),
    )
    out_specs = (
        row_spec(D_MODEL),
        tile_spec(NA_W),
        row_spec(NA_W),
        pl.BlockSpec((TM // LANES, NA_W, LANES), lambda i: (i, 0, 0)),
        tile_spec(DA_QK_W),
        row_spec(DA_QK_W),
        pl.BlockSpec((TM // TK, DA_W, TK), lambda i: (i, 0, 0)),
        tile_spec(MLA_HEADS * (MLA_NOPE + MLA_ROPE)),
        row_spec(2 * MXU_DIM),
        pl.BlockSpec((TM // TK, MLA_W, TK), lambda i: (i, 0, 0)),
    )
    in_specs = _segment_specs(bounds) + [row_spec(LANES)] * 4 + [
        _const_spec(a.shape) for a in (ga, gb, wg, wu, wd, win, gq, gkv, wuq, wukv)]
    return pl.pallas_call(
        functools.partial(_ffn_inproj_kernel, bounds=bounds),
        grid=(nt,),
        in_specs=in_specs,
        out_specs=out_specs,
        out_shape=out_shape,
        compiler_params=pltpu.CompilerParams(dimension_semantics=("arbitrary",),
                                             vmem_limit_bytes=VMEM_LIMIT_BYTES),
        name="ffn_inproj",
    )(*srcs, *tabs, ga, gb, wg, wu, wd, win, gq, gkv, wuq, wukv)


def _outproj_ffn(h, ona, oda, om, wo1, wo2, wo3, g, wg, wu, wd, gf, out_bounds):
    rows = h.shape[0]
    row_spec = lambda w: pl.BlockSpec((TM, w), lambda i: (i, 0))
    if out_bounds is None:
        out_specs = row_spec(D_MODEL)
        out_shape = jax.ShapeDtypeStruct((rows, D_MODEL), jnp.float32)
    else:
        out_specs = tuple(_segment_specs(out_bounds))
        out_shape = tuple(jax.ShapeDtypeStruct(((hi - lo) * TM, D_MODEL), jnp.float32)
                          for lo, hi in zip(out_bounds[:-1], out_bounds[1:]))
    in_specs = [row_spec(D_MODEL), row_spec(NA_W), row_spec(DA_W), row_spec(MLA_W)] + [
        _const_spec(a.shape) for a in (wo1, wo2, wo3, g, wg, wu, wd, gf)]
    return pl.pallas_call(
        functools.partial(_outproj_ffn_kernel, out_bounds=out_bounds),
        grid=(rows // TM,),
        in_specs=in_specs,
        out_specs=out_specs,
        out_shape=out_shape,
        compiler_params=pltpu.CompilerParams(dimension_semantics=("arbitrary",),
                                             vmem_limit_bytes=VMEM_LIMIT_BYTES),
        name="outproj_ffn",
    )(h, ona, oda, om, wo1, wo2, wo3, g, wg, wu, wd, gf)


def _row_band(block, lo, hi):
    row = lax.broadcasted_iota(jnp.int32, block.shape, 0)
    return jnp.where((row >= lo) & (row < hi), block, jnp.zeros_like(block))


def _with_ones(vt):
    return jnp.concatenate([vt, jnp.ones((ONES_ROWS, vt.shape[1]), vt.dtype)], axis=0)


def _meta_key_mask(b):
    row = lax.broadcasted_iota(jnp.int32, (LANES, 1), 0)
    lo = (b % (LANES // N_META)) * N_META
    return jnp.where((row >= lo) & (row < lo + N_META), 0.0, NEG).astype(jnp.float32)


class FlashCfg(NamedTuple):
    n_soft: int
    group_w: int
    soft_per_group: int
    v_of_soft: tuple
    out_w: int
    unroll: int


DA_CFG = FlashCfg(2 * DA_HEADS, LANES, 4, tuple(i // 2 for i in range(2 * DA_HEADS)), DA_W, 4)
MLA_CFG = FlashCfg(MLA_HEADS, MXU_DIM, 2, tuple(range(MLA_HEADS)), MLA_W, 8)


def _build_rhs(cfg, qt_ref, rhs_ref):
    for i in range(cfg.n_soft):
        g, j = i // cfg.soft_per_group, i % cfg.soft_per_group
        if cfg is DA_CFG:
            rhs_ref[i] = _row_band(qt_ref[0, g * LANES:(g + 1) * LANES, :], j * DA_HD, (j + 1) * DA_HD)
        else:
            nope = _row_band(qt_ref[0, g * LANES:(g + 1) * LANES, :], j * MLA_NOPE, (j + 1) * MLA_NOPE)
            rope0 = MLA_HEADS * MLA_NOPE
            rope = _row_band(qt_ref[0, rope0:rope0 + LANES, :], i * MLA_ROPE, (i + 1) * MLA_ROPE)
            rhs_ref[i] = jnp.concatenate([nope, rope], axis=0)


def _flash_tiles(cfg, tiles, rhs_ref, m_ref, acc_ref, stale_max):
    work = [(t, i) for t in range(len(tiles)) for i in range(cfg.n_soft)]

    def score(w):
        kblk_of, _, mask = tiles[w[0]]
        s = jnp.dot(kblk_of(w[1] // cfg.soft_per_group), rhs_ref[w[1]], preferred_element_type=jnp.float32)
        return s if mask is None else s + mask

    pending = [score(w) for w in work[:LOOKAHEAD]]
    for n, (t, i) in enumerate(work):
        vt = _with_ones(tiles[t][1](cfg.v_of_soft[i]))
        s = pending.pop(0)
        if n + LOOKAHEAD < len(work):
            pending.append(score(work[n + LOOKAHEAD]))
        m_prev = m_ref[i]
        m_new = jnp.maximum(m_prev, jnp.max(s, axis=0, keepdims=True))
        alpha = jnp.exp2(m_prev - m_new)
        if stale_max:
            p = jnp.exp2(s - m_prev).astype(jnp.bfloat16)
            acc_ref[i] = (acc_ref[i] + jnp.dot(vt, p, preferred_element_type=jnp.float32)) * alpha
        else:
            p = jnp.exp2(s - m_new).astype(jnp.bfloat16)
            acc_ref[i] = alpha * acc_ref[i] + jnp.dot(vt, p, preferred_element_type=jnp.float32)
        m_ref[i] = m_new


def _feature_indicator(cfg, g):
    f = lax.broadcasted_iota(jnp.int32, (cfg.group_w, LANES), 0)
    j = lax.broadcasted_iota(jnp.int32, (cfg.group_w, LANES), 1)
    if cfg is DA_CFG:
        hit = jnp.right_shift(f, int(math.log2(DA_HD))) == j
    else:
        nope = (f < LANES) & (jnp.right_shift(f, int(math.log2(MLA_NOPE))) == j)
        rope = (f >= LANES) & (jnp.right_shift(f - LANES, int(math.log2(MLA_ROPE))) == g * cfg.soft_per_group + j)
        hit = nope | rope
    return jnp.where(hit & (j < cfg.soft_per_group), 1.0, 0.0).astype(jnp.bfloat16)


def _max_key_norms(cfg, kf_ref, kfm_ref, n_keys, kn_ref):
    gw = cfg.group_w
    n_groups = cfg.n_soft // cfg.soft_per_group
    inds = [_feature_indicator(cfg, g) for g in range(n_groups)]

    def sq_norms(blk, g):
        x = blk.astype(jnp.float32)
        n2 = jnp.dot((x * x).astype(jnp.bfloat16), inds[g], preferred_element_type=jnp.float32)
        return jnp.max(n2, axis=0, keepdims=True)

    def body(c, carry):
        r0 = pl.multiple_of(c * TM, TM)
        return tuple(jnp.maximum(carry[g], sq_norms(kf_ref[pl.ds(r0, TM), g * gw:(g + 1) * gw], g))
                     for g in range(n_groups))

    init = tuple(sq_norms(kfm_ref[:, g * gw:(g + 1) * gw], g) for g in range(n_groups))
    kmax = lax.fori_loop(0, n_keys // TM, body, init)
    lane = lax.broadcasted_iota(jnp.int32, (1, LANES), 1)
    for i in range(cfg.n_soft):
        g, j = divmod(i, cfg.soft_per_group)
        v = jnp.max(jnp.where(lane == j, kmax[g], 0.0), axis=1, keepdims=True)
        kn_ref[i] = jnp.broadcast_to(v, (1, LANES))


def _stale_max_is_safe(cfg, rhs_ref, m_ref, kn_ref):
    ok = None
    for i in range(cfg.n_soft):
        r = rhs_ref[i].astype(jnp.float32)
        bound2 = jnp.sum(r * r, axis=0, keepdims=True) * kn_ref[i][:, 0:1] * NORM_SLACK
        t = m_ref[i] + STALE_MARGIN
        good = (t >= 0.0) & (bound2 <= t * t)
        ok = good if ok is None else ok & good
    return jnp.min(jnp.where(ok, 1.0, 0.0)) > 0.5


def _flash_kernel(*refs, cfg, n_keys, lam_init, meta_queries):
    if cfg is DA_CFG:
        (qt_ref, kf_ref, vt_ref, kfm_ref, vtm_ref, lam_ref, gs_ref, o_ref,
         rhs_ref, m_ref, acc_ref, kn_ref, *rest) = refs
    else:
        qt_ref, kf_ref, vt_ref, kfm_ref, vtm_ref, o_ref, rhs_ref, m_ref, acc_ref, kn_ref, *rest = refs
    b = pl.program_id(0)
    tq = qt_ref.shape[2]
    vd = DA_VD
    gw = cfg.group_w
    if meta_queries:
        _max_key_norms(cfg, kf_ref, kfm_ref, n_keys, kn_ref)
    else:
        @pl.when(pl.program_id(1) == 0)
        def _():
            _max_key_norms(cfg, kf_ref, kfm_ref, n_keys, kn_ref)
    _build_rhs(cfg, qt_ref, rhs_ref)
    m_ref[...] = jnp.full(m_ref.shape, NEG, jnp.float32)
    acc_ref[...] = jnp.zeros(acc_ref.shape, jnp.float32)

    _flash_tiles(
        cfg,
        [(lambda g: kfm_ref[:, g * gw:(g + 1) * gw],
          lambda h: vtm_ref[0, h * vd:(h + 1) * vd, :],
          _meta_key_mask(b))],
        rhs_ref, m_ref, acc_ref, stale_max=False)

    def real_keys(stale_max, unroll):
        def body(it, carry):
            tiles = []
            for u in range(unroll):
                kt = it * unroll + u
                k0 = pl.multiple_of(kt * TK, TK)
                tiles.append((
                    lambda g, k0=k0: kf_ref[pl.ds(k0, TK), g * gw:(g + 1) * gw],
                    lambda h, kt=kt: vt_ref[kt, h * vd:(h + 1) * vd, :],
                    None))
            _flash_tiles(cfg, tiles, rhs_ref, m_ref, acc_ref, stale_max)
            return carry

        lax.fori_loop(0, n_keys // (TK * unroll), body, 0)

    lax.cond(_stale_max_is_safe(cfg, rhs_ref, m_ref, kn_ref),
             lambda: real_keys(True, math.gcd(cfg.unroll, n_keys // TK)),
             lambda: real_keys(False, 1))

    heads = []
    if cfg is DA_CFG:
        lp = lam_ref[...]
        lam = (jnp.exp(jnp.sum(lp[0:1] * lp[1:2], axis=1, keepdims=True))
               - jnp.exp(jnp.sum(lp[2:3] * lp[3:4], axis=1, keepdims=True)) + lam_init)
        for h in range(DA_HEADS):
            a1, a2 = acc_ref[2 * h], acc_ref[2 * h + 1]
            o = a1[0:vd] / a1[vd:vd + 1] - lam * (a2[0:vd] / a2[vd:vd + 1])
            o = o * lax.rsqrt(jnp.mean(o * o, axis=0, keepdims=True) + EPS) * gs_ref[...]
            heads.append(o * (1.0 - lam_init))
    else:
        for h in range(MLA_HEADS):
            a = acc_ref[h]
            heads.append(a[0:vd] / a[vd:vd + 1])
    pairs = [jnp.concatenate(heads[2 * g:2 * g + 2], axis=0).T for g in range(len(heads) // 2)]
    out = jnp.concatenate(pairs, axis=1)
    if meta_queries:
        stage_ref = rest[0]
        stage_ref[...] = out
        r0 = pl.multiple_of((b % (LANES // N_META)) * N_META, N_META)
        o_ref[...] = stage_ref[pl.ds(r0, N_META), :].astype(o_ref.dtype)
    else:
        o_ref[...] = out.astype(o_ref.dtype)


def _flash_call(cfg, grp, qt, kf, vt, extra, o_prev, lam_init, meta_queries, name):
    n, bsz = grp.n, grp.batch
    qw, kw, vw = qt.shape[1], kf.shape[1], vt.shape[1]
    tiles_per_seq = n // TM
    meta_blk0 = grp.meta0 // LANES
    per_blk = LANES // N_META
    if meta_queries:
        grid = (bsz,)
        tq = LANES
        q_map = lambda b: ((meta_blk0 + b // per_blk) // (TM // LANES), 0, (meta_blk0 + b // per_blk) % (TM // LANES))
        o_spec = pl.BlockSpec((N_META, cfg.out_w), lambda b: (grp.meta0 // N_META + b, 0))
        fix = lambda f: (lambda b: f(b))
        sem = ("arbitrary",)
    else:
        grid = (bsz, tiles_per_seq)
        tq = TM
        q_map = lambda b, i: (grp.real0 // TM + b * tiles_per_seq + i, 0, 0)
        o_spec = pl.BlockSpec((TM, cfg.out_w), lambda b, i: (grp.real0 // TM + b * tiles_per_seq + i, 0))
        fix = lambda f: (lambda b, i: f(b))
        sem = ("arbitrary", "arbitrary")
    in_specs = [
        pl.BlockSpec((1, qw, tq), q_map),
        pl.BlockSpec((n, kw), fix(lambda b: (grp.real0 // n + b, 0))),
        pl.BlockSpec((n // TK, vw, TK), fix(lambda b: (grp.real0 // n + b, 0, 0))),
        pl.BlockSpec((LANES, kw), fix(lambda b: (meta_blk0 + b // per_blk, 0))),
        pl.BlockSpec((1, vw, LANES), fix(lambda b: ((meta_blk0 + b // per_blk) // (TK // LANES), 0,
                                                     (meta_blk0 + b // per_blk) % (TK // LANES)))),
    ]
    args = [qt, kf, vt, kf, vt]
    for a in extra:
        in_specs.append(pl.BlockSpec(a.shape, fix(lambda b, nd=a.ndim: (0,) * nd)))
        args.append(a)
    in_specs.append(pl.BlockSpec(memory_space=pl.ANY))
    args.append(o_prev)
    scratch = [
        pltpu.VMEM((cfg.n_soft, cfg.group_w, tq), jnp.bfloat16),
        pltpu.VMEM((cfg.n_soft, 1, tq), jnp.float32),
        pltpu.VMEM((cfg.n_soft, DA_VD + ONES_ROWS, tq), jnp.float32),
        pltpu.VMEM((cfg.n_soft, 1, LANES), jnp.float32),
    ]
    if meta_queries:
        scratch.append(pltpu.VMEM((LANES, cfg.out_w), jnp.float32))

    def body(*refs):
        n_in = len(args)
        ins, rest = refs[:n_in - 1], refs[n_in:]
        _flash_kernel(*ins, *rest, cfg=cfg, n_keys=n, lam_init=lam_init, meta_queries=meta_queries)

    return pl.pallas_call(
        body,
        grid=grid,
        in_specs=in_specs,
        out_specs=o_spec,
        out_shape=jax.ShapeDtypeStruct(o_prev.shape, o_prev.dtype),
        scratch_shapes=scratch,
        input_output_aliases={len(args) - 1: 0},
        compiler_params=pltpu.CompilerParams(dimension_semantics=sem, vmem_limit_bytes=VMEM_LIMIT_BYTES),
        name=name,
    )(*args)


def _na_kernel(qt_ref, kf_ref, vt_ref, kfm_ref, vtm_ref, bias_ref, prev_ref, o_ref, *, grid_rows):
    del prev_ref
    b, i = pl.program_id(0), pl.program_id(1)
    ws = jnp.clip(i * NA_QROWS - WIN_R // 2, 0, grid_rows - NA_KROWS)
    kwin = kf_ref[pl.ds(pl.multiple_of(ws * GRID_W, 2 * GRID_W), NA_KROWS * GRID_W), :]
    vwin = vt_ref[pl.ds(ws // 2, NA_KROWS // 2)]
    kmeta = kfm_ref[...]
    mmask = _meta_key_mask(b)
    heads = []
    for h in range(NA_HEADS):
        g, j = h // 2, h % 2
        rhs = _row_band(qt_ref[0, g * LANES:(g + 1) * LANES, :], j * NA_HD, (j + 1) * NA_HD)
        s1 = jnp.dot(kwin[:, g * LANES:(g + 1) * LANES], rhs, preferred_element_type=jnp.float32) + bias_ref[0, h]
        s2 = jnp.dot(kmeta[:, g * LANES:(g + 1) * LANES], rhs, preferred_element_type=jnp.float32) + mmask
        m = jnp.maximum(jnp.max(s1, axis=0, keepdims=True), jnp.max(s2, axis=0, keepdims=True))
        p1 = jnp.exp2(s1 - m).astype(jnp.bfloat16)
        p2 = jnp.exp2(s2 - m).astype(jnp.bfloat16)
        v1 = jnp.concatenate([vwin[c, h * NA_HD:(h + 1) * NA_HD, :] for c in range(NA_KROWS // 2)], axis=1)
        acc = (jnp.dot(_with_ones(v1), p1, preferred_element_type=jnp.float32)
               + jnp.dot(_with_ones(vtm_ref[0, h * NA_HD:(h + 1) * NA_HD, :]), p2,
                         preferred_element_type=jnp.float32))
        heads.append(acc[0:NA_HD] / acc[NA_HD:NA_HD + 1])
    pairs = [jnp.concatenate(heads[2 * g:2 * g + 2], axis=0).T for g in range(NA_HEADS // 2)]
    o_ref[...] = jnp.concatenate(pairs, axis=1).astype(o_ref.dtype)


def _na_meta_kernel(qt_ref, kf_ref, vt_ref, prev_ref, o_ref):
    del prev_ref
    kr = lax.broadcasted_iota(jnp.int32, (LANES, LANES), 0) // N_META
    qc = lax.broadcasted_iota(jnp.int32, (LANES, LANES), 1) // N_META
    mask = jnp.where(kr == qc, 0.0, NEG).astype(jnp.float32)
    kf = kf_ref[...]
    heads = []
    for h in range(NA_HEADS):
        g, j = h // 2, h % 2
        rhs = _row_band(qt_ref[0, g * LANES:(g + 1) * LANES, :], j * NA_HD, (j + 1) * NA_HD)
        s = jnp.dot(kf[:, g * LANES:(g + 1) * LANES], rhs, preferred_element_type=jnp.float32) + mask
        p = jnp.exp2(s - jnp.max(s, axis=0, keepdims=True)).astype(jnp.bfloat16)
        acc = jnp.dot(_with_ones(vt_ref[0, h * NA_HD:(h + 1) * NA_HD, :]), p, preferred_element_type=jnp.float32)
        heads.append(acc[0:NA_HD] / acc[NA_HD:NA_HD + 1])
    pairs = [jnp.concatenate(heads[2 * g:2 * g + 2], axis=0).T for g in range(NA_HEADS // 2)]
    o_ref[...] = jnp.concatenate(pairs, axis=1).astype(o_ref.dtype)


def _na_call(grp, qt, kf, vt, bias, o_prev):
    n, bsz = grp.n, grp.batch
    grid_rows = n // GRID_W
    steps = grid_rows // NA_QROWS
    tq = NA_QROWS * GRID_W
    meta_blk0 = grp.meta0 // LANES
    per_blk = LANES // N_META
    in_specs = [
        pl.BlockSpec((1, NA_W, tq), lambda b, i: (grp.real0 // TM + b * (n // TM) + i // (TM // tq), 0, i % (TM // tq))),
        pl.BlockSpec((n, NA_W), lambda b, i: (grp.real0 // n + b, 0)),
        pl.BlockSpec((n // LANES, NA_W, LANES), lambda b, i: (grp.real0 // n + b, 0, 0)),
        pl.BlockSpec((LANES, NA_W), lambda b, i: (meta_blk0 + b // per_blk, 0)),
        pl.BlockSpec((1, NA_W, LANES), lambda b, i: (meta_blk0 + b // per_blk, 0, 0)),
        pl.BlockSpec((1, NA_HEADS, NA_KROWS * GRID_W, tq),
                     lambda b, i: (jnp.where(i == 0, 0, jnp.where(i == steps - 1, 2, 1)), 0, 0, 0)),
        pl.BlockSpec(memory_space=pl.ANY),
    ]
    return pl.pallas_call(
        functools.partial(_na_kernel, grid_rows=grid_rows),
        grid=(bsz, steps),
        in_specs=in_specs,
        out_specs=pl.BlockSpec((tq, NA_W), lambda b, i: (grp.real0 // tq + b * steps + i, 0)),
        out_shape=jax.ShapeDtypeStruct(o_prev.shape, o_prev.dtype),
        input_output_aliases={6: 0},
        compiler_params=pltpu.CompilerParams(dimension_semantics=("arbitrary", "arbitrary"),
                                             vmem_limit_bytes=VMEM_LIMIT_BYTES),
        name="na_real",
    )(qt, kf, vt, kf, vt, bias, o_prev)


def _na_meta_call(layout, qt, kf, vt, o_prev):
    blk0 = layout.groups[0].meta0 // LANES
    nblk = sum(_round_up(g.batch * N_META, LANES) for g in layout.groups) // LANES
    sub = TM // LANES
    return pl.pallas_call(
        _na_meta_kernel,
        grid=(nblk,),
        in_specs=[
            pl.BlockSpec((1, NA_W, LANES), lambda i: ((blk0 + i) // sub, 0, (blk0 + i) % sub)),
            pl.BlockSpec((LANES, NA_W), lambda i: (blk0 + i, 0)),
            pl.BlockSpec((1, NA_W, LANES), lambda i: (blk0 + i, 0, 0)),
            pl.BlockSpec(memory_space=pl.ANY),
        ],
        out_specs=pl.BlockSpec((LANES, NA_W), lambda i: (blk0 + i, 0)),
        out_shape=jax.ShapeDtypeStruct(o_prev.shape, o_prev.dtype),
        input_output_aliases={3: 0},
        compiler_params=pltpu.CompilerParams(dimension_semantics=("arbitrary",),
                                             vmem_limit_bytes=VMEM_LIMIT_BYTES),
        name="na_meta",
    )(qt, kf, vt, o_prev)


def _na_bias_tables(rel_bias):
    kc = np.arange(GRID_W)[:, None]
    qc = np.arange(GRID_W)[None, :]
    cs = np.clip(qc - WIN_C // 2, 0, GRID_W - WIN_C)
    col_ok = (kc >= cs) & (kc < cs + WIN_C)
    cidx = np.clip(kc - qc + WIN_C - 1, 0, 2 * WIN_C - 2)
    planes = jnp.where(col_ok[None, None], rel_bias[:, :, cidx] * LOG2E, NEG)
    masked = 2 * WIN_R - 1
    planes = jnp.concatenate([planes, jnp.full((NA_HEADS, 1, GRID_W, GRID_W), NEG, planes.dtype)], axis=1)
    kr = np.arange(NA_KROWS)[:, None]
    qr = np.arange(NA_QROWS)[None, :]
    ridx = []
    for win_lo, q_off in ((0 * qr, qr), (qr, qr + WIN_R // 2), (0 * qr + NA_KROWS - WIN_R, qr + NA_KROWS - NA_QROWS)):
        row_ok = (kr >= win_lo) & (kr < win_lo + WIN_R)
        ridx.append(np.where(row_ok, kr - q_off + WIN_R - 1, masked))
    ridx = np.stack(ridx)
    assert ridx.min() >= 0 and ridx.max() <= masked
    t = planes[:, ridx]
    t = jnp.transpose(t, (1, 0, 2, 4, 3, 5))
    return t.reshape(3, NA_HEADS, NA_KROWS * GRID_W, NA_QROWS * GRID_W).astype(jnp.float32)


def _rope_tables(layout):
    pos = np.zeros((layout.rows,), np.float32)
    for g in layout.groups:
        r = np.arange(g.batch * g.n)
        pos[g.real0:g.real0 + g.batch * g.n] = N_META + r % g.n
        m = np.arange(g.batch * N_META)
        pos[g.meta0:g.meta0 + g.batch * N_META] = m % N_META
    pos = jnp.asarray(pos)

    def tables(dim, period):
        inv = ROPE_THETA ** (-(jnp.arange(0, dim, 2, dtype=jnp.float32) / dim))
        ang = pos[:, None] * inv[None, :]
        cos, sin = jnp.cos(ang), jnp.sin(ang)
        rest = period - dim
        c = jnp.concatenate([cos, cos, jnp.ones((layout.rows, rest), jnp.float32)], axis=1)
        s = jnp.concatenate([-sin, sin, jnp.zeros((layout.rows, rest), jnp.float32)], axis=1)
        reps = LANES // period
        return jnp.tile(c, (1, reps)), jnp.tile(s, (1, reps))

    cda, sda = tables(DA_ROT, DA_HD)
    cm, sm = tables(MLA_ROPE, MLA_ROPE)
    return cda, sda, cm, sm


def _prep_layer(l, norm_g, ffn_w_gate, ffn_w_up, ffn_w_down, w_in, w_out, mla_q_norm_g, mla_kv_norm_g,
                mla_w_uq, mla_w_ukv, da_subln_g):
    bf = jnp.bfloat16
    kr0 = IN_W - MLA_ROPE
    win = jnp.concatenate([w_in[l, :, :kr0]] + [w_in[l, :, kr0:]] * (LANES // MLA_ROPE), axis=1).astype(bf)
    uq = mla_w_uq[l].reshape(Q_LORA, MLA_HEADS, MLA_NOPE + MLA_ROPE)
    uq = jnp.concatenate([uq[:, :, :MLA_NOPE].reshape(Q_LORA, -1), uq[:, :, MLA_NOPE:].reshape(Q_LORA, -1)], axis=1)
    ukv = mla_w_ukv[l].reshape(KV_LORA, MLA_HEADS, MLA_NOPE + MLA_VD)
    ukv = jnp.concatenate([ukv[:, :, :MLA_NOPE].reshape(KV_LORA, -1), ukv[:, :, MLA_NOPE:].reshape(KV_LORA, -1)], axis=1)
    row = lambda v: v.reshape(1, -1)
    return dict(
        g=[row(norm_g[l, i]) for i in range(3)],
        ffn=[(ffn_w_gate[l, i].astype(bf), ffn_w_up[l, i].astype(bf), ffn_w_down[l, i].astype(bf)) for i in range(2)],
        win=win, uq=uq.astype(bf), ukv=ukv.astype(bf),
        gq=row(mla_q_norm_g[l]), gkv=row(mla_kv_norm_g[l]),
        wo=(w_out[l, :NA_W].astype(bf), w_out[l, NA_W:NA_W + DA_W].astype(bf), w_out[l, NA_W + DA_W:].astype(bf)),
        gs=da_subln_g[l].reshape(DA_VD, 1),
    )


def kernel(x_prompt, x_sample, meta_tokens, norm_g, final_norm_g, ffn_w_gate, ffn_w_up, ffn_w_down, w_in, w_out, na_rel_bias, da_lambda, da_subln_g, mla_q_norm_g, mla_kv_norm_g, mla_w_uq, mla_w_ukv):
    xs = (x_prompt, x_sample)
    layout = _make_layout([(x.shape[0], x.shape[1]) for x in xs])
    real_rows = sum(g.batch * g.n for g in layout.groups)
    tail = []
    for g in layout.groups:
        blk = jnp.tile(meta_tokens.astype(jnp.float32), (g.batch, 1))
        tail.append(jnp.pad(blk, ((0, _round_up(g.batch * N_META, LANES) - g.batch * N_META), (0, 0))))
    tail = jnp.concatenate(tail, axis=0)
    tail = jnp.pad(tail, ((0, layout.rows - real_rows - tail.shape[0]), (0, 0)))
    srcs = [x.reshape(-1, D_MODEL) for x in xs] + [tail]
    real_bounds = tuple(g.real0 // TM for g in layout.groups) + (real_rows // TM,)
    bounds = real_bounds + (layout.rows // TM,)
    tabs = _rope_tables(layout)
    gf = final_norm_g.reshape(1, -1)

    for l in range(DEPTH):
        p = _prep_layer(l, norm_g, ffn_w_gate, ffn_w_up, ffn_w_down, w_in, w_out, mla_q_norm_g, mla_kv_norm_g,
                        mla_w_uq, mla_w_ukv, da_subln_g)
        lam_init = 0.8 - 0.6 * math.exp(-0.3 * l)
        (h, naq, nak, nav, daq, dak, dav, mq, mk, mv) = _ffn_inproj(
            srcs, bounds, tabs, p["g"][0], p["g"][1], *p["ffn"][0], p["win"], p["gq"], p["gkv"], p["uq"], p["ukv"])
        bias = _na_bias_tables(na_rel_bias[l])
        o_na = jnp.zeros((layout.rows, NA_W), jnp.bfloat16)
        o_da = jnp.zeros((layout.rows, DA_W), jnp.bfloat16)
        o_m = jnp.zeros((layout.rows, MLA_W), jnp.bfloat16)
        o_na = _na_meta_call(layout, naq, nak, nav, o_na)
        for gi, grp in enumerate(layout.groups):
            o_na = _na_call(grp, naq, nak, nav, bias, o_na)
            for meta_q in (False, True):
                tag = f"g{gi}_{'meta' if meta_q else 'real'}"
                o_da = _flash_call(DA_CFG, grp, daq, dak, dav, (da_lambda[l], p["gs"]), o_da, lam_init, meta_q,
                                   "da_" + tag)
                o_m = _flash_call(MLA_CFG, grp, mq, mk, mv, (), o_m, lam_init, meta_q, "mla_" + tag)
        last = l == DEPTH - 1
        h = _outproj_ffn(h, o_na, o_da, o_m, *p["wo"], p["g"][2], *p["ffn"][1], gf, real_bounds if last else None)
        srcs, bounds = [h], (0, layout.rows // TM)

    return tuple(y.reshape(x.shape) for x, y in zip(xs, h))
```

```python
import functools
import math
from typing import NamedTuple

import jax
import jax.numpy as jnp
import numpy as np
from jax import lax
from jax.experimental import pallas as pl
from jax.experimental.pallas import tpu as pltpu

D_MODEL = 1024
DEPTH = 2
GRID_W = 64
N_META = 16
WIN_R = 8
WIN_C = 16
NA_HEADS = 6
NA_HD = 64
DA_HEADS = 6
DA_HD = 32
DA_VD = 64
DA_ROT = DA_HD // 4
MLA_HEADS = 4
MLA_NOPE = 64
MLA_ROPE = 32
MLA_VD = 64
Q_LORA = 256
KV_LORA = 128
ROPE_THETA = 500000.0
D_FF = 2816
EPS = 1e-6
NA_W = NA_HEADS * NA_HD
DA_W = DA_HEADS * DA_VD
MLA_W = MLA_HEADS * MLA_VD
DA_QK_W = DA_HEADS * 2 * DA_HD
IN_W = 3 * NA_W + 2 * DA_QK_W + DA_W + Q_LORA + KV_LORA + MLA_ROPE

LANES = 128
MXU_DIM = 256
VMEM_LIMIT_BYTES = 58 * 1024 * 1024

TM = 512
TK = 256
NA_QROWS = 4
NA_KROWS = NA_QROWS + WIN_R
LOOKAHEAD = 2
FF_SPLIT = 1536
IN_W_PAD = 2816

LOG2E = 1.4426950408889634
NEG = -1e30
ONES_ROWS = 16
STALE_MARGIN = 60.0
NORM_SLACK = 1.05


def _round_up(x, m):
    return (x + m - 1) // m * m


class Group(NamedTuple):
    batch: int
    n: int
    real0: int
    meta0: int


class Layout(NamedTuple):
    groups: tuple
    rows: int


def _make_layout(shapes):
    row = 0
    real0 = []
    for b, n in shapes:
        assert n % TM == 0 and row % n == 0 and n % GRID_W == 0
        assert (n // GRID_W) % NA_QROWS == 0 and n // GRID_W >= NA_KROWS
        real0.append(row)
        row += b * n
    groups = []
    for (b, n), r0 in zip(shapes, real0):
        groups.append(Group(b, n, r0, row))
        row += _round_up(b * N_META, LANES)
    return Layout(tuple(groups), _round_up(row, TM))


def _rms(x, g):
    return x * lax.rsqrt(jnp.mean(x * x, axis=-1, keepdims=True) + EPS) * g


def _swiglu_half(xn, wg_ref, wu_ref, wd_ref):
    acc = None
    for lo, hi in ((0, FF_SPLIT), (FF_SPLIT, D_FF)):
        gate = jnp.dot(xn, wg_ref[:, lo:hi], preferred_element_type=jnp.float32)
        up = jnp.dot(xn, wu_ref[:, lo:hi], preferred_element_type=jnp.float32)
        hm = (gate * jax.nn.sigmoid(gate) * up).astype(jnp.bfloat16)
        part = jnp.dot(hm, wd_ref[lo:hi, :], preferred_element_type=jnp.float32)
        acc = part if acc is None else acc + part
    return 0.5 * acc


def _rope_chunk(x, c, s, half, period):
    lane = lax.broadcasted_iota(jnp.int32, x.shape, 1)
    lo = (lane & (period - 1)) < half
    partner = jnp.where(lo, pltpu.roll(x, LANES - half, 1), pltpu.roll(x, half, 1))
    return x * c + partner * s


def _segment_specs(bounds):
    return [pl.BlockSpec((TM, D_MODEL), lambda i, lo=lo, hi=hi: (jnp.clip(i - lo, 0, hi - lo - 1), 0))
            for lo, hi in zip(bounds[:-1], bounds[1:])]


def _ffn_inproj_kernel(*refs, bounds):
    n_src = len(bounds) - 1
    srcs = refs[:n_src]
    (cda_ref, sda_ref, cm_ref, sm_ref, ga_ref, gb_ref,
     wg_ref, wu_ref, wd_ref, win_ref, gq_ref, gkv_ref, wuq_ref, wukv_ref,
     h1_ref, naq_ref, nak_ref, nav_ref, daq_ref, dak_ref, dav_ref,
     mq_ref, mk_ref, mv_ref) = refs[n_src:]
    i = pl.program_id(0)
    x = srcs[-1][...]
    for sgm in range(n_src - 2, -1, -1):
        x = jnp.where(i < bounds[sgm + 1], srcs[sgm][...], x)
    xn = _rms(x, ga_ref[...]).astype(jnp.bfloat16)
    h1 = x + _swiglu_half(xn, wg_ref, wu_ref, wd_ref)
    h1_ref[...] = h1
    xn2 = _rms(h1, gb_ref[...]).astype(jnp.bfloat16)

    u = jnp.dot(xn2, win_ref[:, 0:3 * NA_W], preferred_element_type=jnp.float32)
    naq_ref[0] = (u[:, 0:NA_W] * (NA_HD ** -0.5 * LOG2E)).T.astype(jnp.bfloat16)
    nak_ref[...] = u[:, NA_W:2 * NA_W].astype(jnp.bfloat16)
    vt = u[:, 2 * NA_W:3 * NA_W].T.astype(jnp.bfloat16)
    for c in range(TM // LANES):
        nav_ref[c] = vt[:, c * LANES:(c + 1) * LANES]

    o0 = 3 * NA_W
    u = jnp.dot(xn2, win_ref[:, o0:o0 + 2 * DA_QK_W + DA_W], preferred_element_type=jnp.float32)
    cda, sda = cda_ref[...], sda_ref[...]
    q = jnp.concatenate([_rope_chunk(u[:, c * LANES:(c + 1) * LANES], cda, sda, DA_ROT // 2, DA_HD)
                         for c in range(DA_QK_W // LANES)], axis=1)
    k = jnp.concatenate([_rope_chunk(u[:, DA_QK_W + c * LANES:DA_QK_W + (c + 1) * LANES], cda, sda, DA_ROT // 2, DA_HD)
                         for c in range(DA_QK_W // LANES)], axis=1)
    daq_ref[0] = (q * (DA_HD ** -0.5 * LOG2E)).T.astype(jnp.bfloat16)
    dak_ref[...] = k.astype(jnp.bfloat16)
    vt = u[:, 2 * DA_QK_W:2 * DA_QK_W + DA_W].T.astype(jnp.bfloat16)
    for c in range(TM // TK):
        dav_ref[c] = vt[:, c * TK:(c + 1) * TK]

    o1 = o0 + 2 * DA_QK_W + DA_W
    u = jnp.dot(xn2, win_ref[:, o1:IN_W_PAD], preferred_element_type=jnp.float32)
    cm, sm = cm_ref[...], sm_ref[...]
    lane = lax.broadcasted_iota(jnp.int32, cm.shape, 1)
    band = (lane >= MLA_NOPE) & (lane < MLA_NOPE + MLA_ROPE)
    cb, sb = jnp.where(band, cm, 1.0), jnp.where(band, sm, 0.0)
    cq = _rms(u[:, 0:Q_LORA], gq_ref[...]).astype(jnp.bfloat16)
    qm = jnp.dot(cq, wuq_ref[...], preferred_element_type=jnp.float32)
    qm = jnp.concatenate([_rope_chunk(qm[:, h * LANES:(h + 1) * LANES], cb, sb, MLA_ROPE // 2, MLA_ROPE)
                          for h in range(MLA_HEADS)], axis=1) * ((MLA_NOPE + MLA_ROPE) ** -0.5 * LOG2E)
    mq_ref[0] = qm.T.astype(jnp.bfloat16)
    ckv = _rms(u[:, Q_LORA:Q_LORA + KV_LORA], gkv_ref[...]).astype(jnp.bfloat16)
    kv = jnp.dot(ckv, wukv_ref[...], preferred_element_type=jnp.float32)
    kr = _rope_chunk(u[:, Q_LORA + KV_LORA:Q_LORA + KV_LORA + LANES], cm, sm, MLA_ROPE // 2, MLA_ROPE)
    kr = jnp.where(band, kr, 0.0)
    kf = jnp.concatenate([kv[:, h * LANES:(h + 1) * LANES] + kr for h in range(MLA_HEADS)], axis=1)
    mk_ref[...] = kf.astype(jnp.bfloat16)
    vt = kv[:, MLA_HEADS * LANES:MLA_HEADS * LANES + MLA_W].T.astype(jnp.bfloat16)
    for c in range(TM // TK):
        mv_ref[c] = vt[:, c * TK:(c + 1) * TK]


def _outproj_ffn_kernel(h_ref, ona_ref, oda_ref, om_ref, wo1_ref, wo2_ref, wo3_ref, g_ref,
                        wg_ref, wu_ref, wd_ref, gf_ref, *o_refs, out_bounds):
    h = h_ref[...]
    h = h + jnp.dot(ona_ref[...], wo1_ref[...], preferred_element_type=jnp.float32)
    h = h + jnp.dot(oda_ref[...], wo2_ref[...], preferred_element_type=jnp.float32)
    h = h + jnp.dot(om_ref[...], wo3_ref[...], preferred_element_type=jnp.float32)
    xn = _rms(h, g_ref[...]).astype(jnp.bfloat16)
    h = h + _swiglu_half(xn, wg_ref, wu_ref, wd_ref)
    if out_bounds is None:
        o_refs[0][...] = h
    else:
        h = _rms(h, gf_ref[...])
        i = pl.program_id(0)
        for o_ref, lo, hi in zip(o_refs, out_bounds[:-1], out_bounds[1:]):
            @pl.when((i >= lo) & (i < hi))
            def _(o_ref=o_ref):
                o_ref[...] = h


def _const_spec(shape):
    nd = len(shape)
    return pl.BlockSpec(shape, lambda i: (0,) * nd, pipeline_mode=pl.Buffered(1))


def _ffn_inproj(srcs, bounds, tabs, ga, gb, wg, wu, wd, win, gq, gkv, wuq, wukv):
    nt = bounds[-1]
    rows = nt * TM
    row_spec = lambda w: pl.BlockSpec((TM, w), lambda i: (i, 0))
    tile_spec = lambda r: pl.BlockSpec((1, r, TM), lambda i: (i, 0, 0))
    bf = jnp.bfloat16
    out_shape = (
        jax.ShapeDtypeStruct((rows, D_MODEL), jnp.float32),
        jax.ShapeDtypeStruct((nt, NA_W, TM), bf),
        jax.ShapeDtypeStruct((rows, NA_W), bf),
        jax.ShapeDtypeStruct((rows // LANES, NA_W, LANES), bf),
        jax.ShapeDtypeStruct((nt, DA_QK_W, TM), bf),
        jax.ShapeDtypeStruct((rows, DA_QK_W), bf),
        jax.ShapeDtypeStruct((rows // TK, DA_W, TK), bf),
        jax.ShapeDtypeStruct((nt, MLA_HEADS * LANES, TM), bf),
        jax.ShapeDtypeStruct((rows, 2 * MXU_DIM), bf),
        jax.ShapeDtypeStruct((rows // TK, MLA_W, TK), bf),
    )
    out_specs = (
        row_spec(D_MODEL),
        tile_spec(NA_W),
        row_spec(NA_W),
        pl.BlockSpec((TM // LANES, NA_W, LANES), lambda i: (i, 0, 0)),
        tile_spec(DA_QK_W),
        row_spec(DA_QK_W),
        pl.BlockSpec((TM // TK, DA_W, TK), lambda i: (i, 0, 0)),
        tile_spec(MLA_HEADS * LANES),
        row_spec(2 * MXU_DIM),
        pl.BlockSpec((TM // TK, MLA_W, TK), lambda i: (i, 0, 0)),
    )
    in_specs = _segment_specs(bounds) + [row_spec(LANES)] * 4 + [
        _const_spec(a.shape) for a in (ga, gb, wg, wu, wd, win, gq, gkv, wuq, wukv)]
    return pl.pallas_call(
        functools.partial(_ffn_inproj_kernel, bounds=bounds),
        grid=(nt,),
        in_specs=in_specs,
        out_specs=out_specs,
        out_shape=out_shape,
        compiler_params=pltpu.CompilerParams(dimension_semantics=("arbitrary",),
                                             vmem_limit_bytes=VMEM_LIMIT_BYTES),
        name="ffn_inproj",
    )(*srcs, *tabs, ga, gb, wg, wu, wd, win, gq, gkv, wuq, wukv)


def _outproj_ffn(h, ona, oda, om, wo1, wo2, wo3, g, wg, wu, wd, gf, out_bounds):
    rows = h.shape[0]
    row_spec = lambda w: pl.BlockSpec((TM, w), lambda i: (i, 0))
    if out_bounds is None:
        out_specs = row_spec(D_MODEL)
        out_shape = jax.ShapeDtypeStruct((rows, D_MODEL), jnp.float32)
    else:
        out_specs = tuple(_segment_specs(out_bounds))
        out_shape = tuple(jax.ShapeDtypeStruct(((hi - lo) * TM, D_MODEL), jnp.float32)
                          for lo, hi in zip(out_bounds[:-1], out_bounds[1:]))
    in_specs = [row_spec(D_MODEL), row_spec(NA_W), row_spec(DA_W), row_spec(MLA_W)] + [
        _const_spec(a.shape) for a in (wo1, wo2, wo3, g, wg, wu, wd, gf)]
    return pl.pallas_call(
        functools.partial(_outproj_ffn_kernel, out_bounds=out_bounds),
        grid=(rows // TM,),
        in_specs=in_specs,
        out_specs=out_specs,
        out_shape=out_shape,
        compiler_params=pltpu.CompilerParams(dimension_semantics=("arbitrary",),
                                             vmem_limit_bytes=VMEM_LIMIT_BYTES),
        name="outproj_ffn",
    )(h, ona, oda, om, wo1, wo2, wo3, g, wg, wu, wd, gf)


def _row_band(block, lo, hi):
    row = lax.broadcasted_iota(jnp.int32, block.shape, 0)
    return jnp.where((row >= lo) & (row < hi), block, jnp.zeros_like(block))


def _with_ones(vt):
    return jnp.concatenate([vt, jnp.ones((ONES_ROWS, vt.shape[1]), vt.dtype)], axis=0)


def _meta_key_mask(b):
    row = lax.broadcasted_iota(jnp.int32, (LANES, 1), 0)
    lo = (b % (LANES // N_META)) * N_META
    return jnp.where((row >= lo) & (row < lo + N_META), 0.0, NEG).astype(jnp.float32)


class FlashCfg(NamedTuple):
    n_soft: int
    group_w: int
    soft_per_group: int
    v_of_soft: tuple
    out_w: int
    unroll: int


DA_CFG = FlashCfg(2 * DA_HEADS, LANES, 4, tuple(i // 2 for i in range(2 * DA_HEADS)), DA_W, 4)
MLA_CFG = FlashCfg(MLA_HEADS, LANES, 1, tuple(range(MLA_HEADS)), MLA_W, 8)


def _build_rhs(cfg, qt_ref, rhs_ref):
    for i in range(cfg.n_soft):
        g, j = i // cfg.soft_per_group, i % cfg.soft_per_group
        if cfg is DA_CFG:
            rhs_ref[i] = _row_band(qt_ref[0, g * LANES:(g + 1) * LANES, :], j * DA_HD, (j + 1) * DA_HD)
        else:
            rhs_ref[i] = qt_ref[0, i * LANES:(i + 1) * LANES, :]


def _flash_tiles(cfg, tiles, rhs_ref, m_ref, acc_ref, stale_max):
    work = [(t, i) for t in range(len(tiles)) for i in range(cfg.n_soft)]

    def score(w):
        kblk_of, _, mask = tiles[w[0]]
        s = jnp.dot(kblk_of(w[1] // cfg.soft_per_group), rhs_ref[w[1]], preferred_element_type=jnp.float32)
        return s if mask is None else s + mask

    pending = [score(w) for w in work[:LOOKAHEAD]]
    for n, (t, i) in enumerate(work):
        vt = _with_ones(tiles[t][1](cfg.v_of_soft[i]))
        s = pending.pop(0)
        if n + LOOKAHEAD < len(work):
            pending.append(score(work[n + LOOKAHEAD]))
        m_prev = m_ref[i]
        m_new = jnp.maximum(m_prev, jnp.max(s, axis=0, keepdims=True))
        alpha = jnp.exp2(m_prev - m_new)
        if stale_max:
            p = jnp.exp2(s - m_prev).astype(jnp.bfloat16)
            acc_ref[i] = (acc_ref[i] + jnp.dot(vt, p, preferred_element_type=jnp.float32)) * alpha
        else:
            p = jnp.exp2(s - m_new).astype(jnp.bfloat16)
            acc_ref[i] = alpha * acc_ref[i] + jnp.dot(vt, p, preferred_element_type=jnp.float32)
        m_ref[i] = m_new


def _feature_indicator(cfg, g):
    f = lax.broadcasted_iota(jnp.int32, (cfg.group_w, LANES), 0)
    j = lax.broadcasted_iota(jnp.int32, (cfg.group_w, LANES), 1)
    del g
    if cfg is DA_CFG:
        hit = jnp.right_shift(f, int(math.log2(DA_HD))) == j
    else:
        hit = j == 0
    return jnp.where(hit & (j < cfg.soft_per_group), 1.0, 0.0).astype(jnp.bfloat16)


def _max_key_norms(cfg, kf_ref, kfm_ref, n_keys, kn_ref):
    gw = cfg.group_w
    n_groups = cfg.n_soft // cfg.soft_per_group
    inds = [_feature_indicator(cfg, g) for g in range(n_groups)]

    def sq_norms(blk, g):
        x = blk.astype(jnp.float32)
        n2 = jnp.dot((x * x).astype(jnp.bfloat16), inds[g], preferred_element_type=jnp.float32)
        return jnp.max(n2, axis=0, keepdims=True)

    def body(c, carry):
        r0 = pl.multiple_of(c * TM, TM)
        return tuple(jnp.maximum(carry[g], sq_norms(kf_ref[pl.ds(r0, TM), g * gw:(g + 1) * gw], g))
                     for g in range(n_groups))

    init = tuple(sq_norms(kfm_ref[:, g * gw:(g + 1) * gw], g) for g in range(n_groups))
    kmax = lax.fori_loop(0, n_keys // TM, body, init)
    lane = lax.broadcasted_iota(jnp.int32, (1, LANES), 1)
    for i in range(cfg.n_soft):
        g, j = divmod(i, cfg.soft_per_group)
        v = jnp.max(jnp.where(lane == j, kmax[g], 0.0), axis=1, keepdims=True)
        kn_ref[i] = jnp.broadcast_to(v, (1, LANES))


def _stale_max_is_safe(cfg, rhs_ref, m_ref, kn_ref):
    ok = None
    for i in range(cfg.n_soft):
        r = rhs_ref[i].astype(jnp.float32)
        bound2 = jnp.sum(r * r, axis=0, keepdims=True) * kn_ref[i][:, 0:1] * NORM_SLACK
        t = m_ref[i] + STALE_MARGIN
        good = (t >= 0.0) & (bound2 <= t * t)
        ok = good if ok is None else ok & good
    return jnp.min(jnp.where(ok, 1.0, 0.0)) > 0.5


def _flash_kernel(*refs, cfg, n_keys, lam_init, meta_queries):
    if cfg is DA_CFG:
        (qt_ref, kf_ref, vt_ref, kfm_ref, vtm_ref, lam_ref, gs_ref, o_ref,
         rhs_ref, m_ref, acc_ref, kn_ref, *rest) = refs
    else:
        qt_ref, kf_ref, vt_ref, kfm_ref, vtm_ref, o_ref, rhs_ref, m_ref, acc_ref, kn_ref, *rest = refs
    b = pl.program_id(0)
    tq = qt_ref.shape[2]
    vd = DA_VD
    gw = cfg.group_w
    if meta_queries:
        _max_key_norms(cfg, kf_ref, kfm_ref, n_keys, kn_ref)
    else:
        @pl.when(pl.program_id(1) == 0)
        def _():
            _max_key_norms(cfg, kf_ref, kfm_ref, n_keys, kn_ref)
    _build_rhs(cfg, qt_ref, rhs_ref)
    m_ref[...] = jnp.full(m_ref.shape, NEG, jnp.float32)
    acc_ref[...] = jnp.zeros(acc_ref.shape, jnp.float32)

    _flash_tiles(
        cfg,
        [(lambda g: kfm_ref[:, g * gw:(g + 1) * gw],
          lambda h: vtm_ref[0, h * vd:(h + 1) * vd, :],
          _meta_key_mask(b))],
        rhs_ref, m_ref, acc_ref, stale_max=False)

    def real_keys(stale_max, unroll):
        def body(it, carry):
            tiles = []
            for u in range(unroll):
                kt = it * unroll + u
                k0 = pl.multiple_of(kt * TK, TK)
                tiles.append((
                    lambda g, k0=k0: kf_ref[pl.ds(k0, TK), g * gw:(g + 1) * gw],
                    lambda h, kt=kt: vt_ref[kt, h * vd:(h + 1) * vd, :],
                    None))
            _flash_tiles(cfg, tiles, rhs_ref, m_ref, acc_ref, stale_max)
            return carry

        lax.fori_loop(0, n_keys // (TK * unroll), body, 0)

    lax.cond(_stale_max_is_safe(cfg, rhs_ref, m_ref, kn_ref),
             lambda: real_keys(True, math.gcd(cfg.unroll, n_keys // TK)),
             lambda: real_keys(False, 1))

    heads = []
    if cfg is DA_CFG:
        lp = lam_ref[...]
        lam = (jnp.exp(jnp.sum(lp[0:1] * lp[1:2], axis=1, keepdims=True))
               - jnp.exp(jnp.sum(lp[2:3] * lp[3:4], axis=1, keepdims=True)) + lam_init)
        for h in range(DA_HEADS):
            a1, a2 = acc_ref[2 * h], acc_ref[2 * h + 1]
            o = a1[0:vd] / a1[vd:vd + 1] - lam * (a2[0:vd] / a2[vd:vd + 1])
            o = o * lax.rsqrt(jnp.mean(o * o, axis=0, keepdims=True) + EPS) * gs_ref[...]
            heads.append(o * (1.0 - lam_init))
    else:
        for h in range(MLA_HEADS):
            a = acc_ref[h]
            heads.append(a[0:vd] / a[vd:vd + 1])
    pairs = [jnp.concatenate(heads[2 * g:2 * g + 2], axis=0).T for g in range(len(heads) // 2)]
    out = jnp.concatenate(pairs, axis=1)
    if meta_queries:
        stage_ref = rest[0]
        stage_ref[...] = out
        r0 = pl.multiple_of((b % (LANES // N_META)) * N_META, N_META)
        o_ref[...] = stage_ref[pl.ds(r0, N_META), :].astype(o_ref.dtype)
    else:
        o_ref[...] = out.astype(o_ref.dtype)


def _flash_call(cfg, grp, qt, kf, vt, extra, o_prev, lam_init, meta_queries, name):
    n, bsz = grp.n, grp.batch
    qw, kw, vw = qt.shape[1], kf.shape[1], vt.shape[1]
    tiles_per_seq = n // TM
    meta_blk0 = grp.meta0 // LANES
    per_blk = LANES // N_META
    if meta_queries:
        grid = (bsz,)
        tq = LANES
        q_map = lambda b: ((meta_blk0 + b // per_blk) // (TM // LANES), 0, (meta_blk0 + b // per_blk) % (TM // LANES))
        o_spec = pl.BlockSpec((N_META, cfg.out_w), lambda b: (grp.meta0 // N_META + b, 0))
        fix = lambda f: (lambda b: f(b))
        sem = ("arbitrary",)
    else:
        grid = (bsz, tiles_per_seq)
        tq = TM
        q_map = lambda b, i: (grp.real0 // TM + b * tiles_per_seq + i, 0, 0)
        o_spec = pl.BlockSpec((TM, cfg.out_w), lambda b, i: (grp.real0 // TM + b * tiles_per_seq + i, 0))
        fix = lambda f: (lambda b, i: f(b))
        sem = ("arbitrary", "arbitrary")
    in_specs = [
        pl.BlockSpec((1, qw, tq), q_map),
        pl.BlockSpec((n, kw), fix(lambda b: (grp.real0 // n + b, 0))),
        pl.BlockSpec((n // TK, vw, TK), fix(lambda b: (grp.real0 // n + b, 0, 0))),
        pl.BlockSpec((LANES, kw), fix(lambda b: (meta_blk0 + b // per_blk, 0))),
        pl.BlockSpec((1, vw, LANES), fix(lambda b: ((meta_blk0 + b // per_blk) // (TK // LANES), 0,
                                                     (meta_blk0 + b // per_blk) % (TK // LANES)))),
    ]
    args = [qt, kf, vt, kf, vt]
    for a in extra:
        in_specs.append(pl.BlockSpec(a.shape, fix(lambda b, nd=a.ndim: (0,) * nd)))
        args.append(a)
    in_specs.append(pl.BlockSpec(memory_space=pl.ANY))
    args.append(o_prev)
    scratch = [
        pltpu.VMEM((cfg.n_soft, cfg.group_w, tq), jnp.bfloat16),
        pltpu.VMEM((cfg.n_soft, 1, tq), jnp.float32),
        pltpu.VMEM((cfg.n_soft, DA_VD + ONES_ROWS, tq), jnp.float32),
        pltpu.VMEM((cfg.n_soft, 1, LANES), jnp.float32),
    ]
    if meta_queries:
        scratch.append(pltpu.VMEM((LANES, cfg.out_w), jnp.float32))

    def body(*refs):
        n_in = len(args)
        ins, rest = refs[:n_in - 1], refs[n_in:]
        _flash_kernel(*ins, *rest, cfg=cfg, n_keys=n, lam_init=lam_init, meta_queries=meta_queries)

    return pl.pallas_call(
        body,
        grid=grid,
        in_specs=in_specs,
        out_specs=o_spec,
        out_shape=jax.ShapeDtypeStruct(o_prev.shape, o_prev.dtype),
        scratch_shapes=scratch,
        input_output_aliases={len(args) - 1: 0},
        compiler_params=pltpu.CompilerParams(dimension_semantics=sem, vmem_limit_bytes=VMEM_LIMIT_BYTES),
        name=name,
    )(*args)


def _na_kernel(qt_ref, kf_ref, vt_ref, kfm_ref, vtm_ref, bias_ref, prev_ref, o_ref, *, grid_rows):
    del prev_ref
    b, i = pl.program_id(0), pl.program_id(1)
    ws = jnp.clip(i * NA_QROWS - WIN_R // 2, 0, grid_rows - NA_KROWS)
    kwin = kf_ref[pl.ds(pl.multiple_of(ws * GRID_W, 2 * GRID_W), NA_KROWS * GRID_W), :]
    vwin = vt_ref[pl.ds(ws // 2, NA_KROWS // 2)]
    kmeta = kfm_ref[...]
    mmask = _meta_key_mask(b)
    def scores(h):
        g, j = h // 2, h % 2
        rhs = _row_band(qt_ref[0, g * LANES:(g + 1) * LANES, :], j * NA_HD, (j + 1) * NA_HD)
        s1 = jnp.dot(kwin[:, g * LANES:(g + 1) * LANES], rhs, preferred_element_type=jnp.float32) + bias_ref[0, h]
        s2 = jnp.dot(kmeta[:, g * LANES:(g + 1) * LANES], rhs, preferred_element_type=jnp.float32) + mmask
        return s1, s2

    heads = []
    pending = [scores(h) for h in range(LOOKAHEAD)]
    for h in range(NA_HEADS):
        s1, s2 = pending.pop(0)
        if h + LOOKAHEAD < NA_HEADS:
            pending.append(scores(h + LOOKAHEAD))
        m = jnp.maximum(jnp.max(s1, axis=0, keepdims=True), jnp.max(s2, axis=0, keepdims=True))
        p1 = jnp.exp2(s1 - m).astype(jnp.bfloat16)
        p2 = jnp.exp2(s2 - m).astype(jnp.bfloat16)
        v1 = jnp.concatenate([vwin[c, h * NA_HD:(h + 1) * NA_HD, :] for c in range(NA_KROWS // 2)], axis=1)
        acc = (jnp.dot(_with_ones(v1), p1, preferred_element_type=jnp.float32)
               + jnp.dot(_with_ones(vtm_ref[0, h * NA_HD:(h + 1) * NA_HD, :]), p2,
                         preferred_element_type=jnp.float32))
        heads.append(acc[0:NA_HD] / acc[NA_HD:NA_HD + 1])
    pairs = [jnp.concatenate(heads[2 * g:2 * g + 2], axis=0).T for g in range(NA_HEADS // 2)]
    o_ref[...] = jnp.concatenate(pairs, axis=1).astype(o_ref.dtype)


def _na_meta_kernel(qt_ref, kf_ref, vt_ref, prev_ref, o_ref):
    del prev_ref
    kr = lax.broadcasted_iota(jnp.int32, (LANES, LANES), 0) // N_META
    qc = lax.broadcasted_iota(jnp.int32, (LANES, LANES), 1) // N_META
    mask = jnp.where(kr == qc, 0.0, NEG).astype(jnp.float32)
    kf = kf_ref[...]
    heads = []
    for h in range(NA_HEADS):
        g, j = h // 2, h % 2
        rhs = _row_band(qt_ref[0, g * LANES:(g + 1) * LANES, :], j * NA_HD, (j + 1) * NA_HD)
        s = jnp.dot(kf[:, g * LANES:(g + 1) * LANES], rhs, preferred_element_type=jnp.float32) + mask
        p = jnp.exp2(s - jnp.max(s, axis=0, keepdims=True)).astype(jnp.bfloat16)
        acc = jnp.dot(_with_ones(vt_ref[0, h * NA_HD:(h + 1) * NA_HD, :]), p, preferred_element_type=jnp.float32)
        heads.append(acc[0:NA_HD] / acc[NA_HD:NA_HD + 1])
    pairs = [jnp.concatenate(heads[2 * g:2 * g + 2], axis=0).T for g in range(NA_HEADS // 2)]
    o_ref[...] = jnp.concatenate(pairs, axis=1).astype(o_ref.dtype)


def _na_call(grp, qt, kf, vt, bias, o_prev):
    n, bsz = grp.n, grp.batch
    grid_rows = n // GRID_W
    steps = grid_rows // NA_QROWS
    tq = NA_QROWS * GRID_W
    meta_blk0 = grp.meta0 // LANES
    per_blk = LANES // N_META
    in_specs = [
        pl.BlockSpec((1, NA_W, tq), lambda b, i: (grp.real0 // TM + b * (n // TM) + i // (TM // tq), 0, i % (TM // tq))),
        pl.BlockSpec((n, NA_W), lambda b, i: (grp.real0 // n + b, 0)),
        pl.BlockSpec((n // LANES, NA_W, LANES), lambda b, i: (grp.real0 // n + b, 0, 0)),
        pl.BlockSpec((LANES, NA_W), lambda b, i: (meta_blk0 + b // per_blk, 0)),
        pl.BlockSpec((1, NA_W, LANES), lambda b, i: (meta_blk0 + b // per_blk, 0, 0)),
        pl.BlockSpec((1, NA_HEADS, NA_KROWS * GRID_W, tq),
                     lambda b, i: (jnp.where(i == 0, 0, jnp.where(i == steps - 1, 2, 1)), 0, 0, 0)),
        pl.BlockSpec(memory_space=pl.ANY),
    ]
    return pl.pallas_call(
        functools.partial(_na_kernel, grid_rows=grid_rows),
        grid=(bsz, steps),
        in_specs=in_specs,
        out_specs=pl.BlockSpec((tq, NA_W), lambda b, i: (grp.real0 // tq + b * steps + i, 0)),
        out_shape=jax.ShapeDtypeStruct(o_prev.shape, o_prev.dtype),
        input_output_aliases={6: 0},
        compiler_params=pltpu.CompilerParams(dimension_semantics=("arbitrary", "arbitrary"),
                                             vmem_limit_bytes=VMEM_LIMIT_BYTES),
        name="na_real",
    )(qt, kf, vt, kf, vt, bias, o_prev)


def _na_meta_call(layout, qt, kf, vt, o_prev):
    blk0 = layout.groups[0].meta0 // LANES
    nblk = sum(_round_up(g.batch * N_META, LANES) for g in layout.groups) // LANES
    sub = TM // LANES
    return pl.pallas_call(
        _na_meta_kernel,
        grid=(nblk,),
        in_specs=[
            pl.BlockSpec((1, NA_W, LANES), lambda i: ((blk0 + i) // sub, 0, (blk0 + i) % sub)),
            pl.BlockSpec((LANES, NA_W), lambda i: (blk0 + i, 0)),
            pl.BlockSpec((1, NA_W, LANES), lambda i: (blk0 + i, 0, 0)),
            pl.BlockSpec(memory_space=pl.ANY),
        ],
        out_specs=pl.BlockSpec((LANES, NA_W), lambda i: (blk0 + i, 0)),
        out_shape=jax.ShapeDtypeStruct(o_prev.shape, o_prev.dtype),
        input_output_aliases={3: 0},
        compiler_params=pltpu.CompilerParams(dimension_semantics=("arbitrary",),
                                             vmem_limit_bytes=VMEM_LIMIT_BYTES),
        name="na_meta",
    )(qt, kf, vt, o_prev)


def _na_bias_tables(rel_bias):
    kc = np.arange(GRID_W)[:, None]
    qc = np.arange(GRID_W)[None, :]
    cs = np.clip(qc - WIN_C // 2, 0, GRID_W - WIN_C)
    col_ok = (kc >= cs) & (kc < cs + WIN_C)
    cidx = np.clip(kc - qc + WIN_C - 1, 0, 2 * WIN_C - 2)
    planes = jnp.where(col_ok[None, None], rel_bias[:, :, cidx] * LOG2E, NEG)
    masked = 2 * WIN_R - 1
    planes = jnp.concatenate([planes, jnp.full((NA_HEADS, 1, GRID_W, GRID_W), NEG, planes.dtype)], axis=1)
    kr = np.arange(NA_KROWS)[:, None]
    qr = np.arange(NA_QROWS)[None, :]
    ridx = []
    for win_lo, q_off in ((0 * qr, qr), (qr, qr + WIN_R // 2), (0 * qr + NA_KROWS - WIN_R, qr + NA_KROWS - NA_QROWS)):
        row_ok = (kr >= win_lo) & (kr < win_lo + WIN_R)
        ridx.append(np.where(row_ok, kr - q_off + WIN_R - 1, masked))
    ridx = np.stack(ridx)
    assert ridx.min() >= 0 and ridx.max() <= masked
    t = planes[:, ridx]
    t = jnp.transpose(t, (1, 0, 2, 4, 3, 5))
    return t.reshape(3, NA_HEADS, NA_KROWS * GRID_W, NA_QROWS * GRID_W).astype(jnp.float32)


def _rope_tables(layout):
    pos = np.zeros((layout.rows,), np.float32)
    for g in layout.groups:
        r = np.arange(g.batch * g.n)
        pos[g.real0:g.real0 + g.batch * g.n] = N_META + r % g.n
        m = np.arange(g.batch * N_META)
        pos[g.meta0:g.meta0 + g.batch * N_META] = m % N_META
    pos = jnp.asarray(pos)

    def tables(dim, period):
        inv = ROPE_THETA ** (-(jnp.arange(0, dim, 2, dtype=jnp.float32) / dim))
        ang = pos[:, None] * inv[None, :]
        cos, sin = jnp.cos(ang), jnp.sin(ang)
        rest = period - dim
        c = jnp.concatenate([cos, cos, jnp.ones((layout.rows, rest), jnp.float32)], axis=1)
        s = jnp.concatenate([-sin, sin, jnp.zeros((layout.rows, rest), jnp.float32)], axis=1)
        reps = LANES // period
        return jnp.tile(c, (1, reps)), jnp.tile(s, (1, reps))

    cda, sda = tables(DA_ROT, DA_HD)
    cm, sm = tables(MLA_ROPE, MLA_ROPE)
    return cda, sda, cm, sm


def _prep_layer(l, norm_g, ffn_w_gate, ffn_w_up, ffn_w_down, w_in, w_out, mla_q_norm_g, mla_kv_norm_g,
                mla_w_uq, mla_w_ukv, da_subln_g):
    bf = jnp.bfloat16
    kr0 = IN_W - MLA_ROPE
    win = jnp.concatenate([w_in[l, :, :kr0]] + [w_in[l, :, kr0:]] * (LANES // MLA_ROPE), axis=1).astype(bf)
    uq = mla_w_uq[l].reshape(Q_LORA, MLA_HEADS, MLA_NOPE + MLA_ROPE)
    uq = jnp.pad(uq, ((0, 0), (0, 0), (0, LANES - MLA_NOPE - MLA_ROPE))).reshape(Q_LORA, MLA_HEADS * LANES)
    ukv = mla_w_ukv[l].reshape(KV_LORA, MLA_HEADS, MLA_NOPE + MLA_VD)
    uk = jnp.pad(ukv[:, :, :MLA_NOPE], ((0, 0), (0, 0), (0, LANES - MLA_NOPE))).reshape(KV_LORA, MLA_HEADS * LANES)
    ukv = jnp.concatenate([uk, ukv[:, :, MLA_NOPE:].reshape(KV_LORA, -1)], axis=1)
    row = lambda v: v.reshape(1, -1)
    return dict(
        g=[row(norm_g[l, i]) for i in range(3)],
        ffn=[(ffn_w_gate[l, i].astype(bf), ffn_w_up[l, i].astype(bf), ffn_w_down[l, i].astype(bf)) for i in range(2)],
        win=win, uq=uq.astype(bf), ukv=ukv.astype(bf),
        gq=row(mla_q_norm_g[l]), gkv=row(mla_kv_norm_g[l]),
        wo=(w_out[l, :NA_W].astype(bf), w_out[l, NA_W:NA_W + DA_W].astype(bf), w_out[l, NA_W + DA_W:].astype(bf)),
        gs=da_subln_g[l].reshape(DA_VD, 1),
    )


def kernel(x_prompt, x_sample, meta_tokens, norm_g, final_norm_g, ffn_w_gate, ffn_w_up, ffn_w_down, w_in, w_out, na_rel_bias, da_lambda, da_subln_g, mla_q_norm_g, mla_kv_norm_g, mla_w_uq, mla_w_ukv):
    xs = (x_prompt, x_sample)
    layout = _make_layout([(x.shape[0], x.shape[1]) for x in xs])
    real_rows = sum(g.batch * g.n for g in layout.groups)
    tail = []
    for g in layout.groups:
        blk = jnp.tile(meta_tokens.astype(jnp.float32), (g.batch, 1))
        tail.append(jnp.pad(blk, ((0, _round_up(g.batch * N_META, LANES) - g.batch * N_META), (0, 0))))
    tail = jnp.concatenate(tail, axis=0)
    tail = jnp.pad(tail, ((0, layout.rows - real_rows - tail.shape[0]), (0, 0)))
    srcs = [x.reshape(-1, D_MODEL) for x in xs] + [tail]
    real_bounds = tuple(g.real0 // TM for g in layout.groups) + (real_rows // TM,)
    bounds = real_bounds + (layout.rows // TM,)
    tabs = _rope_tables(layout)
    gf = final_norm_g.reshape(1, -1)

    for l in range(DEPTH):
        p = _prep_layer(l, norm_g, ffn_w_gate, ffn_w_up, ffn_w_down, w_in, w_out, mla_q_norm_g, mla_kv_norm_g,
                        mla_w_uq, mla_w_ukv, da_subln_g)
        lam_init = 0.8 - 0.6 * math.exp(-0.3 * l)
        (h, naq, nak, nav, daq, dak, dav, mq, mk, mv) = _ffn_inproj(
            srcs, bounds, tabs, p["g"][0], p["g"][1], *p["ffn"][0], p["win"], p["gq"], p["gkv"], p["uq"], p["ukv"])
        bias = _na_bias_tables(na_rel_bias[l])
        o_na = jnp.zeros((layout.rows, NA_W), jnp.bfloat16)
        o_da = jnp.zeros((layout.rows, DA_W), jnp.bfloat16)
        o_m = jnp.zeros((layout.rows, MLA_W), jnp.bfloat16)
        o_na = _na_meta_call(layout, naq, nak, nav, o_na)
        for gi, grp in enumerate(layout.groups):
            o_na = _na_call(grp, naq, nak, nav, bias, o_na)
            for meta_q in (False, True):
                tag = f"g{gi}_{'meta' if meta_q else 'real'}"
                o_da = _flash_call(DA_CFG, grp, daq, dak, dav, (da_lambda[l], p["gs"]), o_da, lam_init, meta_q,
                                   "da_" + tag)
                o_m = _flash_call(MLA_CFG, grp, mq, mk, mv, (), o_m, lam_init, meta_q, "mla_" + tag)
        last = l == DEPTH - 1
        h = _outproj_ffn(h, o_na, o_da, o_m, *p["wo"], p["g"][2], *p["ffn"][1], gf, real_bounds if last else None)
        srcs, bounds = [h], (0, layout.rows // TM)

    return tuple(y.reshape(x.shape) for x, y in zip(xs, h))
```

```python
import functools
import math
from typing import NamedTuple

import jax
import jax.numpy as jnp
import numpy as np
from jax import lax
from jax.experimental import pallas as pl
from jax.experimental.pallas import tpu as pltpu

D_MODEL = 1024
DEPTH = 2
GRID_W = 64
N_META = 16
WIN_R = 8
WIN_C = 16
NA_HEADS = 6
NA_HD = 64
DA_HEADS = 6
DA_HD = 32
DA_VD = 64
DA_ROT = DA_HD // 4
MLA_HEADS = 4
MLA_NOPE = 64
MLA_ROPE = 32
MLA_VD = 64
Q_LORA = 256
KV_LORA = 128
ROPE_THETA = 500000.0
D_FF = 2816
EPS = 1e-6
NA_W = NA_HEADS * NA_HD
DA_W = DA_HEADS * DA_VD
MLA_W = MLA_HEADS * MLA_VD
DA_QK_W = DA_HEADS * 2 * DA_HD
IN_W = 3 * NA_W + 2 * DA_QK_W + DA_W + Q_LORA + KV_LORA + MLA_ROPE

LANES = 128
MXU_DIM = 256
VMEM_LIMIT_BYTES = 58 * 1024 * 1024

TM = 512
TK = 256
NA_QROWS = 4
NA_KROWS = NA_QROWS + WIN_R
LOOKAHEAD = 2
FF_SPLIT = 1536
IN_W_PAD = 2816

LOG2E = 1.4426950408889634
NEG = -1e30
ONES_ROWS = 16
SCORE_BOUND = 60.0
NORM_SLACK = 1.05


def _round_up(x, m):
    return (x + m - 1) // m * m


class Group(NamedTuple):
    batch: int
    n: int
    real0: int
    meta0: int


class Layout(NamedTuple):
    groups: tuple
    rows: int


def _make_layout(shapes):
    row = 0
    real0 = []
    for b, n in shapes:
        assert n % TM == 0 and row % n == 0 and n % GRID_W == 0
        assert (n // GRID_W) % NA_QROWS == 0 and n // GRID_W >= NA_KROWS
        real0.append(row)
        row += b * n
    groups = []
    for (b, n), r0 in zip(shapes, real0):
        groups.append(Group(b, n, r0, row))
        row += _round_up(b * N_META, LANES)
    return Layout(tuple(groups), _round_up(row, TM))


def _rms(x, g):
    return x * lax.rsqrt(jnp.mean(x * x, axis=-1, keepdims=True) + EPS) * g


def _swiglu_half(xn, wg_ref, wu_ref, wd_ref):
    acc = None
    for lo, hi in ((0, FF_SPLIT), (FF_SPLIT, D_FF)):
        gate = jnp.dot(xn, wg_ref[:, lo:hi], preferred_element_type=jnp.float32)
        up = jnp.dot(xn, wu_ref[:, lo:hi], preferred_element_type=jnp.float32)
        hm = (gate * jax.nn.sigmoid(gate) * up).astype(jnp.bfloat16)
        part = jnp.dot(hm, wd_ref[lo:hi, :], preferred_element_type=jnp.float32)
        acc = part if acc is None else acc + part
    return 0.5 * acc


def _rope_chunk(x, c, s, half, period):
    lane = lax.broadcasted_iota(jnp.int32, x.shape, 1)
    lo = (lane & (period - 1)) < half
    partner = jnp.where(lo, pltpu.roll(x, LANES - half, 1), pltpu.roll(x, half, 1))
    return x * c + partner * s


def _segment_specs(bounds):
    return [pl.BlockSpec((TM, D_MODEL), lambda i, lo=lo, hi=hi: (jnp.clip(i - lo, 0, hi - lo - 1), 0))
            for lo, hi in zip(bounds[:-1], bounds[1:])]


def _ffn_inproj_kernel(*refs, bounds):
    n_src = len(bounds) - 1
    srcs = refs[:n_src]
    (cda_ref, sda_ref, cm_ref, sm_ref, ga_ref, gb_ref,
     wg_ref, wu_ref, wd_ref, win_ref, gq_ref, gkv_ref, wuq_ref, wukv_ref,
     h1_ref, naq_ref, nak_ref, nav_ref, daq_ref, dak_ref, dav_ref,
     mq_ref, mk_ref, mv_ref) = refs[n_src:]
    i = pl.program_id(0)
    x = srcs[-1][...]
    for sgm in range(n_src - 2, -1, -1):
        x = jnp.where(i < bounds[sgm + 1], srcs[sgm][...], x)
    xn = _rms(x, ga_ref[...]).astype(jnp.bfloat16)
    h1 = x + _swiglu_half(xn, wg_ref, wu_ref, wd_ref)
    h1_ref[...] = h1
    xn2 = _rms(h1, gb_ref[...]).astype(jnp.bfloat16)

    u = jnp.dot(xn2, win_ref[:, 0:3 * NA_W], preferred_element_type=jnp.float32)
    naq_ref[0] = (u[:, 0:NA_W] * (NA_HD ** -0.5 * LOG2E)).T.astype(jnp.bfloat16)
    nak_ref[...] = u[:, NA_W:2 * NA_W].astype(jnp.bfloat16)
    vt = u[:, 2 * NA_W:3 * NA_W].T.astype(jnp.bfloat16)
    for c in range(TM // LANES):
        nav_ref[c] = vt[:, c * LANES:(c + 1) * LANES]

    o0 = 3 * NA_W
    u = jnp.dot(xn2, win_ref[:, o0:o0 + 2 * DA_QK_W + DA_W], preferred_element_type=jnp.float32)
    cda, sda = cda_ref[...], sda_ref[...]
    q = jnp.concatenate([_rope_chunk(u[:, c * LANES:(c + 1) * LANES], cda, sda, DA_ROT // 2, DA_HD)
                         for c in range(DA_QK_W // LANES)], axis=1)
    k = jnp.concatenate([_rope_chunk(u[:, DA_QK_W + c * LANES:DA_QK_W + (c + 1) * LANES], cda, sda, DA_ROT // 2, DA_HD)
                         for c in range(DA_QK_W // LANES)], axis=1)
    daq_ref[0] = (q * (DA_HD ** -0.5 * LOG2E)).T.astype(jnp.bfloat16)
    dak_ref[...] = k.astype(jnp.bfloat16)
    vt = u[:, 2 * DA_QK_W:2 * DA_QK_W + DA_W].T.astype(jnp.bfloat16)
    for c in range(TM // TK):
        dav_ref[c] = vt[:, c * TK:(c + 1) * TK]

    o1 = o0 + 2 * DA_QK_W + DA_W
    u = jnp.dot(xn2, win_ref[:, o1:IN_W_PAD], preferred_element_type=jnp.float32)
    cm, sm = cm_ref[...], sm_ref[...]
    lane = lax.broadcasted_iota(jnp.int32, cm.shape, 1)
    band = (lane >= MLA_NOPE) & (lane < MLA_NOPE + MLA_ROPE)
    cb, sb = jnp.where(band, cm, 1.0), jnp.where(band, sm, 0.0)
    cq = _rms(u[:, 0:Q_LORA], gq_ref[...]).astype(jnp.bfloat16)
    qm = jnp.dot(cq, wuq_ref[...], preferred_element_type=jnp.float32)
    qm = jnp.concatenate([_rope_chunk(qm[:, h * LANES:(h + 1) * LANES], cb, sb, MLA_ROPE // 2, MLA_ROPE)
                          for h in range(MLA_HEADS)], axis=1) * ((MLA_NOPE + MLA_ROPE) ** -0.5 * LOG2E)
    mq_ref[0] = qm.T.astype(jnp.bfloat16)
    ckv = _rms(u[:, Q_LORA:Q_LORA + KV_LORA], gkv_ref[...]).astype(jnp.bfloat16)
    kv = jnp.dot(ckv, wukv_ref[...], preferred_element_type=jnp.float32)
    kr = _rope_chunk(u[:, Q_LORA + KV_LORA:Q_LORA + KV_LORA + LANES], cm, sm, MLA_ROPE // 2, MLA_ROPE)
    kr = jnp.where(band, kr, 0.0)
    kf = jnp.concatenate([kv[:, h * LANES:(h + 1) * LANES] + kr for h in range(MLA_HEADS)], axis=1)
    mk_ref[...] = kf.astype(jnp.bfloat16)
    vt = kv[:, MLA_HEADS * LANES:MLA_HEADS * LANES + MLA_W].T.astype(jnp.bfloat16)
    for c in range(TM // TK):
        mv_ref[c] = vt[:, c * TK:(c + 1) * TK]


def _outproj_ffn_kernel(h_ref, ona_ref, oda_ref, om_ref, wo1_ref, wo2_ref, wo3_ref, g_ref,
                        wg_ref, wu_ref, wd_ref, gf_ref, *o_refs, out_bounds):
    h = h_ref[...]
    h = h + jnp.dot(ona_ref[...], wo1_ref[...], preferred_element_type=jnp.float32)
    h = h + jnp.dot(oda_ref[...], wo2_ref[...], preferred_element_type=jnp.float32)
    h = h + jnp.dot(om_ref[...], wo3_ref[...], preferred_element_type=jnp.float32)
    xn = _rms(h, g_ref[...]).astype(jnp.bfloat16)
    h = h + _swiglu_half(xn, wg_ref, wu_ref, wd_ref)
    if out_bounds is None:
        o_refs[0][...] = h
    else:
        h = _rms(h, gf_ref[...])
        i = pl.program_id(0)
        for o_ref, lo, hi in zip(o_refs, out_bounds[:-1], out_bounds[1:]):
            @pl.when((i >= lo) & (i < hi))
            def _(o_ref=o_ref):
                o_ref[...] = h


def _const_spec(shape):
    nd = len(shape)
    return pl.BlockSpec(shape, lambda i: (0,) * nd, pipeline_mode=pl.Buffered(1))


def _ffn_inproj(srcs, bounds, tabs, ga, gb, wg, wu, wd, win, gq, gkv, wuq, wukv):
    nt = bounds[-1]
    rows = nt * TM
    row_spec = lambda w: pl.BlockSpec((TM, w), lambda i: (i, 0))
    tile_spec = lambda r: pl.BlockSpec((1, r, TM), lambda i: (i, 0, 0))
    bf = jnp.bfloat16
    out_shape = (
        jax.ShapeDtypeStruct((rows, D_MODEL), jnp.float32),
        jax.ShapeDtypeStruct((nt, NA_W, TM), bf),
        jax.ShapeDtypeStruct((rows, NA_W), bf),
        jax.ShapeDtypeStruct((rows // LANES, NA_W, LANES), bf),
        jax.ShapeDtypeStruct((nt, DA_QK_W, TM), bf),
        jax.ShapeDtypeStruct((rows, DA_QK_W), bf),
        jax.ShapeDtypeStruct((rows // TK, DA_W, TK), bf),
        jax.ShapeDtypeStruct((nt, MLA_HEADS * LANES, TM), bf),
        jax.ShapeDtypeStruct((rows, 2 * MXU_DIM), bf),
        jax.ShapeDtypeStruct((rows // TK, MLA_W, TK), bf),
    )
    out_specs = (
        row_spec(D_MODEL),
        tile_spec(NA_W),
        row_spec(NA_W),
        pl.BlockSpec((TM // LANES, NA_W, LANES), lambda i: (i, 0, 0)),
        tile_spec(DA_QK_W),
        row_spec(DA_QK_W),
        pl.BlockSpec((TM // TK, DA_W, TK), lambda i: (i, 0, 0)),
        tile_spec(MLA_HEADS * LANES),
        row_spec(2 * MXU_DIM),
        pl.BlockSpec((TM // TK, MLA_W, TK), lambda i: (i, 0, 0)),
    )
    in_specs = _segment_specs(bounds) + [row_spec(LANES)] * 4 + [
        _const_spec(a.shape) for a in (ga, gb, wg, wu, wd, win, gq, gkv, wuq, wukv)]
    return pl.pallas_call(
        functools.partial(_ffn_inproj_kernel, bounds=bounds),
        grid=(nt,),
        in_specs=in_specs,
        out_specs=out_specs,
        out_shape=out_shape,
        compiler_params=pltpu.CompilerParams(dimension_semantics=("arbitrary",),
                                             vmem_limit_bytes=VMEM_LIMIT_BYTES),
        name="ffn_inproj",
    )(*srcs, *tabs, ga, gb, wg, wu, wd, win, gq, gkv, wuq, wukv)


def _outproj_ffn(h, ona, oda, om, wo1, wo2, wo3, g, wg, wu, wd, gf, out_bounds):
    rows = h.shape[0]
    row_spec = lambda w: pl.BlockSpec((TM, w), lambda i: (i, 0))
    if out_bounds is None:
        out_specs = row_spec(D_MODEL)
        out_shape = jax.ShapeDtypeStruct((rows, D_MODEL), jnp.float32)
    else:
        out_specs = tuple(_segment_specs(out_bounds))
        out_shape = tuple(jax.ShapeDtypeStruct(((hi - lo) * TM, D_MODEL), jnp.float32)
                          for lo, hi in zip(out_bounds[:-1], out_bounds[1:]))
    in_specs = [row_spec(D_MODEL), row_spec(NA_W), row_spec(DA_W), row_spec(MLA_W)] + [
        _const_spec(a.shape) for a in (wo1, wo2, wo3, g, wg, wu, wd, gf)]
    return pl.pallas_call(
        functools.partial(_outproj_ffn_kernel, out_bounds=out_bounds),
        grid=(rows // TM,),
        in_specs=in_specs,
        out_specs=out_specs,
        out_shape=out_shape,
        compiler_params=pltpu.CompilerParams(dimension_semantics=("arbitrary",),
                                             vmem_limit_bytes=VMEM_LIMIT_BYTES),
        name="outproj_ffn",
    )(h, ona, oda, om, wo1, wo2, wo3, g, wg, wu, wd, gf)


def _row_band(block, lo, hi):
    row = lax.broadcasted_iota(jnp.int32, block.shape, 0)
    return jnp.where((row >= lo) & (row < hi), block, jnp.zeros_like(block))


def _with_ones(vt):
    return jnp.concatenate([vt, jnp.ones((ONES_ROWS, vt.shape[1]), vt.dtype)], axis=0)


def _meta_key_mask(b):
    row = lax.broadcasted_iota(jnp.int32, (LANES, 1), 0)
    lo = (b % (LANES // N_META)) * N_META
    return jnp.where((row >= lo) & (row < lo + N_META), 0.0, NEG).astype(jnp.float32)


class FlashCfg(NamedTuple):
    n_soft: int
    group_w: int
    soft_per_group: int
    v_of_soft: tuple
    out_w: int
    unroll: int


DA_CFG = FlashCfg(2 * DA_HEADS, LANES, 4, tuple(i // 2 for i in range(2 * DA_HEADS)), DA_W, 4)
MLA_CFG = FlashCfg(MLA_HEADS, LANES, 1, tuple(range(MLA_HEADS)), MLA_W, 8)


def _build_rhs(cfg, qt_ref, rhs_ref):
    for i in range(cfg.n_soft):
        g, j = i // cfg.soft_per_group, i % cfg.soft_per_group
        if cfg is DA_CFG:
            rhs_ref[i] = _row_band(qt_ref[0, g * LANES:(g + 1) * LANES, :], j * DA_HD, (j + 1) * DA_HD)
        else:
            rhs_ref[i] = qt_ref[0, i * LANES:(i + 1) * LANES, :]


def _flash_tiles(cfg, tiles, rhs_ref, m_ref, acc_ref, bounded):
    nt = len(tiles)
    if bounded:
        work = [(t, i) for i in range(cfg.n_soft) for t in range(nt)]
    else:
        work = [(t, i) for t in range(nt) for i in range(cfg.n_soft)]

    def score(w):
        kblk_of, _, mask = tiles[w[0]]
        s = jnp.dot(kblk_of(w[1] // cfg.soft_per_group), rhs_ref[w[1]], preferred_element_type=jnp.float32)
        return s if mask is None else s + mask

    pending = [score(w) for w in work[:LOOKAHEAD]]
    part = None
    for n, (t, i) in enumerate(work):
        vt = _with_ones(tiles[t][1](cfg.v_of_soft[i]))
        s = pending.pop(0)
        if n + LOOKAHEAD < len(work):
            pending.append(score(work[n + LOOKAHEAD]))
        if bounded:
            pv = jnp.dot(vt, jnp.exp2(s).astype(jnp.bfloat16), preferred_element_type=jnp.float32)
            part = pv if t == 0 else part + pv
            if t == nt - 1:
                acc_ref[i] = acc_ref[i] + part
        else:
            m_prev = m_ref[i]
            m_new = jnp.maximum(m_prev, jnp.max(s, axis=0, keepdims=True))
            alpha = jnp.exp2(m_prev - m_new)
            p = jnp.exp2(s - m_new).astype(jnp.bfloat16)
            acc_ref[i] = alpha * acc_ref[i] + jnp.dot(vt, p, preferred_element_type=jnp.float32)
            m_ref[i] = m_new


def _feature_indicator(cfg, g):
    f = lax.broadcasted_iota(jnp.int32, (cfg.group_w, LANES), 0)
    j = lax.broadcasted_iota(jnp.int32, (cfg.group_w, LANES), 1)
    del g
    if cfg is DA_CFG:
        hit = jnp.right_shift(f, int(math.log2(DA_HD))) == j
    else:
        hit = j == 0
    return jnp.where(hit & (j < cfg.soft_per_group), 1.0, 0.0).astype(jnp.bfloat16)


def _max_key_norms(cfg, kf_ref, kfm_ref, n_keys, kn_ref):
    gw = cfg.group_w
    n_groups = cfg.n_soft // cfg.soft_per_group
    inds = [_feature_indicator(cfg, g) for g in range(n_groups)]

    def sq_norms(blk, g):
        x = blk.astype(jnp.float32)
        n2 = jnp.dot((x * x).astype(jnp.bfloat16), inds[g], preferred_element_type=jnp.float32)
        return jnp.max(n2, axis=0, keepdims=True)

    def body(c, carry):
        r0 = pl.multiple_of(c * TM, TM)
        return tuple(jnp.maximum(carry[g], sq_norms(kf_ref[pl.ds(r0, TM), g * gw:(g + 1) * gw], g))
                     for g in range(n_groups))

    init = tuple(sq_norms(kfm_ref[:, g * gw:(g + 1) * gw], g) for g in range(n_groups))
    kmax = lax.fori_loop(0, n_keys // TM, body, init)
    lane = lax.broadcasted_iota(jnp.int32, (1, LANES), 1)
    for i in range(cfg.n_soft):
        g, j = divmod(i, cfg.soft_per_group)
        v = jnp.max(jnp.where(lane == j, kmax[g], 0.0), axis=1, keepdims=True)
        kn_ref[i] = jnp.broadcast_to(v, (1, LANES))


def _scores_are_bounded(cfg, rhs_ref, kn_ref):
    ok = None
    for i in range(cfg.n_soft):
        r = rhs_ref[i].astype(jnp.float32)
        bound2 = jnp.sum(r * r, axis=0, keepdims=True) * kn_ref[i][:, 0:1] * NORM_SLACK
        good = bound2 <= SCORE_BOUND * SCORE_BOUND
        ok = good if ok is None else ok & good
    return jnp.min(jnp.where(ok, 1.0, 0.0)) > 0.5


def _flash_kernel(*refs, cfg, n_keys, lam_init, meta_queries):
    if cfg is DA_CFG:
        (qt_ref, kf_ref, vt_ref, kfm_ref, vtm_ref, lam_ref, gs_ref, o_ref,
         rhs_ref, m_ref, acc_ref, kn_ref, *rest) = refs
    else:
        qt_ref, kf_ref, vt_ref, kfm_ref, vtm_ref, o_ref, rhs_ref, m_ref, acc_ref, kn_ref, *rest = refs
    b = pl.program_id(0)
    tq = qt_ref.shape[2]
    vd = DA_VD
    gw = cfg.group_w
    if meta_queries:
        _max_key_norms(cfg, kf_ref, kfm_ref, n_keys, kn_ref)
    else:
        @pl.when(pl.program_id(1) == 0)
        def _():
            _max_key_norms(cfg, kf_ref, kfm_ref, n_keys, kn_ref)
    _build_rhs(cfg, qt_ref, rhs_ref)
    m_ref[...] = jnp.full(m_ref.shape, NEG, jnp.float32)
    acc_ref[...] = jnp.zeros(acc_ref.shape, jnp.float32)

    def all_keys(bounded, unroll):
        _flash_tiles(
            cfg,
            [(lambda g: kfm_ref[:, g * gw:(g + 1) * gw],
              lambda h: vtm_ref[0, h * vd:(h + 1) * vd, :],
              _meta_key_mask(b))],
            rhs_ref, m_ref, acc_ref, bounded)

        def body(it, carry):
            tiles = []
            for u in range(unroll):
                kt = it * unroll + u
                k0 = pl.multiple_of(kt * TK, TK)
                tiles.append((
                    lambda g, k0=k0: kf_ref[pl.ds(k0, TK), g * gw:(g + 1) * gw],
                    lambda h, kt=kt: vt_ref[kt, h * vd:(h + 1) * vd, :],
                    None))
            _flash_tiles(cfg, tiles, rhs_ref, m_ref, acc_ref, bounded)
            return carry

        lax.fori_loop(0, n_keys // (TK * unroll), body, 0)

    lax.cond(_scores_are_bounded(cfg, rhs_ref, kn_ref),
             lambda: all_keys(True, math.gcd(cfg.unroll, n_keys // TK)),
             lambda: all_keys(False, 1))

    heads = []
    if cfg is DA_CFG:
        lp = lam_ref[...]
        lam = (jnp.exp(jnp.sum(lp[0:1] * lp[1:2], axis=1, keepdims=True))
               - jnp.exp(jnp.sum(lp[2:3] * lp[3:4], axis=1, keepdims=True)) + lam_init)
        for h in range(DA_HEADS):
            a1, a2 = acc_ref[2 * h], acc_ref[2 * h + 1]
            o = a1[0:vd] / a1[vd:vd + 1] - lam * (a2[0:vd] / a2[vd:vd + 1])
            o = o * lax.rsqrt(jnp.mean(o * o, axis=0, keepdims=True) + EPS) * gs_ref[...]
            heads.append(o * (1.0 - lam_init))
    else:
        for h in range(MLA_HEADS):
            a = acc_ref[h]
            heads.append(a[0:vd] / a[vd:vd + 1])
    pairs = [jnp.concatenate(heads[2 * g:2 * g + 2], axis=0).T for g in range(len(heads) // 2)]
    out = jnp.concatenate(pairs, axis=1)
    if meta_queries:
        stage_ref = rest[0]
        stage_ref[...] = out
        r0 = pl.multiple_of((b % (LANES // N_META)) * N_META, N_META)
        o_ref[...] = stage_ref[pl.ds(r0, N_META), :].astype(o_ref.dtype)
    else:
        o_ref[...] = out.astype(o_ref.dtype)


def _flash_call(cfg, grp, qt, kf, vt, extra, o_prev, lam_init, meta_queries, name):
    n, bsz = grp.n, grp.batch
    qw, kw, vw = qt.shape[1], kf.shape[1], vt.shape[1]
    tiles_per_seq = n // TM
    meta_blk0 = grp.meta0 // LANES
    per_blk = LANES // N_META
    if meta_queries:
        grid = (bsz,)
        tq = LANES
        q_map = lambda b: ((meta_blk0 + b // per_blk) // (TM // LANES), 0, (meta_blk0 + b // per_blk) % (TM // LANES))
        o_spec = pl.BlockSpec((N_META, cfg.out_w), lambda b: (grp.meta0 // N_META + b, 0))
        fix = lambda f: (lambda b: f(b))
        sem = ("arbitrary",)
    else:
        grid = (bsz, tiles_per_seq)
        tq = TM
        q_map = lambda b, i: (grp.real0 // TM + b * tiles_per_seq + i, 0, 0)
        o_spec = pl.BlockSpec((TM, cfg.out_w), lambda b, i: (grp.real0 // TM + b * tiles_per_seq + i, 0))
        fix = lambda f: (lambda b, i: f(b))
        sem = ("arbitrary", "arbitrary")
    in_specs = [
        pl.BlockSpec((1, qw, tq), q_map),
        pl.BlockSpec((n, kw), fix(lambda b: (grp.real0 // n + b, 0))),
        pl.BlockSpec((n // TK, vw, TK), fix(lambda b: (grp.real0 // n + b, 0, 0))),
        pl.BlockSpec((LANES, kw), fix(lambda b: (meta_blk0 + b // per_blk, 0))),
        pl.BlockSpec((1, vw, LANES), fix(lambda b: ((meta_blk0 + b // per_blk) // (TK // LANES), 0,
                                                     (meta_blk0 + b // per_blk) % (TK // LANES)))),
    ]
    args = [qt, kf, vt, kf, vt]
    for a in extra:
        in_specs.append(pl.BlockSpec(a.shape, fix(lambda b, nd=a.ndim: (0,) * nd)))
        args.append(a)
    in_specs.append(pl.BlockSpec(memory_space=pl.ANY))
    args.append(o_prev)
    scratch = [
        pltpu.VMEM((cfg.n_soft, cfg.group_w, tq), jnp.bfloat16),
        pltpu.VMEM((cfg.n_soft, 1, tq), jnp.float32),
        pltpu.VMEM((cfg.n_soft, DA_VD + ONES_ROWS, tq), jnp.float32),
        pltpu.VMEM((cfg.n_soft, 1, LANES), jnp.float32),
    ]
    if meta_queries:
        scratch.append(pltpu.VMEM((LANES, cfg.out_w), jnp.float32))

    def body(*refs):
        n_in = len(args)
        ins, rest = refs[:n_in - 1], refs[n_in:]
        _flash_kernel(*ins, *rest, cfg=cfg, n_keys=n, lam_init=lam_init, meta_queries=meta_queries)

    return pl.pallas_call(
        body,
        grid=grid,
        in_specs=in_specs,
        out_specs=o_spec,
        out_shape=jax.ShapeDtypeStruct(o_prev.shape, o_prev.dtype),
        scratch_shapes=scratch,
        input_output_aliases={len(args) - 1: 0},
        compiler_params=pltpu.CompilerParams(dimension_semantics=sem, vmem_limit_bytes=VMEM_LIMIT_BYTES),
        name=name,
    )(*args)


def _na_kernel(qt_ref, kf_ref, vt_ref, kfm_ref, vtm_ref, bias_ref, prev_ref, o_ref, *, grid_rows):
    del prev_ref
    b, i = pl.program_id(0), pl.program_id(1)
    ws = jnp.clip(i * NA_QROWS - WIN_R // 2, 0, grid_rows - NA_KROWS)
    kwin = kf_ref[pl.ds(pl.multiple_of(ws * GRID_W, 2 * GRID_W), NA_KROWS * GRID_W), :]
    vwin = vt_ref[pl.ds(ws // 2, NA_KROWS // 2)]
    kmeta = kfm_ref[...]
    mmask = _meta_key_mask(b)
    def scores(h):
        g, j = h // 2, h % 2
        rhs = _row_band(qt_ref[0, g * LANES:(g + 1) * LANES, :], j * NA_HD, (j + 1) * NA_HD)
        s1 = jnp.dot(kwin[:, g * LANES:(g + 1) * LANES], rhs, preferred_element_type=jnp.float32) + bias_ref[0, h]
        s2 = jnp.dot(kmeta[:, g * LANES:(g + 1) * LANES], rhs, preferred_element_type=jnp.float32) + mmask
        return s1, s2

    heads = []
    pending = [scores(h) for h in range(LOOKAHEAD)]
    for h in range(NA_HEADS):
        s1, s2 = pending.pop(0)
        if h + LOOKAHEAD < NA_HEADS:
            pending.append(scores(h + LOOKAHEAD))
        m = jnp.maximum(jnp.max(s1, axis=0, keepdims=True), jnp.max(s2, axis=0, keepdims=True))
        p1 = jnp.exp2(s1 - m).astype(jnp.bfloat16)
        p2 = jnp.exp2(s2 - m).astype(jnp.bfloat16)
        v1 = jnp.concatenate([vwin[c, h * NA_HD:(h + 1) * NA_HD, :] for c in range(NA_KROWS // 2)], axis=1)
        acc = (jnp.dot(_with_ones(v1), p1, preferred_element_type=jnp.float32)
               + jnp.dot(_with_ones(vtm_ref[0, h * NA_HD:(h + 1) * NA_HD, :]), p2,
                         preferred_element_type=jnp.float32))
        heads.append(acc[0:NA_HD] / acc[NA_HD:NA_HD + 1])
    pairs = [jnp.concatenate(heads[2 * g:2 * g + 2], axis=0).T for g in range(NA_HEADS // 2)]
    o_ref[...] = jnp.concatenate(pairs, axis=1).astype(o_ref.dtype)


def _na_meta_kernel(qt_ref, kf_ref, vt_ref, prev_ref, o_ref):
    del prev_ref
    kr = lax.broadcasted_iota(jnp.int32, (LANES, LANES), 0) // N_META
    qc = lax.broadcasted_iota(jnp.int32, (LANES, LANES), 1) // N_META
    mask = jnp.where(kr == qc, 0.0, NEG).astype(jnp.float32)
    kf = kf_ref[...]
    heads = []
    for h in range(NA_HEADS):
        g, j = h // 2, h % 2
        rhs = _row_band(qt_ref[0, g * LANES:(g + 1) * LANES, :], j * NA_HD, (j + 1) * NA_HD)
        s = jnp.dot(kf[:, g * LANES:(g + 1) * LANES], rhs, preferred_element_type=jnp.float32) + mask
        p = jnp.exp2(s - jnp.max(s, axis=0, keepdims=True)).astype(jnp.bfloat16)
        acc = jnp.dot(_with_ones(vt_ref[0, h * NA_HD:(h + 1) * NA_HD, :]), p, preferred_element_type=jnp.float32)
        heads.append(acc[0:NA_HD] / acc[NA_HD:NA_HD + 1])
    pairs = [jnp.concatenate(heads[2 * g:2 * g + 2], axis=0).T for g in range(NA_HEADS // 2)]
    o_ref[...] = jnp.concatenate(pairs, axis=1).astype(o_ref.dtype)


def _na_call(grp, qt, kf, vt, bias, o_prev):
    n, bsz = grp.n, grp.batch
    grid_rows = n // GRID_W
    steps = grid_rows // NA_QROWS
    tq = NA_QROWS * GRID_W
    meta_blk0 = grp.meta0 // LANES
    per_blk = LANES // N_META
    in_specs = [
        pl.BlockSpec((1, NA_W, tq), lambda b, i: (grp.real0 // TM + b * (n // TM) + i // (TM // tq), 0, i % (TM // tq))),
        pl.BlockSpec((n, NA_W), lambda b, i: (grp.real0 // n + b, 0)),
        pl.BlockSpec((n // LANES, NA_W, LANES), lambda b, i: (grp.real0 // n + b, 0, 0)),
        pl.BlockSpec((LANES, NA_W), lambda b, i: (meta_blk0 + b // per_blk, 0)),
        pl.BlockSpec((1, NA_W, LANES), lambda b, i: (meta_blk0 + b // per_blk, 0, 0)),
        pl.BlockSpec((1, NA_HEADS, NA_KROWS * GRID_W, tq),
                     lambda b, i: (jnp.where(i == 0, 0, jnp.where(i == steps - 1, 2, 1)), 0, 0, 0)),
        pl.BlockSpec(memory_space=pl.ANY),
    ]
    return pl.pallas_call(
        functools.partial(_na_kernel, grid_rows=grid_rows),
        grid=(bsz, steps),
        in_specs=in_specs,
        out_specs=pl.BlockSpec((tq, NA_W), lambda b, i: (grp.real0 // tq + b * steps + i, 0)),
        out_shape=jax.ShapeDtypeStruct(o_prev.shape, o_prev.dtype),
        input_output_aliases={6: 0},
        compiler_params=pltpu.CompilerParams(dimension_semantics=("arbitrary", "arbitrary"),
                                             vmem_limit_bytes=VMEM_LIMIT_BYTES),
        name="na_real",
    )(qt, kf, vt, kf, vt, bias, o_prev)


def _na_meta_call(layout, qt, kf, vt, o_prev):
    blk0 = layout.groups[0].meta0 // LANES
    nblk = sum(_round_up(g.batch * N_META, LANES) for g in layout.groups) // LANES
    sub = TM // LANES
    return pl.pallas_call(
        _na_meta_kernel,
        grid=(nblk,),
        in_specs=[
            pl.BlockSpec((1, NA_W, LANES), lambda i: ((blk0 + i) // sub, 0, (blk0 + i) % sub)),
            pl.BlockSpec((LANES, NA_W), lambda i: (blk0 + i, 0)),
            pl.BlockSpec((1, NA_W, LANES), lambda i: (blk0 + i, 0, 0)),
            pl.BlockSpec(memory_space=pl.ANY),
        ],
        out_specs=pl.BlockSpec((LANES, NA_W), lambda i: (blk0 + i, 0)),
        out_shape=jax.ShapeDtypeStruct(o_prev.shape, o_prev.dtype),
        input_output_aliases={3: 0},
        compiler_params=pltpu.CompilerParams(dimension_semantics=("arbitrary",),
                                             vmem_limit_bytes=VMEM_LIMIT_BYTES),
        name="na_meta",
    )(qt, kf, vt, o_prev)


def _na_bias_tables(rel_bias):
    kc = np.arange(GRID_W)[:, None]
    qc = np.arange(GRID_W)[None, :]
    cs = np.clip(qc - WIN_C // 2, 0, GRID_W - WIN_C)
    col_ok = (kc >= cs) & (kc < cs + WIN_C)
    cidx = np.clip(kc - qc + WIN_C - 1, 0, 2 * WIN_C - 2)
    planes = jnp.where(col_ok[None, None], rel_bias[:, :, cidx] * LOG2E, NEG)
    masked = 2 * WIN_R - 1
    planes = jnp.concatenate([planes, jnp.full((NA_HEADS, 1, GRID_W, GRID_W), NEG, planes.dtype)], axis=1)
    kr = np.arange(NA_KROWS)[:, None]
    qr = np.arange(NA_QROWS)[None, :]
    ridx = []
    for win_lo, q_off in ((0 * qr, qr), (qr, qr + WIN_R // 2), (0 * qr + NA_KROWS - WIN_R, qr + NA_KROWS - NA_QROWS)):
        row_ok = (kr >= win_lo) & (kr < win_lo + WIN_R)
        ridx.append(np.where(row_ok, kr - q_off + WIN_R - 1, masked))
    ridx = np.stack(ridx)
    assert ridx.min() >= 0 and ridx.max() <= masked
    t = planes[:, ridx]
    t = jnp.transpose(t, (1, 0, 2, 4, 3, 5))
    return t.reshape(3, NA_HEADS, NA_KROWS * GRID_W, NA_QROWS * GRID_W).astype(jnp.float32)


def _rope_tables(layout):
    pos = np.zeros((layout.rows,), np.float32)
    for g in layout.groups:
        r = np.arange(g.batch * g.n)
        pos[g.real0:g.real0 + g.batch * g.n] = N_META + r % g.n
        m = np.arange(g.batch * N_META)
        pos[g.meta0:g.meta0 + g.batch * N_META] = m % N_META
    pos = jnp.asarray(pos)

    def tables(dim, period):
        inv = ROPE_THETA ** (-(jnp.arange(0, dim, 2, dtype=jnp.float32) / dim))
        ang = pos[:, None] * inv[None, :]
        cos, sin = jnp.cos(ang), jnp.sin(ang)
        rest = period - dim
        c = jnp.concatenate([cos, cos, jnp.ones((layout.rows, rest), jnp.float32)], axis=1)
        s = jnp.concatenate([-sin, sin, jnp.zeros((layout.rows, rest), jnp.float32)], axis=1)
        reps = LANES // period
        return jnp.tile(c, (1, reps)), jnp.tile(s, (1, reps))

    cda, sda = tables(DA_ROT, DA_HD)
    cm, sm = tables(MLA_ROPE, MLA_ROPE)
    return cda, sda, cm, sm


def _prep_layer(l, norm_g, ffn_w_gate, ffn_w_up, ffn_w_down, w_in, w_out, mla_q_norm_g, mla_kv_norm_g,
                mla_w_uq, mla_w_ukv, da_subln_g):
    bf = jnp.bfloat16
    kr0 = IN_W - MLA_ROPE
    win = jnp.concatenate([w_in[l, :, :kr0]] + [w_in[l, :, kr0:]] * (LANES // MLA_ROPE), axis=1).astype(bf)
    uq = mla_w_uq[l].reshape(Q_LORA, MLA_HEADS, MLA_NOPE + MLA_ROPE)
    uq = jnp.pad(uq, ((0, 0), (0, 0), (0, LANES - MLA_NOPE - MLA_ROPE))).reshape(Q_LORA, MLA_HEADS * LANES)
    ukv = mla_w_ukv[l].reshape(KV_LORA, MLA_HEADS, MLA_NOPE + MLA_VD)
    uk = jnp.pad(ukv[:, :, :MLA_NOPE], ((0, 0), (0, 0), (0, LANES - MLA_NOPE))).reshape(KV_LORA, MLA_HEADS * LANES)
    ukv = jnp.concatenate([uk, ukv[:, :, MLA_NOPE:].reshape(KV_LORA, -1)], axis=1)
    row = lambda v: v.reshape(1, -1)
    return dict(
        g=[row(norm_g[l, i]) for i in range(3)],
        ffn=[(ffn_w_gate[l, i].astype(bf), ffn_w_up[l, i].astype(bf), ffn_w_down[l, i].astype(bf)) for i in range(2)],
        win=win, uq=uq.astype(bf), ukv=ukv.astype(bf),
        gq=row(mla_q_norm_g[l]), gkv=row(mla_kv_norm_g[l]),
        wo=(w_out[l, :NA_W].astype(bf), w_out[l, NA_W:NA_W + DA_W].astype(bf), w_out[l, NA_W + DA_W:].astype(bf)),
        gs=da_subln_g[l].reshape(DA_VD, 1),
    )


def kernel(x_prompt, x_sample, meta_tokens, norm_g, final_norm_g, ffn_w_gate, ffn_w_up, ffn_w_down, w_in, w_out, na_rel_bias, da_lambda, da_subln_g, mla_q_norm_g, mla_kv_norm_g, mla_w_uq, mla_w_ukv):
    xs = (x_prompt, x_sample)
    layout = _make_layout([(x.shape[0], x.shape[1]) for x in xs])
    real_rows = sum(g.batch * g.n for g in layout.groups)
    tail = []
    for g in layout.groups:
        blk = jnp.tile(meta_tokens.astype(jnp.float32), (g.batch, 1))
        tail.append(jnp.pad(blk, ((0, _round_up(g.batch * N_META, LANES) - g.batch * N_META), (0, 0))))
    tail = jnp.concatenate(tail, axis=0)
    tail = jnp.pad(tail, ((0, layout.rows - real_rows - tail.shape[0]), (0, 0)))
    srcs = [x.reshape(-1, D_MODEL) for x in xs] + [tail]
    real_bounds = tuple(g.real0 // TM for g in layout.groups) + (real_rows // TM,)
    bounds = real_bounds + (layout.rows // TM,)
    tabs = _rope_tables(layout)
    gf = final_norm_g.reshape(1, -1)

    for l in range(DEPTH):
        p = _prep_layer(l, norm_g, ffn_w_gate, ffn_w_up, ffn_w_down, w_in, w_out, mla_q_norm_g, mla_kv_norm_g,
                        mla_w_uq, mla_w_ukv, da_subln_g)
        lam_init = 0.8 - 0.6 * math.exp(-0.3 * l)
        (h, naq, nak, nav, daq, dak, dav, mq, mk, mv) = _ffn_inproj(
            srcs, bounds, tabs, p["g"][0], p["g"][1], *p["ffn"][0], p["win"], p["gq"], p["gkv"], p["uq"], p["ukv"])
        bias = _na_bias_tables(na_rel_bias[l])
        o_na = jnp.zeros((layout.rows, NA_W), jnp.bfloat16)
        o_da = jnp.zeros((layout.rows, DA_W), jnp.bfloat16)
        o_m = jnp.zeros((layout.rows, MLA_W), jnp.bfloat16)
        o_na = _na_meta_call(layout, naq, nak, nav, o_na)
        for gi, grp in enumerate(layout.groups):
            o_na = _na_call(grp, naq, nak, nav, bias, o_na)
            for meta_q in (False, True):
                tag = f"g{gi}_{'meta' if meta_q else 'real'}"
                o_da = _flash_call(DA_CFG, grp, daq, dak, dav, (da_lambda[l], p["gs"]), o_da, lam_init, meta_q,
                                   "da_" + tag)
                o_m = _flash_call(MLA_CFG, grp, mq, mk, mv, (), o_m, lam_init, meta_q, "mla_" + tag)
        last = l == DEPTH - 1
        h = _outproj_ffn(h, o_na, o_da, o_m, *p["wo"], p["g"][2], *p["ffn"][1], gf, real_bounds if last else None)
        srcs, bounds = [h], (0, layout.rows // TM)

    return tuple(y.reshape(x.shape) for x, y in zip(xs, h))
```

```python
import functools
import math
from typing import NamedTuple

import jax
import jax.numpy as jnp
import numpy as np
from jax import lax
from jax.experimental import pallas as pl
from jax.experimental.pallas import tpu as pltpu

D_MODEL = 1024
DEPTH = 2
GRID_W = 64
N_META = 16
WIN_R = 8
WIN_C = 16
NA_HEADS = 6
NA_HD = 64
DA_HEADS = 6
DA_HD = 32
DA_VD = 64
DA_ROT = DA_HD // 4
MLA_HEADS = 4
MLA_NOPE = 64
MLA_ROPE = 32
MLA_VD = 64
Q_LORA = 256
KV_LORA = 128
ROPE_THETA = 500000.0
D_FF = 2816
EPS = 1e-6
NA_W = NA_HEADS * NA_HD
DA_W = DA_HEADS * DA_VD
MLA_W = MLA_HEADS * MLA_VD
DA_QK_W = DA_HEADS * 2 * DA_HD
IN_W = 3 * NA_W + 2 * DA_QK_W + DA_W + Q_LORA + KV_LORA + MLA_ROPE

LANES = 128
MXU_DIM = 256
VMEM_LIMIT_BYTES = 58 * 1024 * 1024

TM = 512
TK = 256
NA_QROWS = 4
NA_KROWS = NA_QROWS + WIN_R
LOOKAHEAD = 2
FF_SPLIT = 1536
IN_W_PAD = 2816

LOG2E = 1.4426950408889634
NEG = -1e30
ONES_ROWS = 16
SCORE_BOUND = 60.0
NORM_SLACK = 1.05


def _round_up(x, m):
    return (x + m - 1) // m * m


class Group(NamedTuple):
    batch: int
    n: int
    real0: int
    meta0: int


class Layout(NamedTuple):
    groups: tuple
    rows: int


def _make_layout(shapes):
    row = 0
    real0 = []
    for b, n in shapes:
        assert n % TM == 0 and row % n == 0 and n % GRID_W == 0
        assert (n // GRID_W) % NA_QROWS == 0 and n // GRID_W >= NA_KROWS
        real0.append(row)
        row += b * n
    groups = []
    for (b, n), r0 in zip(shapes, real0):
        groups.append(Group(b, n, r0, row))
        row += _round_up(b * N_META, LANES)
    return Layout(tuple(groups), _round_up(row, TM))


def _rms(x, g):
    return x * lax.rsqrt(jnp.mean(x * x, axis=-1, keepdims=True) + EPS) * g


def _swiglu_half(xn, wg_ref, wu_ref, wd_ref):
    acc = None
    for lo, hi in ((0, FF_SPLIT), (FF_SPLIT, D_FF)):
        gate = jnp.dot(xn, wg_ref[:, lo:hi], preferred_element_type=jnp.float32)
        up = jnp.dot(xn, wu_ref[:, lo:hi], preferred_element_type=jnp.float32)
        hm = (gate * jax.nn.sigmoid(gate) * up).astype(jnp.bfloat16)
        part = jnp.dot(hm, wd_ref[lo:hi, :], preferred_element_type=jnp.float32)
        acc = part if acc is None else acc + part
    return 0.5 * acc


def _rope_chunk(x, c, s, half, period):
    lane = lax.broadcasted_iota(jnp.int32, x.shape, 1)
    lo = (lane & (period - 1)) < half
    partner = jnp.where(lo, pltpu.roll(x, LANES - half, 1), pltpu.roll(x, half, 1))
    return x * c + partner * s


def _segment_specs(bounds):
    return [pl.BlockSpec((TM, D_MODEL), lambda i, lo=lo, hi=hi: (jnp.clip(i - lo, 0, hi - lo - 1), 0))
            for lo, hi in zip(bounds[:-1], bounds[1:])]


def _ffn_inproj_kernel(*refs, bounds):
    n_src = len(bounds) - 1
    srcs = refs[:n_src]
    (cda_ref, sda_ref, cm_ref, sm_ref, ga_ref, gb_ref,
     wg_ref, wu_ref, wd_ref, win_ref, gq_ref, gkv_ref, wuq_ref, wukv_ref,
     h1_ref, naq_ref, nak_ref, nav_ref, daq_ref, dak_ref, dav_ref,
     mq_ref, mk_ref, mv_ref) = refs[n_src:]
    i = pl.program_id(0)
    x = srcs[-1][...]
    for sgm in range(n_src - 2, -1, -1):
        x = jnp.where(i < bounds[sgm + 1], srcs[sgm][...], x)
    xn = _rms(x, ga_ref[...]).astype(jnp.bfloat16)
    h1 = x + _swiglu_half(xn, wg_ref, wu_ref, wd_ref)
    h1_ref[...] = h1
    xn2 = _rms(h1, gb_ref[...]).astype(jnp.bfloat16)

    u = jnp.dot(xn2, win_ref[:, 0:3 * NA_W], preferred_element_type=jnp.float32)
    naq_ref[0] = (u[:, 0:NA_W] * (NA_HD ** -0.5 * LOG2E)).T.astype(jnp.bfloat16)
    nak_ref[...] = u[:, NA_W:2 * NA_W].astype(jnp.bfloat16)
    vt = u[:, 2 * NA_W:3 * NA_W].T.astype(jnp.bfloat16)
    for c in range(TM // LANES):
        nav_ref[c] = vt[:, c * LANES:(c + 1) * LANES]

    o0 = 3 * NA_W
    u = jnp.dot(xn2, win_ref[:, o0:o0 + 2 * DA_QK_W + DA_W], preferred_element_type=jnp.float32)
    cda, sda = cda_ref[...], sda_ref[...]
    q = jnp.concatenate([_rope_chunk(u[:, c * LANES:(c + 1) * LANES], cda, sda, DA_ROT // 2, DA_HD)
                         for c in range(DA_QK_W // LANES)], axis=1)
    k = jnp.concatenate([_rope_chunk(u[:, DA_QK_W + c * LANES:DA_QK_W + (c + 1) * LANES], cda, sda, DA_ROT // 2, DA_HD)
                         for c in range(DA_QK_W // LANES)], axis=1)
    daq_ref[0] = (q * (DA_HD ** -0.5 * LOG2E)).T.astype(jnp.bfloat16)
    dak_ref[...] = k.astype(jnp.bfloat16)
    vt = u[:, 2 * DA_QK_W:2 * DA_QK_W + DA_W].T.astype(jnp.bfloat16)
    for c in range(TM // TK):
        dav_ref[c] = vt[:, c * TK:(c + 1) * TK]

    o1 = o0 + 2 * DA_QK_W + DA_W
    u = jnp.dot(xn2, win_ref[:, o1:IN_W_PAD], preferred_element_type=jnp.float32)
    cm, sm = cm_ref[...], sm_ref[...]
    lane = lax.broadcasted_iota(jnp.int32, cm.shape, 1)
    band = (lane >= MLA_NOPE) & (lane < MLA_NOPE + MLA_ROPE)
    cb, sb = jnp.where(band, cm, 1.0), jnp.where(band, sm, 0.0)
    cq = _rms(u[:, 0:Q_LORA], gq_ref[...]).astype(jnp.bfloat16)
    qm = jnp.dot(cq, wuq_ref[...], preferred_element_type=jnp.float32)
    qm = jnp.concatenate([_rope_chunk(qm[:, h * LANES:(h + 1) * LANES], cb, sb, MLA_ROPE // 2, MLA_ROPE)
                          for h in range(MLA_HEADS)], axis=1) * ((MLA_NOPE + MLA_ROPE) ** -0.5 * LOG2E)
    mq_ref[0] = qm.T.astype(jnp.bfloat16)
    ckv = _rms(u[:, Q_LORA:Q_LORA + KV_LORA], gkv_ref[...]).astype(jnp.bfloat16)
    kv = jnp.dot(ckv, wukv_ref[...], preferred_element_type=jnp.float32)
    kr = _rope_chunk(u[:, Q_LORA + KV_LORA:Q_LORA + KV_LORA + LANES], cm, sm, MLA_ROPE // 2, MLA_ROPE)
    kr = jnp.where(band, kr, 0.0)
    kf = jnp.concatenate([kv[:, h * LANES:(h + 1) * LANES] + kr for h in range(MLA_HEADS)], axis=1)
    mk_ref[...] = kf.astype(jnp.bfloat16)
    vt = kv[:, MLA_HEADS * LANES:MLA_HEADS * LANES + MLA_W].T.astype(jnp.bfloat16)
    for c in range(TM // TK):
        mv_ref[c] = vt[:, c * TK:(c + 1) * TK]


def _outproj_ffn_kernel(h_ref, ona_ref, oda_ref, om_ref, wo1_ref, wo2_ref, wo3_ref, g_ref,
                        wg_ref, wu_ref, wd_ref, gf_ref, *o_refs, out_bounds):
    h = h_ref[...]
    h = h + jnp.dot(ona_ref[...], wo1_ref[...], preferred_element_type=jnp.float32)
    h = h + jnp.dot(oda_ref[...], wo2_ref[...], preferred_element_type=jnp.float32)
    h = h + jnp.dot(om_ref[...], wo3_ref[...], preferred_element_type=jnp.float32)
    xn = _rms(h, g_ref[...]).astype(jnp.bfloat16)
    h = h + _swiglu_half(xn, wg_ref, wu_ref, wd_ref)
    if out_bounds is None:
        o_refs[0][...] = h
    else:
        h = _rms(h, gf_ref[...])
        i = pl.program_id(0)
        for o_ref, lo, hi in zip(o_refs, out_bounds[:-1], out_bounds[1:]):
            @pl.when((i >= lo) & (i < hi))
            def _(o_ref=o_ref):
                o_ref[...] = h


def _const_spec(shape):
    nd = len(shape)
    return pl.BlockSpec(shape, lambda i: (0,) * nd, pipeline_mode=pl.Buffered(1))


def _ffn_inproj(srcs, bounds, tabs, tab_block, ga, gb, wg, wu, wd, win, gq, gkv, wuq, wukv):
    nt = bounds[-1]
    rows = nt * TM
    row_spec = lambda w: pl.BlockSpec((TM, w), lambda i: (i, 0))
    tile_spec = lambda r: pl.BlockSpec((1, r, TM), lambda i: (i, 0, 0))
    bf = jnp.bfloat16
    out_shape = (
        jax.ShapeDtypeStruct((rows, D_MODEL), jnp.float32),
        jax.ShapeDtypeStruct((nt, NA_W, TM), bf),
        jax.ShapeDtypeStruct((rows, NA_W), bf),
        jax.ShapeDtypeStruct((rows // LANES, NA_W, LANES), bf),
        jax.ShapeDtypeStruct((nt, DA_QK_W, TM), bf),
        jax.ShapeDtypeStruct((rows, DA_QK_W), bf),
        jax.ShapeDtypeStruct((rows // TK, DA_W, TK), bf),
        jax.ShapeDtypeStruct((nt, MLA_HEADS * LANES, TM), bf),
        jax.ShapeDtypeStruct((rows, 2 * MXU_DIM), bf),
        jax.ShapeDtypeStruct((rows // TK, MLA_W, TK), bf),
    )
    out_specs = (
        row_spec(D_MODEL),
        tile_spec(NA_W),
        row_spec(NA_W),
        pl.BlockSpec((TM // LANES, NA_W, LANES), lambda i: (i, 0, 0)),
        tile_spec(DA_QK_W),
        row_spec(DA_QK_W),
        pl.BlockSpec((TM // TK, DA_W, TK), lambda i: (i, 0, 0)),
        tile_spec(MLA_HEADS * LANES),
        row_spec(2 * MXU_DIM),
        pl.BlockSpec((TM // TK, MLA_W, TK), lambda i: (i, 0, 0)),
    )
    in_specs = _segment_specs(bounds) + [pl.BlockSpec((TM, LANES), lambda i: (tab_block(i), 0))] * 4 + [
        _const_spec(a.shape) for a in (ga, gb, wg, wu, wd, win, gq, gkv, wuq, wukv)]
    return pl.pallas_call(
        functools.partial(_ffn_inproj_kernel, bounds=bounds),
        grid=(nt,),
        in_specs=in_specs,
        out_specs=out_specs,
        out_shape=out_shape,
        compiler_params=pltpu.CompilerParams(dimension_semantics=("arbitrary",),
                                             vmem_limit_bytes=VMEM_LIMIT_BYTES),
        name="ffn_inproj",
    )(*srcs, *tabs, ga, gb, wg, wu, wd, win, gq, gkv, wuq, wukv)


def _outproj_ffn(h, ona, oda, om, wo1, wo2, wo3, g, wg, wu, wd, gf, out_bounds):
    rows = h.shape[0]
    row_spec = lambda w: pl.BlockSpec((TM, w), lambda i: (i, 0))
    if out_bounds is None:
        out_specs = row_spec(D_MODEL)
        out_shape = jax.ShapeDtypeStruct((rows, D_MODEL), jnp.float32)
    else:
        out_specs = tuple(_segment_specs(out_bounds))
        out_shape = tuple(jax.ShapeDtypeStruct(((hi - lo) * TM, D_MODEL), jnp.float32)
                          for lo, hi in zip(out_bounds[:-1], out_bounds[1:]))
    in_specs = [row_spec(D_MODEL), row_spec(NA_W), row_spec(DA_W), row_spec(MLA_W)] + [
        _const_spec(a.shape) for a in (wo1, wo2, wo3, g, wg, wu, wd, gf)]
    return pl.pallas_call(
        functools.partial(_outproj_ffn_kernel, out_bounds=out_bounds),
        grid=(rows // TM,),
        in_specs=in_specs,
        out_specs=out_specs,
        out_shape=out_shape,
        compiler_params=pltpu.CompilerParams(dimension_semantics=("arbitrary",),
                                             vmem_limit_bytes=VMEM_LIMIT_BYTES),
        name="outproj_ffn",
    )(h, ona, oda, om, wo1, wo2, wo3, g, wg, wu, wd, gf)


def _row_band(block, lo, hi):
    row = lax.broadcasted_iota(jnp.int32, block.shape, 0)
    return jnp.where((row >= lo) & (row < hi), block, jnp.zeros_like(block))


def _with_ones(vt):
    return jnp.concatenate([vt, jnp.ones((ONES_ROWS, vt.shape[1]), vt.dtype)], axis=0)


def _meta_key_mask(b):
    row = lax.broadcasted_iota(jnp.int32, (LANES, 1), 0)
    lo = (b % (LANES // N_META)) * N_META
    return jnp.where((row >= lo) & (row < lo + N_META), 0.0, NEG).astype(jnp.float32)


class FlashCfg(NamedTuple):
    n_soft: int
    group_w: int
    soft_per_group: int
    v_of_soft: tuple
    out_w: int
    unroll: int


DA_CFG = FlashCfg(2 * DA_HEADS, LANES, 4, tuple(i // 2 for i in range(2 * DA_HEADS)), DA_W, 4)
MLA_CFG = FlashCfg(MLA_HEADS, LANES, 1, tuple(range(MLA_HEADS)), MLA_W, 8)


def _build_rhs(cfg, qt_ref, rhs_ref):
    for i in range(cfg.n_soft):
        g, j = i // cfg.soft_per_group, i % cfg.soft_per_group
        if cfg is DA_CFG:
            rhs_ref[i] = _row_band(qt_ref[0, g * LANES:(g + 1) * LANES, :], j * DA_HD, (j + 1) * DA_HD)
        else:
            rhs_ref[i] = qt_ref[0, i * LANES:(i + 1) * LANES, :]


def _flash_tiles(cfg, tiles, rhs_ref, m_ref, acc_ref, bounded):
    nt = len(tiles)
    if bounded:
        work = [(t, i) for i in range(cfg.n_soft) for t in range(nt)]
    else:
        work = [(t, i) for t in range(nt) for i in range(cfg.n_soft)]

    def score(w):
        kblk_of, _, mask = tiles[w[0]]
        s = jnp.dot(kblk_of(w[1] // cfg.soft_per_group), rhs_ref[w[1]], preferred_element_type=jnp.float32)
        return s if mask is None else s + mask

    pending = [score(w) for w in work[:LOOKAHEAD]]
    part = den = None
    for n, (t, i) in enumerate(work):
        vt = tiles[t][1](cfg.v_of_soft[i])
        s = pending.pop(0)
        if n + LOOKAHEAD < len(work):
            pending.append(score(work[n + LOOKAHEAD]))
        if bounded:
            p = jnp.exp2(s)
            pv = jnp.dot(vt, p.astype(jnp.bfloat16), preferred_element_type=jnp.float32)
            ps = jnp.sum(p, axis=0, keepdims=True)
            part, den = (pv, ps) if t == 0 else (part + pv, den + ps)
            if t == nt - 1:
                acc_ref[i, 0:DA_VD, :] = acc_ref[i, 0:DA_VD, :] + part
                acc_ref[i, DA_VD:DA_VD + 1, :] = acc_ref[i, DA_VD:DA_VD + 1, :] + den
        else:
            vt = _with_ones(vt)
            m_prev = m_ref[i]
            m_new = jnp.maximum(m_prev, jnp.max(s, axis=0, keepdims=True))
            alpha = jnp.exp2(m_prev - m_new)
            p = jnp.exp2(s - m_new).astype(jnp.bfloat16)
            acc_ref[i] = alpha * acc_ref[i] + jnp.dot(vt, p, preferred_element_type=jnp.float32)
            m_ref[i] = m_new


def _feature_indicator(cfg, g):
    f = lax.broadcasted_iota(jnp.int32, (cfg.group_w, LANES), 0)
    j = lax.broadcasted_iota(jnp.int32, (cfg.group_w, LANES), 1)
    del g
    if cfg is DA_CFG:
        hit = jnp.right_shift(f, int(math.log2(DA_HD))) == j
    else:
        hit = j == 0
    return jnp.where(hit & (j < cfg.soft_per_group), 1.0, 0.0).astype(jnp.bfloat16)


def _max_key_norms(cfg, kf_ref, kfm_ref, n_keys, kn_ref):
    gw = cfg.group_w
    n_groups = cfg.n_soft // cfg.soft_per_group
    inds = [_feature_indicator(cfg, g) for g in range(n_groups)]

    def sq_norms(blk, g):
        x = blk.astype(jnp.float32)
        n2 = jnp.dot((x * x).astype(jnp.bfloat16), inds[g], preferred_element_type=jnp.float32)
        return jnp.max(n2, axis=0, keepdims=True)

    def body(c, carry):
        r0 = pl.multiple_of(c * TM, TM)
        return tuple(jnp.maximum(carry[g], sq_norms(kf_ref[pl.ds(r0, TM), g * gw:(g + 1) * gw], g))
                     for g in range(n_groups))

    init = tuple(sq_norms(kfm_ref[:, g * gw:(g + 1) * gw], g) for g in range(n_groups))
    kmax = lax.fori_loop(0, n_keys // TM, body, init)
    lane = lax.broadcasted_iota(jnp.int32, (1, LANES), 1)
    for i in range(cfg.n_soft):
        g, j = divmod(i, cfg.soft_per_group)
        v = jnp.max(jnp.where(lane == j, kmax[g], 0.0), axis=1, keepdims=True)
        kn_ref[i] = jnp.broadcast_to(v, (1, LANES))


def _scores_are_bounded(cfg, rhs_ref, kn_ref):
    ok = None
    for i in range(cfg.n_soft):
        r = rhs_ref[i].astype(jnp.float32)
        bound2 = jnp.sum(r * r, axis=0, keepdims=True) * kn_ref[i][:, 0:1] * NORM_SLACK
        good = bound2 <= SCORE_BOUND * SCORE_BOUND
        ok = good if ok is None else ok & good
    return jnp.min(jnp.where(ok, 1.0, 0.0)) > 0.5


def _flash_kernel(*refs, cfg, n_keys, lam_init, meta_queries):
    if cfg is DA_CFG:
        (qt_ref, kf_ref, vt_ref, kfm_ref, vtm_ref, lam_ref, gs_ref, o_ref,
         rhs_ref, m_ref, acc_ref, kn_ref, *rest) = refs
    else:
        qt_ref, kf_ref, vt_ref, kfm_ref, vtm_ref, o_ref, rhs_ref, m_ref, acc_ref, kn_ref, *rest = refs
    b = pl.program_id(0)
    tq = qt_ref.shape[2]
    vd = DA_VD
    gw = cfg.group_w
    if meta_queries:
        _max_key_norms(cfg, kf_ref, kfm_ref, n_keys, kn_ref)
    else:
        @pl.when(pl.program_id(1) == 0)
        def _():
            _max_key_norms(cfg, kf_ref, kfm_ref, n_keys, kn_ref)
    _build_rhs(cfg, qt_ref, rhs_ref)
    m_ref[...] = jnp.full(m_ref.shape, NEG, jnp.float32)
    acc_ref[...] = jnp.zeros(acc_ref.shape, jnp.float32)

    def all_keys(bounded, unroll):
        _flash_tiles(
            cfg,
            [(lambda g: kfm_ref[:, g * gw:(g + 1) * gw],
              lambda h: vtm_ref[0, h * vd:(h + 1) * vd, :],
              _meta_key_mask(b))],
            rhs_ref, m_ref, acc_ref, bounded)

        def body(it, carry):
            tiles = []
            for u in range(unroll):
                kt = it * unroll + u
                k0 = pl.multiple_of(kt * TK, TK)
                tiles.append((
                    lambda g, k0=k0: kf_ref[pl.ds(k0, TK), g * gw:(g + 1) * gw],
                    lambda h, kt=kt: vt_ref[kt, h * vd:(h + 1) * vd, :],
                    None))
            _flash_tiles(cfg, tiles, rhs_ref, m_ref, acc_ref, bounded)
            return carry

        lax.fori_loop(0, n_keys // (TK * unroll), body, 0)

    lax.cond(_scores_are_bounded(cfg, rhs_ref, kn_ref),
             lambda: all_keys(True, math.gcd(cfg.unroll, n_keys // TK)),
             lambda: all_keys(False, 1))

    heads = []
    if cfg is DA_CFG:
        lp = lam_ref[...]
        lam = (jnp.exp(jnp.sum(lp[0:1] * lp[1:2], axis=1, keepdims=True))
               - jnp.exp(jnp.sum(lp[2:3] * lp[3:4], axis=1, keepdims=True)) + lam_init)
        for h in range(DA_HEADS):
            a1, a2 = acc_ref[2 * h], acc_ref[2 * h + 1]
            o = a1[0:vd] / a1[vd:vd + 1] - lam * (a2[0:vd] / a2[vd:vd + 1])
            o = o * lax.rsqrt(jnp.mean(o * o, axis=0, keepdims=True) + EPS) * gs_ref[...]
            heads.append(o * (1.0 - lam_init))
    else:
        for h in range(MLA_HEADS):
            a = acc_ref[h]
            heads.append(a[0:vd] / a[vd:vd + 1])
    pairs = [jnp.concatenate(heads[2 * g:2 * g + 2], axis=0).T for g in range(len(heads) // 2)]
    out = jnp.concatenate(pairs, axis=1)
    if meta_queries:
        stage_ref = rest[0]
        stage_ref[...] = out
        r0 = pl.multiple_of((b % (LANES // N_META)) * N_META, N_META)
        o_ref[...] = stage_ref[pl.ds(r0, N_META), :].astype(o_ref.dtype)
    else:
        o_ref[...] = out.astype(o_ref.dtype)


def _flash_call(cfg, grp, qt, kf, vt, extra, o_prev, lam_init, meta_queries, name):
    n, bsz = grp.n, grp.batch
    qw, kw, vw = qt.shape[1], kf.shape[1], vt.shape[1]
    tiles_per_seq = n // TM
    meta_blk0 = grp.meta0 // LANES
    per_blk = LANES // N_META
    if meta_queries:
        grid = (bsz,)
        tq = LANES
        q_map = lambda b: ((meta_blk0 + b // per_blk) // (TM // LANES), 0, (meta_blk0 + b // per_blk) % (TM // LANES))
        o_spec = pl.BlockSpec((N_META, cfg.out_w), lambda b: (grp.meta0 // N_META + b, 0))
        fix = lambda f: (lambda b: f(b))
        sem = ("arbitrary",)
    else:
        grid = (bsz, tiles_per_seq)
        tq = TM
        q_map = lambda b, i: (grp.real0 // TM + b * tiles_per_seq + i, 0, 0)
        o_spec = pl.BlockSpec((TM, cfg.out_w), lambda b, i: (grp.real0 // TM + b * tiles_per_seq + i, 0))
        fix = lambda f: (lambda b, i: f(b))
        sem = ("arbitrary", "arbitrary")
    in_specs = [
        pl.BlockSpec((1, qw, tq), q_map),
        pl.BlockSpec((n, kw), fix(lambda b: (grp.real0 // n + b, 0))),
        pl.BlockSpec((n // TK, vw, TK), fix(lambda b: (grp.real0 // n + b, 0, 0))),
        pl.BlockSpec((LANES, kw), fix(lambda b: (meta_blk0 + b // per_blk, 0))),
        pl.BlockSpec((1, vw, LANES), fix(lambda b: ((meta_blk0 + b // per_blk) // (TK // LANES), 0,
                                                     (meta_blk0 + b // per_blk) % (TK // LANES)))),
    ]
    args = [qt, kf, vt, kf, vt]
    for a in extra:
        in_specs.append(pl.BlockSpec(a.shape, fix(lambda b, nd=a.ndim: (0,) * nd)))
        args.append(a)
    in_specs.append(pl.BlockSpec(memory_space=pl.ANY))
    args.append(o_prev)
    scratch = [
        pltpu.VMEM((cfg.n_soft, cfg.group_w, tq), jnp.bfloat16),
        pltpu.VMEM((cfg.n_soft, 1, tq), jnp.float32),
        pltpu.VMEM((cfg.n_soft, DA_VD + ONES_ROWS, tq), jnp.float32),
        pltpu.VMEM((cfg.n_soft, 1, LANES), jnp.float32),
    ]
    if meta_queries:
        scratch.append(pltpu.VMEM((LANES, cfg.out_w), jnp.float32))

    def body(*refs):
        n_in = len(args)
        ins, rest = refs[:n_in - 1], refs[n_in:]
        _flash_kernel(*ins, *rest, cfg=cfg, n_keys=n, lam_init=lam_init, meta_queries=meta_queries)

    return pl.pallas_call(
        body,
        grid=grid,
        in_specs=in_specs,
        out_specs=o_spec,
        out_shape=jax.ShapeDtypeStruct(o_prev.shape, o_prev.dtype),
        scratch_shapes=scratch,
        input_output_aliases={len(args) - 1: 0},
        compiler_params=pltpu.CompilerParams(dimension_semantics=sem, vmem_limit_bytes=VMEM_LIMIT_BYTES),
        name=name,
    )(*args)


def _na_kernel(qt_ref, kf_ref, vt_ref, kfm_ref, vtm_ref, bias_ref, prev_ref, o_ref, *, grid_rows):
    del prev_ref
    b, i = pl.program_id(0), pl.program_id(1)
    ws = jnp.clip(i * NA_QROWS - WIN_R // 2, 0, grid_rows - NA_KROWS)
    kwin = kf_ref[pl.ds(pl.multiple_of(ws * GRID_W, 2 * GRID_W), NA_KROWS * GRID_W), :]
    vwin = vt_ref[pl.ds(ws // 2, NA_KROWS // 2)]
    kmeta = kfm_ref[...]
    mmask = _meta_key_mask(b)
    def scores(h):
        g, j = h // 2, h % 2
        rhs = _row_band(qt_ref[0, g * LANES:(g + 1) * LANES, :], j * NA_HD, (j + 1) * NA_HD)
        s1 = jnp.dot(kwin[:, g * LANES:(g + 1) * LANES], rhs, preferred_element_type=jnp.float32) + bias_ref[0, h]
        s2 = jnp.dot(kmeta[:, g * LANES:(g + 1) * LANES], rhs, preferred_element_type=jnp.float32) + mmask
        return s1, s2

    heads = []
    pending = [scores(h) for h in range(LOOKAHEAD)]
    for h in range(NA_HEADS):
        s1, s2 = pending.pop(0)
        if h + LOOKAHEAD < NA_HEADS:
            pending.append(scores(h + LOOKAHEAD))
        m = jnp.maximum(jnp.max(s1, axis=0, keepdims=True), jnp.max(s2, axis=0, keepdims=True))
        p1 = jnp.exp2(s1 - m).astype(jnp.bfloat16)
        p2 = jnp.exp2(s2 - m).astype(jnp.bfloat16)
        v1 = jnp.concatenate([vwin[c, h * NA_HD:(h + 1) * NA_HD, :] for c in range(NA_KROWS // 2)], axis=1)
        acc = (jnp.dot(_with_ones(v1), p1, preferred_element_type=jnp.float32)
               + jnp.dot(_with_ones(vtm_ref[0, h * NA_HD:(h + 1) * NA_HD, :]), p2,
                         preferred_element_type=jnp.float32))
        heads.append(acc[0:NA_HD] / acc[NA_HD:NA_HD + 1])
    pairs = [jnp.concatenate(heads[2 * g:2 * g + 2], axis=0).T for g in range(NA_HEADS // 2)]
    o_ref[...] = jnp.concatenate(pairs, axis=1).astype(o_ref.dtype)


def _na_meta_kernel(qt_ref, kf_ref, vt_ref, prev_ref, o_ref):
    del prev_ref
    kr = lax.broadcasted_iota(jnp.int32, (LANES, LANES), 0) // N_META
    qc = lax.broadcasted_iota(jnp.int32, (LANES, LANES), 1) // N_META
    mask = jnp.where(kr == qc, 0.0, NEG).astype(jnp.float32)
    kf = kf_ref[...]
    heads = []
    for h in range(NA_HEADS):
        g, j = h // 2, h % 2
        rhs = _row_band(qt_ref[0, g * LANES:(g + 1) * LANES, :], j * NA_HD, (j + 1) * NA_HD)
        s = jnp.dot(kf[:, g * LANES:(g + 1) * LANES], rhs, preferred_element_type=jnp.float32) + mask
        p = jnp.exp2(s - jnp.max(s, axis=0, keepdims=True)).astype(jnp.bfloat16)
        acc = jnp.dot(_with_ones(vt_ref[0, h * NA_HD:(h + 1) * NA_HD, :]), p, preferred_element_type=jnp.float32)
        heads.append(acc[0:NA_HD] / acc[NA_HD:NA_HD + 1])
    pairs = [jnp.concatenate(heads[2 * g:2 * g + 2], axis=0).T for g in range(NA_HEADS // 2)]
    o_ref[...] = jnp.concatenate(pairs, axis=1).astype(o_ref.dtype)


def _na_call(grp, qt, kf, vt, bias, o_prev):
    n, bsz = grp.n, grp.batch
    grid_rows = n // GRID_W
    steps = grid_rows // NA_QROWS
    tq = NA_QROWS * GRID_W
    meta_blk0 = grp.meta0 // LANES
    per_blk = LANES // N_META
    in_specs = [
        pl.BlockSpec((1, NA_W, tq), lambda b, i: (grp.real0 // TM + b * (n // TM) + i // (TM // tq), 0, i % (TM // tq))),
        pl.BlockSpec((n, NA_W), lambda b, i: (grp.real0 // n + b, 0)),
        pl.BlockSpec((n // LANES, NA_W, LANES), lambda b, i: (grp.real0 // n + b, 0, 0)),
        pl.BlockSpec((LANES, NA_W), lambda b, i: (meta_blk0 + b // per_blk, 0)),
        pl.BlockSpec((1, NA_W, LANES), lambda b, i: (meta_blk0 + b // per_blk, 0, 0)),
        pl.BlockSpec((1, NA_HEADS, NA_KROWS * GRID_W, tq),
                     lambda b, i: (jnp.where(i == 0, 0, jnp.where(i == steps - 1, 2, 1)), 0, 0, 0)),
        pl.BlockSpec(memory_space=pl.ANY),
    ]
    return pl.pallas_call(
        functools.partial(_na_kernel, grid_rows=grid_rows),
        grid=(bsz, steps),
        in_specs=in_specs,
        out_specs=pl.BlockSpec((tq, NA_W), lambda b, i: (grp.real0 // tq + b * steps + i, 0)),
        out_shape=jax.ShapeDtypeStruct(o_prev.shape, o_prev.dtype),
        input_output_aliases={6: 0},
        compiler_params=pltpu.CompilerParams(dimension_semantics=("arbitrary", "arbitrary"),
                                             vmem_limit_bytes=VMEM_LIMIT_BYTES),
        name="na_real",
    )(qt, kf, vt, kf, vt, bias, o_prev)


def _na_meta_call(layout, qt, kf, vt, o_prev):
    blk0 = layout.groups[0].meta0 // LANES
    nblk = sum(_round_up(g.batch * N_META, LANES) for g in layout.groups) // LANES
    sub = TM // LANES
    return pl.pallas_call(
        _na_meta_kernel,
        grid=(nblk,),
        in_specs=[
            pl.BlockSpec((1, NA_W, LANES), lambda i: ((blk0 + i) // sub, 0, (blk0 + i) % sub)),
            pl.BlockSpec((LANES, NA_W), lambda i: (blk0 + i, 0)),
            pl.BlockSpec((1, NA_W, LANES), lambda i: (blk0 + i, 0, 0)),
            pl.BlockSpec(memory_space=pl.ANY),
        ],
        out_specs=pl.BlockSpec((LANES, NA_W), lambda i: (blk0 + i, 0)),
        out_shape=jax.ShapeDtypeStruct(o_prev.shape, o_prev.dtype),
        input_output_aliases={3: 0},
        compiler_params=pltpu.CompilerParams(dimension_semantics=("arbitrary",),
                                             vmem_limit_bytes=VMEM_LIMIT_BYTES),
        name="na_meta",
    )(qt, kf, vt, o_prev)


def _na_bias_tables(rel_bias):
    kc = np.arange(GRID_W)[:, None]
    qc = np.arange(GRID_W)[None, :]
    cs = np.clip(qc - WIN_C // 2, 0, GRID_W - WIN_C)
    col_ok = (kc >= cs) & (kc < cs + WIN_C)
    cidx = np.clip(kc - qc + WIN_C - 1, 0, 2 * WIN_C - 2)
    planes = jnp.where(col_ok[None, None], rel_bias[:, :, cidx] * LOG2E, NEG)
    masked = 2 * WIN_R - 1
    planes = jnp.concatenate([planes, jnp.full((NA_HEADS, 1, GRID_W, GRID_W), NEG, planes.dtype)], axis=1)
    kr = np.arange(NA_KROWS)[:, None]
    qr = np.arange(NA_QROWS)[None, :]
    ridx = []
    for win_lo, q_off in ((0 * qr, qr), (qr, qr + WIN_R // 2), (0 * qr + NA_KROWS - WIN_R, qr + NA_KROWS - NA_QROWS)):
        row_ok = (kr >= win_lo) & (kr < win_lo + WIN_R)
        ridx.append(np.where(row_ok, kr - q_off + WIN_R - 1, masked))
    ridx = np.stack(ridx)
    assert ridx.min() >= 0 and ridx.max() <= masked
    t = planes[:, ridx]
    t = jnp.transpose(t, (1, 0, 2, 4, 3, 5))
    return t.reshape(3, NA_HEADS, NA_KROWS * GRID_W, NA_QROWS * GRID_W).astype(jnp.float32)


def _rope_tables(layout):
    pos, starts, periods = [], [], []
    for g in layout.groups:
        starts.append(len(pos))
        periods.append(g.n // TM)
        for t in range(g.n // TM):
            pos.append(N_META + t * TM + np.arange(TM))
    tail0 = sum(g.batch * g.n for g in layout.groups)
    starts.append(len(pos))
    for t in range(tail0 // TM, layout.rows // TM):
        r = t * TM + np.arange(TM)
        p = np.zeros((TM,), np.int64)
        for g in layout.groups:
            inside = (r >= g.meta0) & (r < g.meta0 + g.batch * N_META)
            p = np.where(inside, (r - g.meta0) % N_META, p)
        pos.append(p)
    pos = jnp.asarray(np.concatenate(pos).astype(np.float32))
    bounds = [g.real0 // TM for g in layout.groups] + [tail0 // TM]

    def block_of_tile(i):
        blk = starts[-1] + (i - bounds[-1])
        for k in range(len(layout.groups) - 1, -1, -1):
            blk = jnp.where(i < bounds[k + 1], starts[k] + (i - bounds[k]) % periods[k], blk)
        return blk

    def tables(dim, period):
        inv = ROPE_THETA ** (-(jnp.arange(0, dim, 2, dtype=jnp.float32) / dim))
        ang = pos[:, None] * inv[None, :]
        cos, sin = jnp.cos(ang), jnp.sin(ang)
        rest = period - dim
        c = jnp.concatenate([cos, cos, jnp.ones((pos.shape[0], rest), jnp.float32)], axis=1)
        s = jnp.concatenate([-sin, sin, jnp.zeros((pos.shape[0], rest), jnp.float32)], axis=1)
        reps = LANES // period
        return jnp.tile(c, (1, reps)), jnp.tile(s, (1, reps))

    cda, sda = tables(DA_ROT, DA_HD)
    cm, sm = tables(MLA_ROPE, MLA_ROPE)
    return (cda, sda, cm, sm), block_of_tile


def _prep_layer(l, norm_g, ffn_w_gate, ffn_w_up, ffn_w_down, w_in, w_out, mla_q_norm_g, mla_kv_norm_g,
                mla_w_uq, mla_w_ukv, da_subln_g):
    bf = jnp.bfloat16
    kr0 = IN_W - MLA_ROPE
    win = jnp.concatenate([w_in[l, :, :kr0]] + [w_in[l, :, kr0:]] * (LANES // MLA_ROPE), axis=1).astype(bf)
    uq = mla_w_uq[l].reshape(Q_LORA, MLA_HEADS, MLA_NOPE + MLA_ROPE)
    uq = jnp.pad(uq, ((0, 0), (0, 0), (0, LANES - MLA_NOPE - MLA_ROPE))).reshape(Q_LORA, MLA_HEADS * LANES)
    ukv = mla_w_ukv[l].reshape(KV_LORA, MLA_HEADS, MLA_NOPE + MLA_VD)
    uk = jnp.pad(ukv[:, :, :MLA_NOPE], ((0, 0), (0, 0), (0, LANES - MLA_NOPE))).reshape(KV_LORA, MLA_HEADS * LANES)
    ukv = jnp.concatenate([uk, ukv[:, :, MLA_NOPE:].reshape(KV_LORA, -1)], axis=1)
    row = lambda v: v.reshape(1, -1)
    return dict(
        g=[row(norm_g[l, i]) for i in range(3)],
        ffn=[(ffn_w_gate[l, i].astype(bf), ffn_w_up[l, i].astype(bf), ffn_w_down[l, i].astype(bf)) for i in range(2)],
        win=win, uq=uq.astype(bf), ukv=ukv.astype(bf),
        gq=row(mla_q_norm_g[l]), gkv=row(mla_kv_norm_g[l]),
        wo=(w_out[l, :NA_W].astype(bf), w_out[l, NA_W:NA_W + DA_W].astype(bf), w_out[l, NA_W + DA_W:].astype(bf)),
        gs=da_subln_g[l].reshape(DA_VD, 1),
    )


def kernel(x_prompt, x_sample, meta_tokens, norm_g, final_norm_g, ffn_w_gate, ffn_w_up, ffn_w_down, w_in, w_out, na_rel_bias, da_lambda, da_subln_g, mla_q_norm_g, mla_kv_norm_g, mla_w_uq, mla_w_ukv):
    xs = (x_prompt, x_sample)
    layout = _make_layout([(x.shape[0], x.shape[1]) for x in xs])
    real_rows = sum(g.batch * g.n for g in layout.groups)
    tail = []
    for g in layout.groups:
        blk = jnp.tile(meta_tokens.astype(jnp.float32), (g.batch, 1))
        tail.append(jnp.pad(blk, ((0, _round_up(g.batch * N_META, LANES) - g.batch * N_META), (0, 0))))
    tail = jnp.concatenate(tail, axis=0)
    tail = jnp.pad(tail, ((0, layout.rows - real_rows - tail.shape[0]), (0, 0)))
    srcs = [x.reshape(-1, D_MODEL) for x in xs] + [tail]
    real_bounds = tuple(g.real0 // TM for g in layout.groups) + (real_rows // TM,)
    bounds = real_bounds + (layout.rows // TM,)
    tabs, tab_block = _rope_tables(layout)
    gf = final_norm_g.reshape(1, -1)

    o_na = jnp.zeros((layout.rows, NA_W), jnp.bfloat16)
    o_da = jnp.zeros((layout.rows, DA_W), jnp.bfloat16)
    o_m = jnp.zeros((layout.rows, MLA_W), jnp.bfloat16)
    for l in range(DEPTH):
        p = _prep_layer(l, norm_g, ffn_w_gate, ffn_w_up, ffn_w_down, w_in, w_out, mla_q_norm_g, mla_kv_norm_g,
                        mla_w_uq, mla_w_ukv, da_subln_g)
        lam_init = 0.8 - 0.6 * math.exp(-0.3 * l)
        (h, naq, nak, nav, daq, dak, dav, mq, mk, mv) = _ffn_inproj(
            srcs, bounds, tabs, tab_block, p["g"][0], p["g"][1], *p["ffn"][0], p["win"], p["gq"], p["gkv"], p["uq"], p["ukv"])
        bias = _na_bias_tables(na_rel_bias[l])
        o_na = _na_meta_call(layout, naq, nak, nav, o_na)
        for gi, grp in enumerate(layout.groups):
            o_na = _na_call(grp, naq, nak, nav, bias, o_na)
            for meta_q in (False, True):
                tag = f"g{gi}_{'meta' if meta_q else 'real'}"
                o_da = _flash_call(DA_CFG, grp, daq, dak, dav, (da_lambda[l], p["gs"]), o_da, lam_init, meta_q,
                                   "da_" + tag)
                o_m = _flash_call(MLA_CFG, grp, mq, mk, mv, (), o_m, lam_init, meta_q, "mla_" + tag)
        last = l == DEPTH - 1
        h = _outproj_ffn(h, o_na, o_da, o_m, *p["wo"], p["g"][2], *p["ffn"][1], gf, real_bounds if last else None)
        srcs, bounds = [h], (0, layout.rows // TM)

    return tuple(y.reshape(x.shape) for x, y in zip(xs, h))
```

```python
import functools
import math
from typing import NamedTuple

import jax
import jax.numpy as jnp
import numpy as np
from jax import lax
from jax.experimental import pallas as pl
from jax.experimental.pallas import tpu as pltpu

D_MODEL = 1024
DEPTH = 2
GRID_W = 64
N_META = 16
WIN_R = 8
WIN_C = 16
NA_HEADS = 6
NA_HD = 64
DA_HEADS = 6
DA_HD = 32
DA_VD = 64
DA_ROT = DA_HD // 4
MLA_HEADS = 4
MLA_NOPE = 64
MLA_ROPE = 32
MLA_VD = 64
Q_LORA = 256
KV_LORA = 128
ROPE_THETA = 500000.0
D_FF = 2816
EPS = 1e-6
NA_W = NA_HEADS * NA_HD
DA_W = DA_HEADS * DA_VD
MLA_W = MLA_HEADS * MLA_VD
DA_QK_W = DA_HEADS * 2 * DA_HD
IN_W = 3 * NA_W + 2 * DA_QK_W + DA_W + Q_LORA + KV_LORA + MLA_ROPE

LANES = 128
MXU_DIM = 256
VMEM_LIMIT_BYTES = 58 * 1024 * 1024

TM = 512
TK = 256
NA_QROWS = 4
NA_KROWS = NA_QROWS + WIN_R
LOOKAHEAD = 2
FF_SPLIT = 1536
IN_W_PAD = 2816

LOG2E = 1.4426950408889634
NEG = -1e30
ONES_ROWS = 16
SCORE_BOUND = 60.0
NORM_SLACK = 1.05


def _round_up(x, m):
    return (x + m - 1) // m * m


class Group(NamedTuple):
    batch: int
    n: int
    real0: int
    meta0: int


class Layout(NamedTuple):
    groups: tuple
    rows: int


def _make_layout(shapes):
    row = 0
    real0 = []
    for b, n in shapes:
        assert n % TM == 0 and row % n == 0 and n % GRID_W == 0
        assert (n // GRID_W) % NA_QROWS == 0 and n // GRID_W >= NA_KROWS
        real0.append(row)
        row += b * n
    groups = []
    for (b, n), r0 in zip(shapes, real0):
        groups.append(Group(b, n, r0, row))
        row += _round_up(b * N_META, LANES)
    return Layout(tuple(groups), _round_up(row, TM))


def _rms(x, g):
    return x * lax.rsqrt(jnp.mean(x * x, axis=-1, keepdims=True) + EPS) * g


def _swiglu_half(xn, wg_ref, wu_ref, wd_ref):
    acc = None
    for lo, hi in ((0, FF_SPLIT), (FF_SPLIT, D_FF)):
        gate = jnp.dot(xn, wg_ref[:, lo:hi], preferred_element_type=jnp.float32)
        up = jnp.dot(xn, wu_ref[:, lo:hi], preferred_element_type=jnp.float32)
        hm = (gate * jax.nn.sigmoid(gate) * up).astype(jnp.bfloat16)
        part = jnp.dot(hm, wd_ref[lo:hi, :], preferred_element_type=jnp.float32)
        acc = part if acc is None else acc + part
    return 0.5 * acc


def _rope_chunk(x, c, s, half, period):
    lane = lax.broadcasted_iota(jnp.int32, x.shape, 1)
    lo = (lane & (period - 1)) < half
    partner = jnp.where(lo, pltpu.roll(x, LANES - half, 1), pltpu.roll(x, half, 1))
    return x * c + partner * s


def _segment_specs(bounds):
    return [pl.BlockSpec((TM, D_MODEL), lambda i, lo=lo, hi=hi: (jnp.clip(i - lo, 0, hi - lo - 1), 0))
            for lo, hi in zip(bounds[:-1], bounds[1:])]


def _ffn_inproj_kernel(*refs, bounds):
    n_src = len(bounds) - 1
    srcs = refs[:n_src]
    (cda_ref, sda_ref, cm_ref, sm_ref, ga_ref, gb_ref,
     wg_ref, wu_ref, wd_ref, win_ref, gq_ref, gkv_ref, wuq_ref, wukv_ref,
     h1_ref, naq_ref, nak_ref, nav_ref, daq_ref, dak_ref, dav_ref,
     mq_ref, mk_ref, mv_ref) = refs[n_src:]
    i = pl.program_id(0)
    x = srcs[-1][...]
    for sgm in range(n_src - 2, -1, -1):
        x = jnp.where(i < bounds[sgm + 1], srcs[sgm][...], x)
    xn = _rms(x, ga_ref[...]).astype(jnp.bfloat16)
    h1 = x + _swiglu_half(xn, wg_ref, wu_ref, wd_ref)
    h1_ref[...] = h1
    xn2 = _rms(h1, gb_ref[...]).astype(jnp.bfloat16)

    u = jnp.dot(xn2, win_ref[:, 0:3 * NA_W], preferred_element_type=jnp.float32)
    naq_ref[0] = (u[:, 0:NA_W] * (NA_HD ** -0.5 * LOG2E)).T.astype(jnp.bfloat16)
    nak_ref[...] = u[:, NA_W:2 * NA_W].astype(jnp.bfloat16)
    vt = u[:, 2 * NA_W:3 * NA_W].T.astype(jnp.bfloat16)
    for c in range(TM // LANES):
        nav_ref[c] = vt[:, c * LANES:(c + 1) * LANES]

    o0 = 3 * NA_W
    u = jnp.dot(xn2, win_ref[:, o0:o0 + 2 * DA_QK_W + DA_W], preferred_element_type=jnp.float32)
    cda, sda = cda_ref[...], sda_ref[...]
    q = jnp.concatenate([_rope_chunk(u[:, c * LANES:(c + 1) * LANES], cda, sda, DA_ROT // 2, DA_HD)
                         for c in range(DA_QK_W // LANES)], axis=1)
    k = jnp.concatenate([_rope_chunk(u[:, DA_QK_W + c * LANES:DA_QK_W + (c + 1) * LANES], cda, sda, DA_ROT // 2, DA_HD)
                         for c in range(DA_QK_W // LANES)], axis=1)
    daq_ref[0] = (q * (DA_HD ** -0.5 * LOG2E)).T.astype(jnp.bfloat16)
    dak_ref[...] = k.astype(jnp.bfloat16)
    vt = u[:, 2 * DA_QK_W:2 * DA_QK_W + DA_W].T.astype(jnp.bfloat16)
    for c in range(TM // TK):
        dav_ref[c] = vt[:, c * TK:(c + 1) * TK]

    o1 = o0 + 2 * DA_QK_W + DA_W
    u = jnp.dot(xn2, win_ref[:, o1:IN_W_PAD], preferred_element_type=jnp.float32)
    cm, sm = cm_ref[...], sm_ref[...]
    lane = lax.broadcasted_iota(jnp.int32, cm.shape, 1)
    band = (lane >= MLA_NOPE) & (lane < MLA_NOPE + MLA_ROPE)
    cb, sb = jnp.where(band, cm, 1.0), jnp.where(band, sm, 0.0)
    cq = _rms(u[:, 0:Q_LORA], gq_ref[...]).astype(jnp.bfloat16)
    qm = jnp.dot(cq, wuq_ref[...], preferred_element_type=jnp.float32)
    qm = jnp.concatenate([_rope_chunk(qm[:, h * LANES:(h + 1) * LANES], cb, sb, MLA_ROPE // 2, MLA_ROPE)
                          for h in range(MLA_HEADS)], axis=1) * ((MLA_NOPE + MLA_ROPE) ** -0.5 * LOG2E)
    mq_ref[0] = qm.T.astype(jnp.bfloat16)
    ckv = _rms(u[:, Q_LORA:Q_LORA + KV_LORA], gkv_ref[...]).astype(jnp.bfloat16)
    kv = jnp.dot(ckv, wukv_ref[...], preferred_element_type=jnp.float32)
    kr = _rope_chunk(u[:, Q_LORA + KV_LORA:Q_LORA + KV_LORA + LANES], cm, sm, MLA_ROPE // 2, MLA_ROPE)
    kr = jnp.where(band, kr, 0.0)
    kf = jnp.concatenate([kv[:, h * LANES:(h + 1) * LANES] + kr for h in range(MLA_HEADS)], axis=1)
    mk_ref[...] = kf.astype(jnp.bfloat16)
    vt = kv[:, MLA_HEADS * LANES:MLA_HEADS * LANES + MLA_W].T.astype(jnp.bfloat16)
    for c in range(TM // TK):
        mv_ref[c] = vt[:, c * TK:(c + 1) * TK]


def _outproj_ffn_kernel(h_ref, ona_ref, oda_ref, om_ref, wo1_ref, wo2_ref, wo3_ref, g_ref,
                        wg_ref, wu_ref, wd_ref, gf_ref, *o_refs, out_bounds):
    h = h_ref[...]
    h = h + jnp.dot(ona_ref[...], wo1_ref[...], preferred_element_type=jnp.float32)
    h = h + jnp.dot(oda_ref[...], wo2_ref[...], preferred_element_type=jnp.float32)
    h = h + jnp.dot(om_ref[...], wo3_ref[...], preferred_element_type=jnp.float32)
    xn = _rms(h, g_ref[...]).astype(jnp.bfloat16)
    h = h + _swiglu_half(xn, wg_ref, wu_ref, wd_ref)
    if out_bounds is None:
        o_refs[0][...] = h
    else:
        h = _rms(h, gf_ref[...])
        i = pl.program_id(0)
        for o_ref, lo, hi in zip(o_refs, out_bounds[:-1], out_bounds[1:]):
            @pl.when((i >= lo) & (i < hi))
            def _(o_ref=o_ref):
                o_ref[...] = h


def _const_spec(shape):
    nd = len(shape)
    return pl.BlockSpec(shape, lambda i: (0,) * nd, pipeline_mode=pl.Buffered(1))


def _ffn_inproj(srcs, bounds, tabs, tab_block, ga, gb, wg, wu, wd, win, gq, gkv, wuq, wukv):
    nt = bounds[-1]
    rows = nt * TM
    row_spec = lambda w: pl.BlockSpec((TM, w), lambda i: (i, 0))
    tile_spec = lambda r: pl.BlockSpec((1, r, TM), lambda i: (i, 0, 0))
    bf = jnp.bfloat16
    out_shape = (
        jax.ShapeDtypeStruct((rows, D_MODEL), jnp.float32),
        jax.ShapeDtypeStruct((nt, NA_W, TM), bf),
        jax.ShapeDtypeStruct((rows, NA_W), bf),
        jax.ShapeDtypeStruct((rows // LANES, NA_W, LANES), bf),
        jax.ShapeDtypeStruct((nt, DA_QK_W, TM), bf),
        jax.ShapeDtypeStruct((rows, DA_QK_W), bf),
        jax.ShapeDtypeStruct((rows // TK, DA_W, TK), bf),
        jax.ShapeDtypeStruct((nt, MLA_HEADS * LANES, TM), bf),
        jax.ShapeDtypeStruct((rows, 2 * MXU_DIM), bf),
        jax.ShapeDtypeStruct((rows // TK, MLA_W, TK), bf),
    )
    out_specs = (
        row_spec(D_MODEL),
        tile_spec(NA_W),
        row_spec(NA_W),
        pl.BlockSpec((TM // LANES, NA_W, LANES), lambda i: (i, 0, 0)),
        tile_spec(DA_QK_W),
        row_spec(DA_QK_W),
        pl.BlockSpec((TM // TK, DA_W, TK), lambda i: (i, 0, 0)),
        tile_spec(MLA_HEADS * LANES),
        row_spec(2 * MXU_DIM),
        pl.BlockSpec((TM // TK, MLA_W, TK), lambda i: (i, 0, 0)),
    )
    in_specs = _segment_specs(bounds) + [pl.BlockSpec((TM, LANES), lambda i: (tab_block(i), 0))] * 4 + [
        _const_spec(a.shape) for a in (ga, gb, wg, wu, wd, win, gq, gkv, wuq, wukv)]
    return pl.pallas_call(
        functools.partial(_ffn_inproj_kernel, bounds=bounds),
        grid=(nt,),
        in_specs=in_specs,
        out_specs=out_specs,
        out_shape=out_shape,
        compiler_params=pltpu.CompilerParams(dimension_semantics=("arbitrary",),
                                             vmem_limit_bytes=VMEM_LIMIT_BYTES),
        name="ffn_inproj",
    )(*srcs, *tabs, ga, gb, wg, wu, wd, win, gq, gkv, wuq, wukv)


def _outproj_ffn(h, ona, oda, om, wo1, wo2, wo3, g, wg, wu, wd, gf, out_bounds):
    rows = h.shape[0]
    row_spec = lambda w: pl.BlockSpec((TM, w), lambda i: (i, 0))
    if out_bounds is None:
        out_specs = row_spec(D_MODEL)
        out_shape = jax.ShapeDtypeStruct((rows, D_MODEL), jnp.float32)
    else:
        out_specs = tuple(_segment_specs(out_bounds))
        out_shape = tuple(jax.ShapeDtypeStruct(((hi - lo) * TM, D_MODEL), jnp.float32)
                          for lo, hi in zip(out_bounds[:-1], out_bounds[1:]))
    in_specs = [row_spec(D_MODEL), row_spec(NA_W), row_spec(DA_W), row_spec(MLA_W)] + [
        _const_spec(a.shape) for a in (wo1, wo2, wo3, g, wg, wu, wd, gf)]
    return pl.pallas_call(
        functools.partial(_outproj_ffn_kernel, out_bounds=out_bounds),
        grid=(rows // TM,),
        in_specs=in_specs,
        out_specs=out_specs,
        out_shape=out_shape,
        compiler_params=pltpu.CompilerParams(dimension_semantics=("arbitrary",),
                                             vmem_limit_bytes=VMEM_LIMIT_BYTES),
        name="outproj_ffn",
    )(h, ona, oda, om, wo1, wo2, wo3, g, wg, wu, wd, gf)


def _row_band(block, lo, hi):
    row = lax.broadcasted_iota(jnp.int32, block.shape, 0)
    return jnp.where((row >= lo) & (row < hi), block, jnp.zeros_like(block))


def _with_ones(vt):
    return jnp.concatenate([vt, jnp.ones((ONES_ROWS, vt.shape[1]), vt.dtype)], axis=0)


def _meta_key_mask(b):
    row = lax.broadcasted_iota(jnp.int32, (LANES, 1), 0)
    lo = (b % (LANES // N_META)) * N_META
    return jnp.where((row >= lo) & (row < lo + N_META), 0.0, NEG).astype(jnp.float32)


class FlashCfg(NamedTuple):
    n_soft: int
    group_w: int
    soft_per_group: int
    v_of_soft: tuple
    out_w: int
    unroll: int


DA_CFG = FlashCfg(2 * DA_HEADS, LANES, 4, tuple(i // 2 for i in range(2 * DA_HEADS)), DA_W, 8)
MLA_CFG = FlashCfg(MLA_HEADS, LANES, 1, tuple(range(MLA_HEADS)), MLA_W, 16)
NA_CFG = FlashCfg(NA_HEADS, LANES, 2, tuple(range(NA_HEADS)), NA_W, 1)


def _build_rhs(cfg, qt_ref, rhs_ref):
    for i in range(cfg.n_soft):
        g, j = i // cfg.soft_per_group, i % cfg.soft_per_group
        if cfg is DA_CFG:
            rhs_ref[i] = _row_band(qt_ref[0, g * LANES:(g + 1) * LANES, :], j * DA_HD, (j + 1) * DA_HD)
        else:
            rhs_ref[i] = qt_ref[0, i * LANES:(i + 1) * LANES, :]


def _flash_tiles(cfg, tiles, rhs_ref, m_ref, acc_ref, bounded):
    nt = len(tiles)
    if bounded:
        work = [(t, i) for i in range(cfg.n_soft) for t in range(nt)]
    else:
        work = [(t, i) for t in range(nt) for i in range(cfg.n_soft)]

    def score(w):
        kblk_of, _, mask = tiles[w[0]]
        s = jnp.dot(kblk_of(w[1] // cfg.soft_per_group), rhs_ref[w[1]], preferred_element_type=jnp.float32)
        return s if mask is None else s + mask

    pending = [score(w) for w in work[:LOOKAHEAD]]
    part = den = None
    for n, (t, i) in enumerate(work):
        vt = tiles[t][1](cfg.v_of_soft[i])
        s = pending.pop(0)
        if n + LOOKAHEAD < len(work):
            pending.append(score(work[n + LOOKAHEAD]))
        if bounded:
            p = jnp.exp2(s)
            pv = jnp.dot(vt, p.astype(jnp.bfloat16), preferred_element_type=jnp.float32)
            ps = jnp.sum(p, axis=0, keepdims=True)
            part, den = (pv, ps) if t == 0 else (part + pv, den + ps)
            if t == nt - 1:
                acc_ref[i, 0:DA_VD, :] = acc_ref[i, 0:DA_VD, :] + part
                acc_ref[i, DA_VD:DA_VD + 1, :] = acc_ref[i, DA_VD:DA_VD + 1, :] + den
        else:
            vt = _with_ones(vt)
            m_prev = m_ref[i]
            m_new = jnp.maximum(m_prev, jnp.max(s, axis=0, keepdims=True))
            alpha = jnp.exp2(m_prev - m_new)
            p = jnp.exp2(s - m_new).astype(jnp.bfloat16)
            acc_ref[i] = alpha * acc_ref[i] + jnp.dot(vt, p, preferred_element_type=jnp.float32)
            m_ref[i] = m_new


def _feature_indicator(cfg):
    f = lax.broadcasted_iota(jnp.int32, (cfg.group_w, LANES), 0)
    j = lax.broadcasted_iota(jnp.int32, (cfg.group_w, LANES), 1)
    band = cfg.group_w // cfg.soft_per_group
    hit = jnp.right_shift(f, int(math.log2(band))) == j
    return jnp.where(hit & (j < cfg.soft_per_group), 1.0, 0.0).astype(jnp.bfloat16)


def _max_key_norms(cfg, kf_ref, kfm_ref, n_keys, kn_ref):
    gw = cfg.group_w
    n_groups = cfg.n_soft // cfg.soft_per_group
    ind = _feature_indicator(cfg)

    def sq_norms(blk):
        x = blk.astype(jnp.float32)
        n2 = jnp.dot((x * x).astype(jnp.bfloat16), ind, preferred_element_type=jnp.float32)
        return jnp.max(n2, axis=0, keepdims=True)

    def body(c, carry):
        r0 = pl.multiple_of(c * TM, TM)
        return tuple(jnp.maximum(carry[g], sq_norms(kf_ref[pl.ds(r0, TM), g * gw:(g + 1) * gw]))
                     for g in range(n_groups))

    init = tuple(sq_norms(kfm_ref[:, g * gw:(g + 1) * gw]) for g in range(n_groups))
    kmax = lax.fori_loop(0, n_keys // TM, body, init)
    lane = lax.broadcasted_iota(jnp.int32, (1, LANES), 1)
    for i in range(cfg.n_soft):
        g, j = divmod(i, cfg.soft_per_group)
        v = jnp.max(jnp.where(lane == j, kmax[g], 0.0), axis=1, keepdims=True)
        kn_ref[i] = jnp.broadcast_to(v, (1, LANES))


def _scores_are_bounded(cfg, rhs_ref, kn_ref):
    ok = None
    for i in range(cfg.n_soft):
        r = rhs_ref[i].astype(jnp.float32)
        bound2 = jnp.sum(r * r, axis=0, keepdims=True) * kn_ref[i][:, 0:1] * NORM_SLACK
        good = bound2 <= SCORE_BOUND * SCORE_BOUND
        ok = good if ok is None else ok & good
    return jnp.min(jnp.where(ok, 1.0, 0.0)) > 0.5


def _flash_kernel(*refs, cfg, n_keys, lam_init, meta_queries):
    if cfg is DA_CFG:
        (qt_ref, kf_ref, vt_ref, kfm_ref, vtm_ref, lam_ref, gs_ref, o_ref,
         rhs_ref, m_ref, acc_ref, kn_ref, *rest) = refs
    else:
        qt_ref, kf_ref, vt_ref, kfm_ref, vtm_ref, o_ref, rhs_ref, m_ref, acc_ref, kn_ref, *rest = refs
    b = pl.program_id(0)
    tq = qt_ref.shape[2]
    vd = DA_VD
    gw = cfg.group_w
    if meta_queries:
        _max_key_norms(cfg, kf_ref, kfm_ref, n_keys, kn_ref)
    else:
        @pl.when(pl.program_id(1) == 0)
        def _():
            _max_key_norms(cfg, kf_ref, kfm_ref, n_keys, kn_ref)
    _build_rhs(cfg, qt_ref, rhs_ref)
    m_ref[...] = jnp.full(m_ref.shape, NEG, jnp.float32)
    acc_ref[...] = jnp.zeros(acc_ref.shape, jnp.float32)

    def all_keys(bounded, unroll):
        _flash_tiles(
            cfg,
            [(lambda g: kfm_ref[:, g * gw:(g + 1) * gw],
              lambda h: vtm_ref[0, h * vd:(h + 1) * vd, :],
              _meta_key_mask(b))],
            rhs_ref, m_ref, acc_ref, bounded)

        def body(it, carry):
            tiles = []
            for u in range(unroll):
                kt = it * unroll + u
                k0 = pl.multiple_of(kt * TK, TK)
                tiles.append((
                    lambda g, k0=k0: kf_ref[pl.ds(k0, TK), g * gw:(g + 1) * gw],
                    lambda h, kt=kt: vt_ref[kt, h * vd:(h + 1) * vd, :],
                    None))
            _flash_tiles(cfg, tiles, rhs_ref, m_ref, acc_ref, bounded)
            return carry

        lax.fori_loop(0, n_keys // (TK * unroll), body, 0)

    lax.cond(_scores_are_bounded(cfg, rhs_ref, kn_ref),
             lambda: all_keys(True, math.gcd(cfg.unroll, n_keys // TK)),
             lambda: all_keys(False, 1))

    heads = []
    if cfg is DA_CFG:
        lp = lam_ref[...]
        lam = (jnp.exp(jnp.sum(lp[0:1] * lp[1:2], axis=1, keepdims=True))
               - jnp.exp(jnp.sum(lp[2:3] * lp[3:4], axis=1, keepdims=True)) + lam_init)
        for h in range(DA_HEADS):
            a1, a2 = acc_ref[2 * h], acc_ref[2 * h + 1]
            o = a1[0:vd] / a1[vd:vd + 1] - lam * (a2[0:vd] / a2[vd:vd + 1])
            o = o * lax.rsqrt(jnp.mean(o * o, axis=0, keepdims=True) + EPS) * gs_ref[...]
            heads.append(o * (1.0 - lam_init))
    else:
        for h in range(MLA_HEADS):
            a = acc_ref[h]
            heads.append(a[0:vd] / a[vd:vd + 1])
    pairs = [jnp.concatenate(heads[2 * g:2 * g + 2], axis=0).T for g in range(len(heads) // 2)]
    out = jnp.concatenate(pairs, axis=1)
    if meta_queries:
        stage_ref = rest[0]
        stage_ref[...] = out
        r0 = pl.multiple_of((b % (LANES // N_META)) * N_META, N_META)
        o_ref[...] = stage_ref[pl.ds(r0, N_META), :].astype(o_ref.dtype)
    else:
        o_ref[...] = out.astype(o_ref.dtype)


def _flash_call(cfg, grp, qt, kf, vt, extra, o_prev, lam_init, meta_queries, name):
    n, bsz = grp.n, grp.batch
    qw, kw, vw = qt.shape[1], kf.shape[1], vt.shape[1]
    tiles_per_seq = n // TM
    meta_blk0 = grp.meta0 // LANES
    per_blk = LANES // N_META
    if meta_queries:
        grid = (bsz,)
        tq = LANES
        q_map = lambda b: ((meta_blk0 + b // per_blk) // (TM // LANES), 0, (meta_blk0 + b // per_blk) % (TM // LANES))
        o_spec = pl.BlockSpec((N_META, cfg.out_w), lambda b: (grp.meta0 // N_META + b, 0))
        fix = lambda f: (lambda b: f(b))
        sem = ("arbitrary",)
    else:
        grid = (bsz, tiles_per_seq)
        tq = TM
        q_map = lambda b, i: (grp.real0 // TM + b * tiles_per_seq + i, 0, 0)
        o_spec = pl.BlockSpec((TM, cfg.out_w), lambda b, i: (grp.real0 // TM + b * tiles_per_seq + i, 0))
        fix = lambda f: (lambda b, i: f(b))
        sem = ("arbitrary", "arbitrary")
    in_specs = [
        pl.BlockSpec((1, qw, tq), q_map),
        pl.BlockSpec((n, kw), fix(lambda b: (grp.real0 // n + b, 0))),
        pl.BlockSpec((n // TK, vw, TK), fix(lambda b: (grp.real0 // n + b, 0, 0))),
        pl.BlockSpec((LANES, kw), fix(lambda b: (meta_blk0 + b // per_blk, 0))),
        pl.BlockSpec((1, vw, LANES), fix(lambda b: ((meta_blk0 + b // per_blk) // (TK // LANES), 0,
                                                     (meta_blk0 + b // per_blk) % (TK // LANES)))),
    ]
    args = [qt, kf, vt, kf, vt]
    for a in extra:
        in_specs.append(pl.BlockSpec(a.shape, fix(lambda b, nd=a.ndim: (0,) * nd)))
        args.append(a)
    in_specs.append(pl.BlockSpec(memory_space=pl.ANY))
    args.append(o_prev)
    scratch = [
        pltpu.VMEM((cfg.n_soft, cfg.group_w, tq), jnp.bfloat16),
        pltpu.VMEM((cfg.n_soft, 1, tq), jnp.float32),
        pltpu.VMEM((cfg.n_soft, DA_VD + ONES_ROWS, tq), jnp.float32),
        pltpu.VMEM((cfg.n_soft, 1, LANES), jnp.float32),
    ]
    if meta_queries:
        scratch.append(pltpu.VMEM((LANES, cfg.out_w), jnp.float32))

    def body(*refs):
        n_in = len(args)
        ins, rest = refs[:n_in - 1], refs[n_in:]
        _flash_kernel(*ins, *rest, cfg=cfg, n_keys=n, lam_init=lam_init, meta_queries=meta_queries)

    return pl.pallas_call(
        body,
        grid=grid,
        in_specs=in_specs,
        out_specs=o_spec,
        out_shape=jax.ShapeDtypeStruct(o_prev.shape, o_prev.dtype),
        scratch_shapes=scratch,
        input_output_aliases={len(args) - 1: 0},
        compiler_params=pltpu.CompilerParams(dimension_semantics=sem, vmem_limit_bytes=VMEM_LIMIT_BYTES),
        name=name,
    )(*args)


def _na_kernel(qt_ref, kf_ref, vt_ref, kfm_ref, vtm_ref, bias_ref, bmax_ref, prev_ref, o_ref, kn_ref, *, grid_rows):
    del prev_ref
    b, i = pl.program_id(0), pl.program_id(1)

    @pl.when(i == 0)
    def _():
        _max_key_norms(NA_CFG, kf_ref, kfm_ref, grid_rows * GRID_W, kn_ref)

    ws = jnp.clip(i * NA_QROWS - WIN_R // 2, 0, grid_rows - NA_KROWS)
    kwin = kf_ref[pl.ds(pl.multiple_of(ws * GRID_W, 2 * GRID_W), NA_KROWS * GRID_W), :]
    vwin = vt_ref[pl.ds(ws // 2, NA_KROWS // 2)]
    kmeta = kfm_ref[...]
    mmask = _meta_key_mask(b)

    def rhs_of(h):
        g, j = h // 2, h % 2
        return _row_band(qt_ref[0, g * LANES:(g + 1) * LANES, :], j * NA_HD, (j + 1) * NA_HD)

    def scores(h):
        g, rhs = h // 2, rhs_of(h)
        s1 = jnp.dot(kwin[:, g * LANES:(g + 1) * LANES], rhs, preferred_element_type=jnp.float32) + bias_ref[0, h]
        s2 = jnp.dot(kmeta[:, g * LANES:(g + 1) * LANES], rhs, preferred_element_type=jnp.float32) + mmask
        return s1, s2

    def run(bounded):
        heads = []
        pending = [scores(h) for h in range(LOOKAHEAD)]
        for h in range(NA_HEADS):
            s1, s2 = pending.pop(0)
            if h + LOOKAHEAD < NA_HEADS:
                pending.append(scores(h + LOOKAHEAD))
            if not bounded:
                m = jnp.maximum(jnp.max(s1, axis=0, keepdims=True), jnp.max(s2, axis=0, keepdims=True))
                s1, s2 = s1 - m, s2 - m
            p1 = jnp.exp2(s1).astype(jnp.bfloat16)
            p2 = jnp.exp2(s2).astype(jnp.bfloat16)
            v1 = jnp.concatenate([vwin[c, h * NA_HD:(h + 1) * NA_HD, :] for c in range(NA_KROWS // 2)], axis=1)
            acc = (jnp.dot(_with_ones(v1), p1, preferred_element_type=jnp.float32)
                   + jnp.dot(_with_ones(vtm_ref[0, h * NA_HD:(h + 1) * NA_HD, :]), p2,
                             preferred_element_type=jnp.float32))
            heads.append(acc[0:NA_HD] / acc[NA_HD:NA_HD + 1])
        pairs = [jnp.concatenate(heads[2 * g:2 * g + 2], axis=0).T for g in range(NA_HEADS // 2)]
        o_ref[...] = jnp.concatenate(pairs, axis=1).astype(o_ref.dtype)

    limit = SCORE_BOUND - bmax_ref[...]
    ok = None
    for h in range(NA_HEADS):
        r = rhs_of(h).astype(jnp.float32)
        bound2 = jnp.sum(r * r, axis=0, keepdims=True) * kn_ref[h][:, 0:1] * NORM_SLACK
        good = (limit > 0.0) & (bound2 <= limit * limit)
        ok = good if ok is None else ok & good
    lax.cond(jnp.min(jnp.where(ok, 1.0, 0.0)) > 0.5, lambda: run(True), lambda: run(False))


def _na_meta_kernel(qt_ref, kf_ref, vt_ref, prev_ref, o_ref):
    del prev_ref
    kr = lax.broadcasted_iota(jnp.int32, (LANES, LANES), 0) // N_META
    qc = lax.broadcasted_iota(jnp.int32, (LANES, LANES), 1) // N_META
    mask = jnp.where(kr == qc, 0.0, NEG).astype(jnp.float32)
    kf = kf_ref[...]
    heads = []
    for h in range(NA_HEADS):
        g, j = h // 2, h % 2
        rhs = _row_band(qt_ref[0, g * LANES:(g + 1) * LANES, :], j * NA_HD, (j + 1) * NA_HD)
        s = jnp.dot(kf[:, g * LANES:(g + 1) * LANES], rhs, preferred_element_type=jnp.float32) + mask
        p = jnp.exp2(s - jnp.max(s, axis=0, keepdims=True)).astype(jnp.bfloat16)
        acc = jnp.dot(_with_ones(vt_ref[0, h * NA_HD:(h + 1) * NA_HD, :]), p, preferred_element_type=jnp.float32)
        heads.append(acc[0:NA_HD] / acc[NA_HD:NA_HD + 1])
    pairs = [jnp.concatenate(heads[2 * g:2 * g + 2], axis=0).T for g in range(NA_HEADS // 2)]
    o_ref[...] = jnp.concatenate(pairs, axis=1).astype(o_ref.dtype)


def _na_call(grp, qt, kf, vt, bias, bmax, o_prev):
    n, bsz = grp.n, grp.batch
    grid_rows = n // GRID_W
    steps = grid_rows // NA_QROWS
    tq = NA_QROWS * GRID_W
    meta_blk0 = grp.meta0 // LANES
    per_blk = LANES // N_META
    in_specs = [
        pl.BlockSpec((1, NA_W, tq), lambda b, i: (grp.real0 // TM + b * (n // TM) + i // (TM // tq), 0, i % (TM // tq))),
        pl.BlockSpec((n, NA_W), lambda b, i: (grp.real0 // n + b, 0)),
        pl.BlockSpec((n // LANES, NA_W, LANES), lambda b, i: (grp.real0 // n + b, 0, 0)),
        pl.BlockSpec((LANES, NA_W), lambda b, i: (meta_blk0 + b // per_blk, 0)),
        pl.BlockSpec((1, NA_W, LANES), lambda b, i: (meta_blk0 + b // per_blk, 0, 0)),
        pl.BlockSpec((1, NA_HEADS, NA_KROWS * GRID_W, tq),
                     lambda b, i: (jnp.where(i == 0, 0, jnp.where(i == steps - 1, 2, 1)), 0, 0, 0)),
        pl.BlockSpec((1, 1), lambda b, i: (0, 0)),
        pl.BlockSpec(memory_space=pl.ANY),
    ]
    return pl.pallas_call(
        functools.partial(_na_kernel, grid_rows=grid_rows),
        grid=(bsz, steps),
        in_specs=in_specs,
        out_specs=pl.BlockSpec((tq, NA_W), lambda b, i: (grp.real0 // tq + b * steps + i, 0)),
        out_shape=jax.ShapeDtypeStruct(o_prev.shape, o_prev.dtype),
        scratch_shapes=[pltpu.VMEM((NA_HEADS, 1, LANES), jnp.float32)],
        input_output_aliases={7: 0},
        compiler_params=pltpu.CompilerParams(dimension_semantics=("arbitrary", "arbitrary"),
                                             vmem_limit_bytes=VMEM_LIMIT_BYTES),
        name="na_real",
    )(qt, kf, vt, kf, vt, bias, bmax, o_prev)


def _na_meta_call(layout, qt, kf, vt, o_prev):
    blk0 = layout.groups[0].meta0 // LANES
    nblk = sum(_round_up(g.batch * N_META, LANES) for g in layout.groups) // LANES
    sub = TM // LANES
    return pl.pallas_call(
        _na_meta_kernel,
        grid=(nblk,),
        in_specs=[
            pl.BlockSpec((1, NA_W, LANES), lambda i: ((blk0 + i) // sub, 0, (blk0 + i) % sub)),
            pl.BlockSpec((LANES, NA_W), lambda i: (blk0 + i, 0)),
            pl.BlockSpec((1, NA_W, LANES), lambda i: (blk0 + i, 0, 0)),
            pl.BlockSpec(memory_space=pl.ANY),
        ],
        out_specs=pl.BlockSpec((LANES, NA_W), lambda i: (blk0 + i, 0)),
        out_shape=jax.ShapeDtypeStruct(o_prev.shape, o_prev.dtype),
        input_output_aliases={3: 0},
        compiler_params=pltpu.CompilerParams(dimension_semantics=("arbitrary",),
                                             vmem_limit_bytes=VMEM_LIMIT_BYTES),
        name="na_meta",
    )(qt, kf, vt, o_prev)


def _na_bias_tables(rel_bias):
    kc = np.arange(GRID_W)[:, None]
    qc = np.arange(GRID_W)[None, :]
    cs = np.clip(qc - WIN_C // 2, 0, GRID_W - WIN_C)
    col_ok = (kc >= cs) & (kc < cs + WIN_C)
    cidx = np.clip(kc - qc + WIN_C - 1, 0, 2 * WIN_C - 2)
    planes = jnp.where(col_ok[None, None], rel_bias[:, :, cidx] * LOG2E, NEG)
    masked = 2 * WIN_R - 1
    planes = jnp.concatenate([planes, jnp.full((NA_HEADS, 1, GRID_W, GRID_W), NEG, planes.dtype)], axis=1)
    kr = np.arange(NA_KROWS)[:, None]
    qr = np.arange(NA_QROWS)[None, :]
    ridx = []
    for win_lo, q_off in ((0 * qr, qr), (qr, qr + WIN_R // 2), (0 * qr + NA_KROWS - WIN_R, qr + NA_KROWS - NA_QROWS)):
        row_ok = (kr >= win_lo) & (kr < win_lo + WIN_R)
        ridx.append(np.where(row_ok, kr - q_off + WIN_R - 1, masked))
    ridx = np.stack(ridx)
    assert ridx.min() >= 0 and ridx.max() <= masked
    t = planes[:, ridx]
    t = jnp.transpose(t, (1, 0, 2, 4, 3, 5))
    return t.reshape(3, NA_HEADS, NA_KROWS * GRID_W, NA_QROWS * GRID_W).astype(jnp.float32)


def _rope_tables(layout):
    pos, starts, periods = [], [], []
    for g in layout.groups:
        starts.append(len(pos))
        periods.append(g.n // TM)
        for t in range(g.n // TM):
            pos.append(N_META + t * TM + np.arange(TM))
    tail0 = sum(g.batch * g.n for g in layout.groups)
    starts.append(len(pos))
    for t in range(tail0 // TM, layout.rows // TM):
        r = t * TM + np.arange(TM)
        p = np.zeros((TM,), np.int64)
        for g in layout.groups:
            inside = (r >= g.meta0) & (r < g.meta0 + g.batch * N_META)
            p = np.where(inside, (r - g.meta0) % N_META, p)
        pos.append(p)
    pos = jnp.asarray(np.concatenate(pos).astype(np.float32))
    bounds = [g.real0 // TM for g in layout.groups] + [tail0 // TM]

    def block_of_tile(i):
        blk = starts[-1] + (i - bounds[-1])
        for k in range(len(layout.groups) - 1, -1, -1):
            blk = jnp.where(i < bounds[k + 1], starts[k] + (i - bounds[k]) % periods[k], blk)
        return blk

    def tables(dim, period):
        inv = ROPE_THETA ** (-(jnp.arange(0, dim, 2, dtype=jnp.float32) / dim))
        ang = pos[:, None] * inv[None, :]
        cos, sin = jnp.cos(ang), jnp.sin(ang)
        rest = period - dim
        c = jnp.concatenate([cos, cos, jnp.ones((pos.shape[0], rest), jnp.float32)], axis=1)
        s = jnp.concatenate([-sin, sin, jnp.zeros((pos.shape[0], rest), jnp.float32)], axis=1)
        reps = LANES // period
        return jnp.tile(c, (1, reps)), jnp.tile(s, (1, reps))

    cda, sda = tables(DA_ROT, DA_HD)
    cm, sm = tables(MLA_ROPE, MLA_ROPE)
    return (cda, sda, cm, sm), block_of_tile


def _prep_layer(l, norm_g, ffn_w_gate, ffn_w_up, ffn_w_down, w_in, w_out, mla_q_norm_g, mla_kv_norm_g,
                mla_w_uq, mla_w_ukv, da_subln_g):
    bf = jnp.bfloat16
    kr0 = IN_W - MLA_ROPE
    win = jnp.concatenate([w_in[l, :, :kr0]] + [w_in[l, :, kr0:]] * (LANES // MLA_ROPE), axis=1).astype(bf)
    uq = mla_w_uq[l].reshape(Q_LORA, MLA_HEADS, MLA_NOPE + MLA_ROPE)
    uq = jnp.pad(uq, ((0, 0), (0, 0), (0, LANES - MLA_NOPE - MLA_ROPE))).reshape(Q_LORA, MLA_HEADS * LANES)
    ukv = mla_w_ukv[l].reshape(KV_LORA, MLA_HEADS, MLA_NOPE + MLA_VD)
    uk = jnp.pad(ukv[:, :, :MLA_NOPE], ((0, 0), (0, 0), (0, LANES - MLA_NOPE))).reshape(KV_LORA, MLA_HEADS * LANES)
    ukv = jnp.concatenate([uk, ukv[:, :, MLA_NOPE:].reshape(KV_LORA, -1)], axis=1)
    row = lambda v: v.reshape(1, -1)
    return dict(
        g=[row(norm_g[l, i]) for i in range(3)],
        ffn=[(ffn_w_gate[l, i].astype(bf), ffn_w_up[l, i].astype(bf), ffn_w_down[l, i].astype(bf)) for i in range(2)],
        win=win, uq=uq.astype(bf), ukv=ukv.astype(bf),
        gq=row(mla_q_norm_g[l]), gkv=row(mla_kv_norm_g[l]),
        wo=(w_out[l, :NA_W].astype(bf), w_out[l, NA_W:NA_W + DA_W].astype(bf), w_out[l, NA_W + DA_W:].astype(bf)),
        gs=da_subln_g[l].reshape(DA_VD, 1),
    )


def kernel(x_prompt, x_sample, meta_tokens, norm_g, final_norm_g, ffn_w_gate, ffn_w_up, ffn_w_down, w_in, w_out, na_rel_bias, da_lambda, da_subln_g, mla_q_norm_g, mla_kv_norm_g, mla_w_uq, mla_w_ukv):
    xs = (x_prompt, x_sample)
    layout = _make_layout([(x.shape[0], x.shape[1]) for x in xs])
    real_rows = sum(g.batch * g.n for g in layout.groups)
    tail = []
    for g in layout.groups:
        blk = jnp.tile(meta_tokens.astype(jnp.float32), (g.batch, 1))
        tail.append(jnp.pad(blk, ((0, _round_up(g.batch * N_META, LANES) - g.batch * N_META), (0, 0))))
    tail = jnp.concatenate(tail, axis=0)
    tail = jnp.pad(tail, ((0, layout.rows - real_rows - tail.shape[0]), (0, 0)))
    srcs = [x.reshape(-1, D_MODEL) for x in xs] + [tail]
    real_bounds = tuple(g.real0 // TM for g in layout.groups) + (real_rows // TM,)
    bounds = real_bounds + (layout.rows // TM,)
    tabs, tab_block = _rope_tables(layout)
    gf = final_norm_g.reshape(1, -1)

    o_na = jnp.zeros((layout.rows, NA_W), jnp.bfloat16)
    o_da = jnp.zeros((layout.rows, DA_W), jnp.bfloat16)
    o_m = jnp.zeros((layout.rows, MLA_W), jnp.bfloat16)
    for l in range(DEPTH):
        p = _prep_layer(l, norm_g, ffn_w_gate, ffn_w_up, ffn_w_down, w_in, w_out, mla_q_norm_g, mla_kv_norm_g,
                        mla_w_uq, mla_w_ukv, da_subln_g)
        lam_init = 0.8 - 0.6 * math.exp(-0.3 * l)
        (h, naq, nak, nav, daq, dak, dav, mq, mk, mv) = _ffn_inproj(
            srcs, bounds, tabs, tab_block, p["g"][0], p["g"][1], *p["ffn"][0], p["win"], p["gq"], p["gkv"], p["uq"], p["ukv"])
        bias = _na_bias_tables(na_rel_bias[l])
        bmax = (jnp.max(jnp.abs(na_rel_bias[l])) * LOG2E).reshape(1, 1)
        o_na = _na_meta_call(layout, naq, nak, nav, o_na)
        for gi, grp in enumerate(layout.groups):
            o_na = _na_call(grp, naq, nak, nav, bias, bmax, o_na)
            for meta_q in (False, True):
                tag = f"g{gi}_{'meta' if meta_q else 'real'}"
                o_da = _flash_call(DA_CFG, grp, daq, dak, dav, (da_lambda[l], p["gs"]), o_da, lam_init, meta_q,
                                   "da_" + tag)
                o_m = _flash_call(MLA_CFG, grp, mq, mk, mv, (), o_m, lam_init, meta_q, "mla_" + tag)
        last = l == DEPTH - 1
        h = _outproj_ffn(h, o_na, o_da, o_m, *p["wo"], p["g"][2], *p["ffn"][1], gf, real_bounds if last else None)
        srcs, bounds = [h], (0, layout.rows // TM)

    return tuple(y.reshape(x.shape) for x, y in zip(xs, h))
```

```python
import functools
import math
from typing import NamedTuple

import jax
import jax.numpy as jnp
import numpy as np
from jax import lax
from jax.experimental import pallas as pl
from jax.experimental.pallas import tpu as pltpu

D_MODEL = 1024
DEPTH = 2
GRID_W = 64
N_META = 16
WIN_R = 8
WIN_C = 16
NA_HEADS = 6
NA_HD = 64
DA_HEADS = 6
DA_HD = 32
DA_VD = 64
DA_ROT = DA_HD // 4
MLA_HEADS = 4
MLA_NOPE = 64
MLA_ROPE = 32
MLA_VD = 64
Q_LORA = 256
KV_LORA = 128
ROPE_THETA = 500000.0
D_FF = 2816
EPS = 1e-6
NA_W = NA_HEADS * NA_HD
DA_W = DA_HEADS * DA_VD
MLA_W = MLA_HEADS * MLA_VD
DA_QK_W = DA_HEADS * 2 * DA_HD
IN_W = 3 * NA_W + 2 * DA_QK_W + DA_W + Q_LORA + KV_LORA + MLA_ROPE

LANES = 128
MXU_DIM = 256
VMEM_LIMIT_BYTES = 58 * 1024 * 1024

TM = 512
TK = 256
NA_QROWS = 4
NA_KROWS = NA_QROWS + WIN_R
LOOKAHEAD = 2
FF_SPLIT = 1536
IN_W_PAD = 2816

LOG2E = 1.4426950408889634
NEG = -1e30
ONES_ROWS = 16
SCORE_BOUND = 60.0
NORM_SLACK = 1.05


def _round_up(x, m):
    return (x + m - 1) // m * m


class Group(NamedTuple):
    batch: int
    n: int
    real0: int
    meta0: int


class Layout(NamedTuple):
    groups: tuple
    rows: int


def _make_layout(shapes):
    row = 0
    real0 = []
    for b, n in shapes:
        assert n % TM == 0 and row % n == 0 and n % GRID_W == 0
        assert (n // GRID_W) % NA_QROWS == 0 and n // GRID_W >= NA_KROWS
        real0.append(row)
        row += b * n
    groups = []
    for (b, n), r0 in zip(shapes, real0):
        groups.append(Group(b, n, r0, row))
        row += _round_up(b * N_META, LANES)
    return Layout(tuple(groups), _round_up(row, TM))


def _rms(x, g):
    return x * lax.rsqrt(jnp.mean(x * x, axis=-1, keepdims=True) + EPS) * g


def _swiglu_half(xn, wg_ref, wu_ref, wd_ref):
    acc = None
    for lo, hi in ((0, FF_SPLIT), (FF_SPLIT, D_FF)):
        gate = jnp.dot(xn, wg_ref[:, lo:hi], preferred_element_type=jnp.float32)
        up = jnp.dot(xn, wu_ref[:, lo:hi], preferred_element_type=jnp.float32)
        hm = (gate * jax.nn.sigmoid(gate) * up).astype(jnp.bfloat16)
        part = jnp.dot(hm, wd_ref[lo:hi, :], preferred_element_type=jnp.float32)
        acc = part if acc is None else acc + part
    return 0.5 * acc


def _rope_chunk(x, c, s, half, period):
    lane = lax.broadcasted_iota(jnp.int32, x.shape, 1)
    lo = (lane & (period - 1)) < half
    partner = jnp.where(lo, pltpu.roll(x, LANES - half, 1), pltpu.roll(x, half, 1))
    return x * c + partner * s


def _segment_specs(bounds):
    return [pl.BlockSpec((TM, D_MODEL), lambda i, lo=lo, hi=hi: (jnp.clip(i - lo, 0, hi - lo - 1), 0))
            for lo, hi in zip(bounds[:-1], bounds[1:])]


def _ffn_inproj_kernel(*refs, bounds):
    n_src = len(bounds) - 1
    srcs = refs[:n_src]
    (cda_ref, sda_ref, cm_ref, sm_ref, ga_ref, gb_ref,
     wg_ref, wu_ref, wd_ref, win_ref, gq_ref, gkv_ref, wuq_ref, wukv_ref,
     h1_ref, naq_ref, nak_ref, nav_ref, daq_ref, dak_ref, dav_ref,
     mq_ref, mk_ref, mv_ref) = refs[n_src:]
    i = pl.program_id(0)
    x = srcs[-1][...]
    for sgm in range(n_src - 2, -1, -1):
        x = jnp.where(i < bounds[sgm + 1], srcs[sgm][...], x)
    xn = _rms(x, ga_ref[...]).astype(jnp.bfloat16)
    h1 = x + _swiglu_half(xn, wg_ref, wu_ref, wd_ref)
    h1_ref[...] = h1
    xn2 = _rms(h1, gb_ref[...]).astype(jnp.bfloat16)

    u = jnp.dot(xn2, win_ref[:, 0:3 * NA_W], preferred_element_type=jnp.float32)
    naq_ref[0] = (u[:, 0:NA_W] * (NA_HD ** -0.5 * LOG2E)).T.astype(jnp.bfloat16)
    nak_ref[...] = u[:, NA_W:2 * NA_W].astype(jnp.bfloat16)
    vt = u[:, 2 * NA_W:3 * NA_W].T.astype(jnp.bfloat16)
    for c in range(TM // LANES):
        nav_ref[c] = vt[:, c * LANES:(c + 1) * LANES]

    o0 = 3 * NA_W
    u = jnp.dot(xn2, win_ref[:, o0:o0 + 2 * DA_QK_W + DA_W], preferred_element_type=jnp.float32)
    cda, sda = cda_ref[...], sda_ref[...]
    q = jnp.concatenate([_rope_chunk(u[:, c * LANES:(c + 1) * LANES], cda, sda, DA_ROT // 2, DA_HD)
                         for c in range(DA_QK_W // LANES)], axis=1)
    k = jnp.concatenate([_rope_chunk(u[:, DA_QK_W + c * LANES:DA_QK_W + (c + 1) * LANES], cda, sda, DA_ROT // 2, DA_HD)
                         for c in range(DA_QK_W // LANES)], axis=1)
    daq_ref[0] = (q * (DA_HD ** -0.5 * LOG2E)).T.astype(jnp.bfloat16)
    dak_ref[...] = k.astype(jnp.bfloat16)
    vt = u[:, 2 * DA_QK_W:2 * DA_QK_W + DA_W].T.astype(jnp.bfloat16)
    for c in range(TM // TK):
        dav_ref[c] = vt[:, c * TK:(c + 1) * TK]

    o1 = o0 + 2 * DA_QK_W + DA_W
    u = jnp.dot(xn2, win_ref[:, o1:IN_W_PAD], preferred_element_type=jnp.float32)
    cm, sm = cm_ref[...], sm_ref[...]
    lane = lax.broadcasted_iota(jnp.int32, cm.shape, 1)
    band = (lane >= MLA_NOPE) & (lane < MLA_NOPE + MLA_ROPE)
    cb, sb = jnp.where(band, cm, 1.0), jnp.where(band, sm, 0.0)
    cq = _rms(u[:, 0:Q_LORA], gq_ref[...]).astype(jnp.bfloat16)
    qm = jnp.dot(cq, wuq_ref[...], preferred_element_type=jnp.float32)
    qm = jnp.concatenate([_rope_chunk(qm[:, h * LANES:(h + 1) * LANES], cb, sb, MLA_ROPE // 2, MLA_ROPE)
                          for h in range(MLA_HEADS)], axis=1) * ((MLA_NOPE + MLA_ROPE) ** -0.5 * LOG2E)
    mq_ref[0] = qm.T.astype(jnp.bfloat16)
    ckv = _rms(u[:, Q_LORA:Q_LORA + KV_LORA], gkv_ref[...]).astype(jnp.bfloat16)
    kv = jnp.dot(ckv, wukv_ref[...], preferred_element_type=jnp.float32)
    kr = _rope_chunk(u[:, Q_LORA + KV_LORA:Q_LORA + KV_LORA + LANES], cm, sm, MLA_ROPE // 2, MLA_ROPE)
    kr = jnp.where(band, kr, 0.0)
    kf = jnp.concatenate([kv[:, h * LANES:(h + 1) * LANES] + kr for h in range(MLA_HEADS)], axis=1)
    mk_ref[...] = kf.astype(jnp.bfloat16)
    vt = kv[:, MLA_HEADS * LANES:MLA_HEADS * LANES + MLA_W].T.astype(jnp.bfloat16)
    for c in range(TM // TK):
        mv_ref[c] = vt[:, c * TK:(c + 1) * TK]


def _outproj_ffn_kernel(h_ref, ona_ref, oda_ref, om_ref, wo1_ref, wo2_ref, wo3_ref, g_ref,
                        wg_ref, wu_ref, wd_ref, gf_ref, *o_refs, out_bounds):
    h = h_ref[...]
    h = h + jnp.dot(ona_ref[...], wo1_ref[...], preferred_element_type=jnp.float32)
    h = h + jnp.dot(oda_ref[...], wo2_ref[...], preferred_element_type=jnp.float32)
    h = h + jnp.dot(om_ref[...], wo3_ref[...], preferred_element_type=jnp.float32)
    xn = _rms(h, g_ref[...]).astype(jnp.bfloat16)
    h = h + _swiglu_half(xn, wg_ref, wu_ref, wd_ref)
    if out_bounds is None:
        o_refs[0][...] = h
    else:
        h = _rms(h, gf_ref[...])
        i = pl.program_id(0)
        for o_ref, lo, hi in zip(o_refs, out_bounds[:-1], out_bounds[1:]):
            @pl.when((i >= lo) & (i < hi))
            def _(o_ref=o_ref):
                o_ref[...] = h


def _const_spec(shape):
    nd = len(shape)
    return pl.BlockSpec(shape, lambda i: (0,) * nd, pipeline_mode=pl.Buffered(1))


def _ffn_inproj(srcs, bounds, tabs, tab_block, ga, gb, wg, wu, wd, win, gq, gkv, wuq, wukv):
    nt = bounds[-1]
    rows = nt * TM
    row_spec = lambda w: pl.BlockSpec((TM, w), lambda i: (i, 0))
    tile_spec = lambda r: pl.BlockSpec((1, r, TM), lambda i: (i, 0, 0))
    bf = jnp.bfloat16
    out_shape = (
        jax.ShapeDtypeStruct((rows, D_MODEL), jnp.float32),
        jax.ShapeDtypeStruct((nt, NA_W, TM), bf),
        jax.ShapeDtypeStruct((rows, NA_W), bf),
        jax.ShapeDtypeStruct((rows // LANES, NA_W, LANES), bf),
        jax.ShapeDtypeStruct((nt, DA_QK_W, TM), bf),
        jax.ShapeDtypeStruct((rows, DA_QK_W), bf),
        jax.ShapeDtypeStruct((rows // TK, DA_W, TK), bf),
        jax.ShapeDtypeStruct((nt, MLA_HEADS * LANES, TM), bf),
        jax.ShapeDtypeStruct((rows, 2 * MXU_DIM), bf),
        jax.ShapeDtypeStruct((rows // TK, MLA_W, TK), bf),
    )
    out_specs = (
        row_spec(D_MODEL),
        tile_spec(NA_W),
        row_spec(NA_W),
        pl.BlockSpec((TM // LANES, NA_W, LANES), lambda i: (i, 0, 0)),
        tile_spec(DA_QK_W),
        row_spec(DA_QK_W),
        pl.BlockSpec((TM // TK, DA_W, TK), lambda i: (i, 0, 0)),
        tile_spec(MLA_HEADS * LANES),
        row_spec(2 * MXU_DIM),
        pl.BlockSpec((TM // TK, MLA_W, TK), lambda i: (i, 0, 0)),
    )
    in_specs = _segment_specs(bounds) + [pl.BlockSpec((TM, LANES), lambda i: (tab_block(i), 0))] * 4 + [
        _const_spec(a.shape) for a in (ga, gb, wg, wu, wd, win, gq, gkv, wuq, wukv)]
    return pl.pallas_call(
        functools.partial(_ffn_inproj_kernel, bounds=bounds),
        grid=(nt,),
        in_specs=in_specs,
        out_specs=out_specs,
        out_shape=out_shape,
        compiler_params=pltpu.CompilerParams(dimension_semantics=("arbitrary",),
                                             vmem_limit_bytes=VMEM_LIMIT_BYTES),
        name="ffn_inproj",
    )(*srcs, *tabs, ga, gb, wg, wu, wd, win, gq, gkv, wuq, wukv)


def _outproj_ffn(h, ona, oda, om, wo1, wo2, wo3, g, wg, wu, wd, gf, out_bounds):
    rows = h.shape[0]
    row_spec = lambda w: pl.BlockSpec((TM, w), lambda i: (i, 0))
    if out_bounds is None:
        out_specs = row_spec(D_MODEL)
        out_shape = jax.ShapeDtypeStruct((rows, D_MODEL), jnp.float32)
    else:
        out_specs = tuple(_segment_specs(out_bounds))
        out_shape = tuple(jax.ShapeDtypeStruct(((hi - lo) * TM, D_MODEL), jnp.float32)
                          for lo, hi in zip(out_bounds[:-1], out_bounds[1:]))
    in_specs = [row_spec(D_MODEL), row_spec(NA_W), row_spec(DA_W), row_spec(MLA_W)] + [
        _const_spec(a.shape) for a in (wo1, wo2, wo3, g, wg, wu, wd, gf)]
    return pl.pallas_call(
        functools.partial(_outproj_ffn_kernel, out_bounds=out_bounds),
        grid=(rows // TM,),
        in_specs=in_specs,
        out_specs=out_specs,
        out_shape=out_shape,
        compiler_params=pltpu.CompilerParams(dimension_semantics=("arbitrary",),
                                             vmem_limit_bytes=VMEM_LIMIT_BYTES),
        name="outproj_ffn",
    )(h, ona, oda, om, wo1, wo2, wo3, g, wg, wu, wd, gf)


def _row_band(block, lo, hi):
    row = lax.broadcasted_iota(jnp.int32, block.shape, 0)
    return jnp.where((row >= lo) & (row < hi), block, jnp.zeros_like(block))


def _with_ones(vt):
    return jnp.concatenate([vt, jnp.ones((ONES_ROWS, vt.shape[1]), vt.dtype)], axis=0)


def _meta_key_mask(b):
    row = lax.broadcasted_iota(jnp.int32, (LANES, 1), 0)
    lo = (b % (LANES // N_META)) * N_META
    return jnp.where((row >= lo) & (row < lo + N_META), 0.0, NEG).astype(jnp.float32)


class FlashCfg(NamedTuple):
    n_soft: int
    group_w: int
    soft_per_group: int
    v_of_soft: tuple
    out_w: int
    unroll: int


DA_CFG = FlashCfg(2 * DA_HEADS, LANES, 4, tuple(i // 2 for i in range(2 * DA_HEADS)), DA_W, 16)
MLA_CFG = FlashCfg(MLA_HEADS, LANES, 1, tuple(range(MLA_HEADS)), MLA_W, 32)
NA_CFG = FlashCfg(NA_HEADS, LANES, 2, tuple(range(NA_HEADS)), NA_W, 1)


def _build_rhs(cfg, qt_ref, rhs_ref):
    for i in range(cfg.n_soft):
        g, j = i // cfg.soft_per_group, i % cfg.soft_per_group
        if cfg is DA_CFG:
            rhs_ref[i] = _row_band(qt_ref[0, g * LANES:(g + 1) * LANES, :], j * DA_HD, (j + 1) * DA_HD)
        else:
            rhs_ref[i] = qt_ref[0, i * LANES:(i + 1) * LANES, :]


def _flash_tiles(cfg, tiles, rhs_ref, m_ref, acc_ref, bounded):
    nt = len(tiles)
    if bounded:
        work = [(t, i) for i in range(cfg.n_soft) for t in range(nt)]
    else:
        work = [(t, i) for t in range(nt) for i in range(cfg.n_soft)]

    def score(w):
        kblk_of, _, mask = tiles[w[0]]
        s = jnp.dot(kblk_of(w[1] // cfg.soft_per_group), rhs_ref[w[1]], preferred_element_type=jnp.float32)
        return s if mask is None else s + mask

    pending = [score(w) for w in work[:LOOKAHEAD]]
    part = den = None
    for n, (t, i) in enumerate(work):
        vt = tiles[t][1](cfg.v_of_soft[i])
        s = pending.pop(0)
        if n + LOOKAHEAD < len(work):
            pending.append(score(work[n + LOOKAHEAD]))
        if bounded:
            p = jnp.exp2(s)
            pv = jnp.dot(vt, p.astype(jnp.bfloat16), preferred_element_type=jnp.float32)
            ps = jnp.sum(p, axis=0, keepdims=True)
            part, den = (pv, ps) if t == 0 else (part + pv, den + ps)
            if t == nt - 1:
                acc_ref[i, 0:DA_VD, :] = acc_ref[i, 0:DA_VD, :] + part
                acc_ref[i, DA_VD:DA_VD + 1, :] = acc_ref[i, DA_VD:DA_VD + 1, :] + den
        else:
            vt = _with_ones(vt)
            m_prev = m_ref[i]
            m_new = jnp.maximum(m_prev, jnp.max(s, axis=0, keepdims=True))
            alpha = jnp.exp2(m_prev - m_new)
            p = jnp.exp2(s - m_new).astype(jnp.bfloat16)
            acc_ref[i] = alpha * acc_ref[i] + jnp.dot(vt, p, preferred_element_type=jnp.float32)
            m_ref[i] = m_new


def _feature_indicator(cfg):
    f = lax.broadcasted_iota(jnp.int32, (cfg.group_w, LANES), 0)
    j = lax.broadcasted_iota(jnp.int32, (cfg.group_w, LANES), 1)
    band = cfg.group_w // cfg.soft_per_group
    hit = jnp.right_shift(f, int(math.log2(band))) == j
    return jnp.where(hit & (j < cfg.soft_per_group), 1.0, 0.0).astype(jnp.bfloat16)


def _max_key_norms(cfg, kf_ref, kfm_ref, n_keys, kn_ref):
    gw = cfg.group_w
    n_groups = cfg.n_soft // cfg.soft_per_group
    ind = _feature_indicator(cfg)

    def sq_norms(blk):
        x = blk.astype(jnp.float32)
        n2 = jnp.dot((x * x).astype(jnp.bfloat16), ind, preferred_element_type=jnp.float32)
        return jnp.max(n2, axis=0, keepdims=True)

    def body(c, carry):
        r0 = pl.multiple_of(c * TM, TM)
        return tuple(jnp.maximum(carry[g], sq_norms(kf_ref[pl.ds(r0, TM), g * gw:(g + 1) * gw]))
                     for g in range(n_groups))

    init = tuple(sq_norms(kfm_ref[:, g * gw:(g + 1) * gw]) for g in range(n_groups))
    kmax = lax.fori_loop(0, n_keys // TM, body, init)
    lane = lax.broadcasted_iota(jnp.int32, (1, LANES), 1)
    for i in range(cfg.n_soft):
        g, j = divmod(i, cfg.soft_per_group)
        v = jnp.max(jnp.where(lane == j, kmax[g], 0.0), axis=1, keepdims=True)
        kn_ref[i] = jnp.broadcast_to(v, (1, LANES))


def _scores_are_bounded(cfg, rhs_ref, kn_ref):
    ok = None
    for i in range(cfg.n_soft):
        r = rhs_ref[i].astype(jnp.float32)
        bound2 = jnp.sum(r * r, axis=0, keepdims=True) * kn_ref[i][:, 0:1] * NORM_SLACK
        good = bound2 <= SCORE_BOUND * SCORE_BOUND
        ok = good if ok is None else ok & good
    return jnp.min(jnp.where(ok, 1.0, 0.0)) > 0.5


def _flash_kernel(*refs, cfg, n_keys, lam_init, meta_queries):
    if cfg is DA_CFG:
        (qt_ref, kf_ref, vt_ref, kfm_ref, vtm_ref, lam_ref, gs_ref, o_ref,
         rhs_ref, m_ref, acc_ref, kn_ref, *rest) = refs
    else:
        qt_ref, kf_ref, vt_ref, kfm_ref, vtm_ref, o_ref, rhs_ref, m_ref, acc_ref, kn_ref, *rest = refs
    b = pl.program_id(0)
    tq = qt_ref.shape[2]
    vd = DA_VD
    gw = cfg.group_w
    if meta_queries:
        _max_key_norms(cfg, kf_ref, kfm_ref, n_keys, kn_ref)
    else:
        @pl.when(pl.program_id(1) == 0)
        def _():
            _max_key_norms(cfg, kf_ref, kfm_ref, n_keys, kn_ref)
    _build_rhs(cfg, qt_ref, rhs_ref)
    m_ref[...] = jnp.full(m_ref.shape, NEG, jnp.float32)
    acc_ref[...] = jnp.zeros(acc_ref.shape, jnp.float32)

    def all_keys(bounded, unroll):
        _flash_tiles(
            cfg,
            [(lambda g: kfm_ref[:, g * gw:(g + 1) * gw],
              lambda h: vtm_ref[0, h * vd:(h + 1) * vd, :],
              _meta_key_mask(b))],
            rhs_ref, m_ref, acc_ref, bounded)

        def body(it, carry):
            tiles = []
            for u in range(unroll):
                kt = it * unroll + u
                k0 = pl.multiple_of(kt * TK, TK)
                tiles.append((
                    lambda g, k0=k0: kf_ref[pl.ds(k0, TK), g * gw:(g + 1) * gw],
                    lambda h, kt=kt: vt_ref[kt, h * vd:(h + 1) * vd, :],
                    None))
            _flash_tiles(cfg, tiles, rhs_ref, m_ref, acc_ref, bounded)
            return carry

        lax.fori_loop(0, n_keys // (TK * unroll), body, 0)

    lax.cond(_scores_are_bounded(cfg, rhs_ref, kn_ref),
             lambda: all_keys(True, math.gcd(cfg.unroll, n_keys // TK)),
             lambda: all_keys(False, 1))

    heads = []
    if cfg is DA_CFG:
        lp = lam_ref[...]
        lam = (jnp.exp(jnp.sum(lp[0:1] * lp[1:2], axis=1, keepdims=True))
               - jnp.exp(jnp.sum(lp[2:3] * lp[3:4], axis=1, keepdims=True)) + lam_init)
        for h in range(DA_HEADS):
            a1, a2 = acc_ref[2 * h], acc_ref[2 * h + 1]
            o = a1[0:vd] / a1[vd:vd + 1] - lam * (a2[0:vd] / a2[vd:vd + 1])
            o = o * lax.rsqrt(jnp.mean(o * o, axis=0, keepdims=True) + EPS) * gs_ref[...]
            heads.append(o * (1.0 - lam_init))
    else:
        for h in range(MLA_HEADS):
            a = acc_ref[h]
            heads.append(a[0:vd] / a[vd:vd + 1])
    pairs = [jnp.concatenate(heads[2 * g:2 * g + 2], axis=0).T for g in range(len(heads) // 2)]
    out = jnp.concatenate(pairs, axis=1)
    if meta_queries:
        stage_ref = rest[0]
        stage_ref[...] = out
        r0 = pl.multiple_of((b % (LANES // N_META)) * N_META, N_META)
        o_ref[...] = stage_ref[pl.ds(r0, N_META), :].astype(o_ref.dtype)
    else:
        o_ref[...] = out.astype(o_ref.dtype)


def _flash_call(cfg, grp, qt, kf, vt, extra, o_prev, lam_init, meta_queries, name):
    n, bsz = grp.n, grp.batch
    qw, kw, vw = qt.shape[1], kf.shape[1], vt.shape[1]
    tiles_per_seq = n // TM
    meta_blk0 = grp.meta0 // LANES
    per_blk = LANES // N_META
    if meta_queries:
        grid = (bsz,)
        tq = LANES
        q_map = lambda b: ((meta_blk0 + b // per_blk) // (TM // LANES), 0, (meta_blk0 + b // per_blk) % (TM // LANES))
        o_spec = pl.BlockSpec((N_META, cfg.out_w), lambda b: (grp.meta0 // N_META + b, 0))
        fix = lambda f: (lambda b: f(b))
        sem = ("arbitrary",)
    else:
        grid = (bsz, tiles_per_seq)
        tq = TM
        q_map = lambda b, i: (grp.real0 // TM + b * tiles_per_seq + i, 0, 0)
        o_spec = pl.BlockSpec((TM, cfg.out_w), lambda b, i: (grp.real0 // TM + b * tiles_per_seq + i, 0))
        fix = lambda f: (lambda b, i: f(b))
        sem = ("arbitrary", "arbitrary")
    in_specs = [
        pl.BlockSpec((1, qw, tq), q_map),
        pl.BlockSpec((n, kw), fix(lambda b: (grp.real0 // n + b, 0))),
        pl.BlockSpec((n // TK, vw, TK), fix(lambda b: (grp.real0 // n + b, 0, 0))),
        pl.BlockSpec((LANES, kw), fix(lambda b: (meta_blk0 + b // per_blk, 0))),
        pl.BlockSpec((1, vw, LANES), fix(lambda b: ((meta_blk0 + b // per_blk) // (TK // LANES), 0,
                                                     (meta_blk0 + b // per_blk) % (TK // LANES)))),
    ]
    args = [qt, kf, vt, kf, vt]
    for a in extra:
        in_specs.append(pl.BlockSpec(a.shape, fix(lambda b, nd=a.ndim: (0,) * nd)))
        args.append(a)
    in_specs.append(pl.BlockSpec(memory_space=pl.ANY))
    args.append(o_prev)
    scratch = [
        pltpu.VMEM((cfg.n_soft, cfg.group_w, tq), jnp.bfloat16),
        pltpu.VMEM((cfg.n_soft, 1, tq), jnp.float32),
        pltpu.VMEM((cfg.n_soft, DA_VD + ONES_ROWS, tq), jnp.float32),
        pltpu.VMEM((cfg.n_soft, 1, LANES), jnp.float32),
    ]
    if meta_queries:
        scratch.append(pltpu.VMEM((LANES, cfg.out_w), jnp.float32))

    def body(*refs):
        n_in = len(args)
        ins, rest = refs[:n_in - 1], refs[n_in:]
        _flash_kernel(*ins, *rest, cfg=cfg, n_keys=n, lam_init=lam_init, meta_queries=meta_queries)

    return pl.pallas_call(
        body,
        grid=grid,
        in_specs=in_specs,
        out_specs=o_spec,
        out_shape=jax.ShapeDtypeStruct(o_prev.shape, o_prev.dtype),
        scratch_shapes=scratch,
        input_output_aliases={len(args) - 1: 0},
        compiler_params=pltpu.CompilerParams(dimension_semantics=sem, vmem_limit_bytes=VMEM_LIMIT_BYTES),
        name=name,
    )(*args)


def _na_kernel(qt_ref, kf_ref, vt_ref, kfm_ref, vtm_ref, bias_ref, bmax_ref, prev_ref, o_ref, kn_ref, *, grid_rows):
    del prev_ref
    b, i = pl.program_id(0), pl.program_id(1)

    @pl.when(i == 0)
    def _():
        _max_key_norms(NA_CFG, kf_ref, kfm_ref, grid_rows * GRID_W, kn_ref)

    ws = jnp.clip(i * NA_QROWS - WIN_R // 2, 0, grid_rows - NA_KROWS)
    kwin = kf_ref[pl.ds(pl.multiple_of(ws * GRID_W, 2 * GRID_W), NA_KROWS * GRID_W), :]
    vwin = vt_ref[pl.ds(ws // 2, NA_KROWS // 2)]
    kmeta = kfm_ref[...]
    mmask = _meta_key_mask(b)

    def rhs_of(h):
        g, j = h // 2, h % 2
        return _row_band(qt_ref[0, g * LANES:(g + 1) * LANES, :], j * NA_HD, (j + 1) * NA_HD)

    def scores(h):
        g, rhs = h // 2, rhs_of(h)
        s1 = jnp.dot(kwin[:, g * LANES:(g + 1) * LANES], rhs, preferred_element_type=jnp.float32) + bias_ref[0, h]
        s2 = jnp.dot(kmeta[:, g * LANES:(g + 1) * LANES], rhs, preferred_element_type=jnp.float32) + mmask
        return s1, s2

    def run(bounded):
        heads = []
        pending = [scores(h) for h in range(LOOKAHEAD)]
        for h in range(NA_HEADS):
            s1, s2 = pending.pop(0)
            if h + LOOKAHEAD < NA_HEADS:
                pending.append(scores(h + LOOKAHEAD))
            if not bounded:
                m = jnp.maximum(jnp.max(s1, axis=0, keepdims=True), jnp.max(s2, axis=0, keepdims=True))
                s1, s2 = s1 - m, s2 - m
            p1 = jnp.exp2(s1).astype(jnp.bfloat16)
            p2 = jnp.exp2(s2).astype(jnp.bfloat16)
            v1 = jnp.concatenate([vwin[c, h * NA_HD:(h + 1) * NA_HD, :] for c in range(NA_KROWS // 2)], axis=1)
            acc = (jnp.dot(_with_ones(v1), p1, preferred_element_type=jnp.float32)
                   + jnp.dot(_with_ones(vtm_ref[0, h * NA_HD:(h + 1) * NA_HD, :]), p2,
                             preferred_element_type=jnp.float32))
            heads.append(acc[0:NA_HD] / acc[NA_HD:NA_HD + 1])
        pairs = [jnp.concatenate(heads[2 * g:2 * g + 2], axis=0).T for g in range(NA_HEADS // 2)]
        o_ref[...] = jnp.concatenate(pairs, axis=1).astype(o_ref.dtype)

    limit = SCORE_BOUND - bmax_ref[...]
    ok = None
    for h in range(NA_HEADS):
        r = rhs_of(h).astype(jnp.float32)
        bound2 = jnp.sum(r * r, axis=0, keepdims=True) * kn_ref[h][:, 0:1] * NORM_SLACK
        good = (limit > 0.0) & (bound2 <= limit * limit)
        ok = good if ok is None else ok & good
    lax.cond(jnp.min(jnp.where(ok, 1.0, 0.0)) > 0.5, lambda: run(True), lambda: run(False))


def _na_meta_kernel(qt_ref, kf_ref, vt_ref, prev_ref, o_ref):
    del prev_ref
    kr = lax.broadcasted_iota(jnp.int32, (LANES, LANES), 0) // N_META
    qc = lax.broadcasted_iota(jnp.int32, (LANES, LANES), 1) // N_META
    mask = jnp.where(kr == qc, 0.0, NEG).astype(jnp.float32)
    kf = kf_ref[...]
    heads = []
    for h in range(NA_HEADS):
        g, j = h // 2, h % 2
        rhs = _row_band(qt_ref[0, g * LANES:(g + 1) * LANES, :], j * NA_HD, (j + 1) * NA_HD)
        s = jnp.dot(kf[:, g * LANES:(g + 1) * LANES], rhs, preferred_element_type=jnp.float32) + mask
        p = jnp.exp2(s - jnp.max(s, axis=0, keepdims=True)).astype(jnp.bfloat16)
        acc = jnp.dot(_with_ones(vt_ref[0, h * NA_HD:(h + 1) * NA_HD, :]), p, preferred_element_type=jnp.float32)
        heads.append(acc[0:NA_HD] / acc[NA_HD:NA_HD + 1])
    pairs = [jnp.concatenate(heads[2 * g:2 * g + 2], axis=0).T for g in range(NA_HEADS // 2)]
    o_ref[...] = jnp.concatenate(pairs, axis=1).astype(o_ref.dtype)


def _na_call(grp, qt, kf, vt, bias, bmax, o_prev):
    n, bsz = grp.n, grp.batch
    grid_rows = n // GRID_W
    steps = grid_rows // NA_QROWS
    tq = NA_QROWS * GRID_W
    meta_blk0 = grp.meta0 // LANES
    per_blk = LANES // N_META
    in_specs = [
        pl.BlockSpec((1, NA_W, tq), lambda b, i: (grp.real0 // TM + b * (n // TM) + i // (TM // tq), 0, i % (TM // tq))),
        pl.BlockSpec((n, NA_W), lambda b, i: (grp.real0 // n + b, 0)),
        pl.BlockSpec((n // LANES, NA_W, LANES), lambda b, i: (grp.real0 // n + b, 0, 0)),
        pl.BlockSpec((LANES, NA_W), lambda b, i: (meta_blk0 + b // per_blk, 0)),
        pl.BlockSpec((1, NA_W, LANES), lambda b, i: (meta_blk0 + b // per_blk, 0, 0)),
        pl.BlockSpec((1, NA_HEADS, NA_KROWS * GRID_W, tq),
                     lambda b, i: (jnp.where(i == 0, 0, jnp.where(i == steps - 1, 2, 1)), 0, 0, 0)),
        pl.BlockSpec((1, 1), lambda b, i: (0, 0)),
        pl.BlockSpec(memory_space=pl.ANY),
    ]
    return pl.pallas_call(
        functools.partial(_na_kernel, grid_rows=grid_rows),
        grid=(bsz, steps),
        in_specs=in_specs,
        out_specs=pl.BlockSpec((tq, NA_W), lambda b, i: (grp.real0 // tq + b * steps + i, 0)),
        out_shape=jax.ShapeDtypeStruct(o_prev.shape, o_prev.dtype),
        scratch_shapes=[pltpu.VMEM((NA_HEADS, 1, LANES), jnp.float32)],
        input_output_aliases={7: 0},
        compiler_params=pltpu.CompilerParams(dimension_semantics=("arbitrary", "arbitrary"),
                                             vmem_limit_bytes=VMEM_LIMIT_BYTES),
        name="na_real",
    )(qt, kf, vt, kf, vt, bias, bmax, o_prev)


def _na_meta_call(layout, qt, kf, vt, o_prev):
    blk0 = layout.groups[0].meta0 // LANES
    nblk = sum(_round_up(g.batch * N_META, LANES) for g in layout.groups) // LANES
    sub = TM // LANES
    return pl.pallas_call(
        _na_meta_kernel,
        grid=(nblk,),
        in_specs=[
            pl.BlockSpec((1, NA_W, LANES), lambda i: ((blk0 + i) // sub, 0, (blk0 + i) % sub)),
            pl.BlockSpec((LANES, NA_W), lambda i: (blk0 + i, 0)),
            pl.BlockSpec((1, NA_W, LANES), lambda i: (blk0 + i, 0, 0)),
            pl.BlockSpec(memory_space=pl.ANY),
        ],
        out_specs=pl.BlockSpec((LANES, NA_W), lambda i: (blk0 + i, 0)),
        out_shape=jax.ShapeDtypeStruct(o_prev.shape, o_prev.dtype),
        input_output_aliases={3: 0},
        compiler_params=pltpu.CompilerParams(dimension_semantics=("arbitrary",),
                                             vmem_limit_bytes=VMEM_LIMIT_BYTES),
        name="na_meta",
    )(qt, kf, vt, o_prev)


def _na_bias_tables(rel_bias):
    kc = np.arange(GRID_W)[:, None]
    qc = np.arange(GRID_W)[None, :]
    cs = np.clip(qc - WIN_C // 2, 0, GRID_W - WIN_C)
    col_ok = (kc >= cs) & (kc < cs + WIN_C)
    cidx = np.clip(kc - qc + WIN_C - 1, 0, 2 * WIN_C - 2)
    planes = jnp.where(col_ok[None, None], rel_bias[:, :, cidx] * LOG2E, NEG)
    masked = 2 * WIN_R - 1
    planes = jnp.concatenate([planes, jnp.full((NA_HEADS, 1, GRID_W, GRID_W), NEG, planes.dtype)], axis=1)
    kr = np.arange(NA_KROWS)[:, None]
    qr = np.arange(NA_QROWS)[None, :]
    ridx = []
    for win_lo, q_off in ((0 * qr, qr), (qr, qr + WIN_R // 2), (0 * qr + NA_KROWS - WIN_R, qr + NA_KROWS - NA_QROWS)):
        row_ok = (kr >= win_lo) & (kr < win_lo + WIN_R)
        ridx.append(np.where(row_ok, kr - q_off + WIN_R - 1, masked))
    ridx = np.stack(ridx)
    assert ridx.min() >= 0 and ridx.max() <= masked
    t = planes[:, ridx]
    t = jnp.transpose(t, (1, 0, 2, 4, 3, 5))
    return t.reshape(3, NA_HEADS, NA_KROWS * GRID_W, NA_QROWS * GRID_W).astype(jnp.float32)


def _rope_tables(layout):
    pos, starts, periods = [], [], []
    for g in layout.groups:
        starts.append(len(pos))
        periods.append(g.n // TM)
        for t in range(g.n // TM):
            pos.append(N_META + t * TM + np.arange(TM))
    tail0 = sum(g.batch * g.n for g in layout.groups)
    starts.append(len(pos))
    for t in range(tail0 // TM, layout.rows // TM):
        r = t * TM + np.arange(TM)
        p = np.zeros((TM,), np.int64)
        for g in layout.groups:
            inside = (r >= g.meta0) & (r < g.meta0 + g.batch * N_META)
            p = np.where(inside, (r - g.meta0) % N_META, p)
        pos.append(p)
    pos = jnp.asarray(np.concatenate(pos).astype(np.float32))
    bounds = [g.real0 // TM for g in layout.groups] + [tail0 // TM]

    def block_of_tile(i):
        blk = starts[-1] + (i - bounds[-1])
        for k in range(len(layout.groups) - 1, -1, -1):
            blk = jnp.where(i < bounds[k + 1], starts[k] + (i - bounds[k]) % periods[k], blk)
        return blk

    def tables(dim, period):
        inv = ROPE_THETA ** (-(jnp.arange(0, dim, 2, dtype=jnp.float32) / dim))
        ang = pos[:, None] * inv[None, :]
        cos, sin = jnp.cos(ang), jnp.sin(ang)
        rest = period - dim
        c = jnp.concatenate([cos, cos, jnp.ones((pos.shape[0], rest), jnp.float32)], axis=1)
        s = jnp.concatenate([-sin, sin, jnp.zeros((pos.shape[0], rest), jnp.float32)], axis=1)
        reps = LANES // period
        return jnp.tile(c, (1, reps)), jnp.tile(s, (1, reps))

    cda, sda = tables(DA_ROT, DA_HD)
    cm, sm = tables(MLA_ROPE, MLA_ROPE)
    return (cda, sda, cm, sm), block_of_tile


def _prep_layer(l, norm_g, ffn_w_gate, ffn_w_up, ffn_w_down, w_in, w_out, mla_q_norm_g, mla_kv_norm_g,
                mla_w_uq, mla_w_ukv, da_subln_g):
    bf = jnp.bfloat16
    kr0 = IN_W - MLA_ROPE
    win = jnp.concatenate([w_in[l, :, :kr0]] + [w_in[l, :, kr0:]] * (LANES // MLA_ROPE), axis=1).astype(bf)
    uq = mla_w_uq[l].reshape(Q_LORA, MLA_HEADS, MLA_NOPE + MLA_ROPE)
    uq = jnp.pad(uq, ((0, 0), (0, 0), (0, LANES - MLA_NOPE - MLA_ROPE))).reshape(Q_LORA, MLA_HEADS * LANES)
    ukv = mla_w_ukv[l].reshape(KV_LORA, MLA_HEADS, MLA_NOPE + MLA_VD)
    uk = jnp.pad(ukv[:, :, :MLA_NOPE], ((0, 0), (0, 0), (0, LANES - MLA_NOPE))).reshape(KV_LORA, MLA_HEADS * LANES)
    ukv = jnp.concatenate([uk, ukv[:, :, MLA_NOPE:].reshape(KV_LORA, -1)], axis=1)
    row = lambda v: v.reshape(1, -1)
    return dict(
        g=[row(norm_g[l, i]) for i in range(3)],
        ffn=[(ffn_w_gate[l, i].astype(bf), ffn_w_up[l, i].astype(bf), ffn_w_down[l, i].astype(bf)) for i in range(2)],
        win=win, uq=uq.astype(bf), ukv=ukv.astype(bf),
        gq=row(mla_q_norm_g[l]), gkv=row(mla_kv_norm_g[l]),
        wo=(w_out[l, :NA_W].astype(bf), w_out[l, NA_W:NA_W + DA_W].astype(bf), w_out[l, NA_W + DA_W:].astype(bf)),
        gs=da_subln_g[l].reshape(DA_VD, 1),
    )


def kernel(x_prompt, x_sample, meta_tokens, norm_g, final_norm_g, ffn_w_gate, ffn_w_up, ffn_w_down, w_in, w_out, na_rel_bias, da_lambda, da_subln_g, mla_q_norm_g, mla_kv_norm_g, mla_w_uq, mla_w_ukv):
    xs = (x_prompt, x_sample)
    layout = _make_layout([(x.shape[0], x.shape[1]) for x in xs])
    real_rows = sum(g.batch * g.n for g in layout.groups)
    tail = []
    for g in layout.groups:
        blk = jnp.tile(meta_tokens.astype(jnp.float32), (g.batch, 1))
        tail.append(jnp.pad(blk, ((0, _round_up(g.batch * N_META, LANES) - g.batch * N_META), (0, 0))))
    tail = jnp.concatenate(tail, axis=0)
    tail = jnp.pad(tail, ((0, layout.rows - real_rows - tail.shape[0]), (0, 0)))
    srcs = [x.reshape(-1, D_MODEL) for x in xs] + [tail]
    real_bounds = tuple(g.real0 // TM for g in layout.groups) + (real_rows // TM,)
    bounds = real_bounds + (layout.rows // TM,)
    tabs, tab_block = _rope_tables(layout)
    gf = final_norm_g.reshape(1, -1)

    o_na = jnp.zeros((layout.rows, NA_W), jnp.bfloat16)
    o_da = jnp.zeros((layout.rows, DA_W), jnp.bfloat16)
    o_m = jnp.zeros((layout.rows, MLA_W), jnp.bfloat16)
    for l in range(DEPTH):
        p = _prep_layer(l, norm_g, ffn_w_gate, ffn_w_up, ffn_w_down, w_in, w_out, mla_q_norm_g, mla_kv_norm_g,
                        mla_w_uq, mla_w_ukv, da_subln_g)
        lam_init = 0.8 - 0.6 * math.exp(-0.3 * l)
        (h, naq, nak, nav, daq, dak, dav, mq, mk, mv) = _ffn_inproj(
            srcs, bounds, tabs, tab_block, p["g"][0], p["g"][1], *p["ffn"][0], p["win"], p["gq"], p["gkv"], p["uq"], p["ukv"])
        bias = _na_bias_tables(na_rel_bias[l])
        bmax = (jnp.max(jnp.abs(na_rel_bias[l])) * LOG2E).reshape(1, 1)
        o_na = _na_meta_call(layout, naq, nak, nav, o_na)
        for gi, grp in enumerate(layout.groups):
            o_na = _na_call(grp, naq, nak, nav, bias, bmax, o_na)
            for meta_q in (False, True):
                tag = f"g{gi}_{'meta' if meta_q else 'real'}"
                o_da = _flash_call(DA_CFG, grp, daq, dak, dav, (da_lambda[l], p["gs"]), o_da, lam_init, meta_q,
                                   "da_" + tag)
                o_m = _flash_call(MLA_CFG, grp, mq, mk, mv, (), o_m, lam_init, meta_q, "mla_" + tag)
        last = l == DEPTH - 1
        h = _outproj_ffn(h, o_na, o_da, o_m, *p["wo"], p["g"][2], *p["ffn"][1], gf, real_bounds if last else None)
        srcs, bounds = [h], (0, layout.rows // TM)

    return tuple(y.reshape(x.shape) for x, y in zip(xs, h))
```

```python
import functools
import math
from typing import NamedTuple

import jax
import jax.numpy as jnp
import numpy as np
from jax import lax
from jax.experimental import pallas as pl
from jax.experimental.pallas import tpu as pltpu

D_MODEL = 1024
DEPTH = 2
GRID_W = 64
N_META = 16
WIN_R = 8
WIN_C = 16
NA_HEADS = 6
NA_HD = 64
DA_HEADS = 6
DA_HD = 32
DA_VD = 64
DA_ROT = DA_HD // 4
MLA_HEADS = 4
MLA_NOPE = 64
MLA_ROPE = 32
MLA_VD = 64
Q_LORA = 256
KV_LORA = 128
ROPE_THETA = 500000.0
D_FF = 2816
EPS = 1e-6
NA_W = NA_HEADS * NA_HD
DA_W = DA_HEADS * DA_VD
MLA_W = MLA_HEADS * MLA_VD
DA_QK_W = DA_HEADS * 2 * DA_HD
IN_W = 3 * NA_W + 2 * DA_QK_W + DA_W + Q_LORA + KV_LORA + MLA_ROPE

LANES = 128
MXU_DIM = 256
VMEM_LIMIT_BYTES = 60 * 1024 * 1024

TM = 512
TK = 256
NA_QROWS = 4
NA_KROWS = NA_QROWS + WIN_R
LOOKAHEAD = 2
FF_SPLIT = 1536
IN_W_PAD = 2816

LOG2E = 1.4426950408889634
NEG = -1e30
ONES_ROWS = 16
SCORE_BOUND = 60.0
NORM_SLACK = 1.05


def _round_up(x, m):
    return (x + m - 1) // m * m


class Group(NamedTuple):
    batch: int
    n: int
    real0: int
    meta0: int


class Layout(NamedTuple):
    groups: tuple
    rows: int


def _make_layout(shapes):
    row = 0
    real0 = []
    for b, n in shapes:
        assert n % TM == 0 and row % n == 0 and n % GRID_W == 0
        assert (n // GRID_W) % NA_QROWS == 0 and n // GRID_W >= NA_KROWS
        real0.append(row)
        row += b * n
    groups = []
    for (b, n), r0 in zip(shapes, real0):
        groups.append(Group(b, n, r0, row))
        row += _round_up(b * N_META, LANES)
    return Layout(tuple(groups), _round_up(row, TM))


def _rms(x, g):
    return x * lax.rsqrt(jnp.mean(x * x, axis=-1, keepdims=True) + EPS) * g


def _swiglu_half(xn, wg_ref, wu_ref, wd_ref):
    acc = None
    for lo, hi in ((0, FF_SPLIT), (FF_SPLIT, D_FF)):
        gate = jnp.dot(xn, wg_ref[:, lo:hi], preferred_element_type=jnp.float32)
        up = jnp.dot(xn, wu_ref[:, lo:hi], preferred_element_type=jnp.float32)
        hm = (gate * jax.nn.sigmoid(gate) * up).astype(jnp.bfloat16)
        part = jnp.dot(hm, wd_ref[lo:hi, :], preferred_element_type=jnp.float32)
        acc = part if acc is None else acc + part
    return 0.5 * acc


def _rope_chunk(x, c, s, half, period):
    lane = lax.broadcasted_iota(jnp.int32, x.shape, 1)
    lo = (lane & (period - 1)) < half
    partner = jnp.where(lo, pltpu.roll(x, LANES - half, 1), pltpu.roll(x, half, 1))
    return x * c + partner * s


def _segment_specs(bounds):
    return [pl.BlockSpec((TM, D_MODEL), lambda i, lo=lo, hi=hi: (jnp.clip(i - lo, 0, hi - lo - 1), 0))
            for lo, hi in zip(bounds[:-1], bounds[1:])]


def _ffn_inproj_kernel(*refs, bounds):
    n_src = len(bounds) - 1
    srcs = refs[:n_src]
    (cda_ref, sda_ref, cm_ref, sm_ref, ga_ref, gb_ref,
     wg_ref, wu_ref, wd_ref, win_ref, gq_ref, gkv_ref, wuq_ref, wukv_ref,
     h1_ref, naq_ref, nak_ref, nav_ref, daq_ref, dak_ref, dav_ref,
     mq_ref, mk_ref, mv_ref, xn2_ref) = refs[n_src:]
    i = pl.program_id(0)

    @pl.when(i == 0)
    def _():
        xn2_ref[...] = jnp.zeros(xn2_ref.shape, xn2_ref.dtype)

    xn2 = xn2_ref[...]

    u = jnp.dot(xn2, win_ref[:, 0:3 * NA_W], preferred_element_type=jnp.float32)
    naq_ref[0] = (u[:, 0:NA_W] * (NA_HD ** -0.5 * LOG2E)).T.astype(jnp.bfloat16)
    nak_ref[...] = u[:, NA_W:2 * NA_W].astype(jnp.bfloat16)
    vt = u[:, 2 * NA_W:3 * NA_W].T.astype(jnp.bfloat16)
    for c in range(TM // LANES):
        nav_ref[c] = vt[:, c * LANES:(c + 1) * LANES]

    o0 = 3 * NA_W
    u = jnp.dot(xn2, win_ref[:, o0:o0 + 2 * DA_QK_W + DA_W], preferred_element_type=jnp.float32)
    cda, sda = cda_ref[...], sda_ref[...]
    q = jnp.concatenate([_rope_chunk(u[:, c * LANES:(c + 1) * LANES], cda, sda, DA_ROT // 2, DA_HD)
                         for c in range(DA_QK_W // LANES)], axis=1)
    k = jnp.concatenate([_rope_chunk(u[:, DA_QK_W + c * LANES:DA_QK_W + (c + 1) * LANES], cda, sda, DA_ROT // 2, DA_HD)
                         for c in range(DA_QK_W // LANES)], axis=1)
    daq_ref[0] = (q * (DA_HD ** -0.5 * LOG2E)).T.astype(jnp.bfloat16)
    dak_ref[...] = k.astype(jnp.bfloat16)
    vt = u[:, 2 * DA_QK_W:2 * DA_QK_W + DA_W].T.astype(jnp.bfloat16)
    for c in range(TM // TK):
        dav_ref[c] = vt[:, c * TK:(c + 1) * TK]

    o1 = o0 + 2 * DA_QK_W + DA_W
    u = jnp.dot(xn2, win_ref[:, o1:IN_W_PAD], preferred_element_type=jnp.float32)
    cm, sm = cm_ref[...], sm_ref[...]
    lane = lax.broadcasted_iota(jnp.int32, cm.shape, 1)
    band = (lane >= MLA_NOPE) & (lane < MLA_NOPE + MLA_ROPE)
    cb, sb = jnp.where(band, cm, 1.0), jnp.where(band, sm, 0.0)
    cq = _rms(u[:, 0:Q_LORA], gq_ref[...]).astype(jnp.bfloat16)
    qm = jnp.dot(cq, wuq_ref[...], preferred_element_type=jnp.float32)
    qm = jnp.concatenate([_rope_chunk(qm[:, h * LANES:(h + 1) * LANES], cb, sb, MLA_ROPE // 2, MLA_ROPE)
                          for h in range(MLA_HEADS)], axis=1) * ((MLA_NOPE + MLA_ROPE) ** -0.5 * LOG2E)
    mq_ref[0] = qm.T.astype(jnp.bfloat16)
    ckv = _rms(u[:, Q_LORA:Q_LORA + KV_LORA], gkv_ref[...]).astype(jnp.bfloat16)
    kv = jnp.dot(ckv, wukv_ref[...], preferred_element_type=jnp.float32)
    kr = _rope_chunk(u[:, Q_LORA + KV_LORA:Q_LORA + KV_LORA + LANES], cm, sm, MLA_ROPE // 2, MLA_ROPE)
    kr = jnp.where(band, kr, 0.0)
    kf = jnp.concatenate([kv[:, h * LANES:(h + 1) * LANES] + kr for h in range(MLA_HEADS)], axis=1)
    mk_ref[...] = kf.astype(jnp.bfloat16)
    vt = kv[:, MLA_HEADS * LANES:MLA_HEADS * LANES + MLA_W].T.astype(jnp.bfloat16)
    for c in range(TM // TK):
        mv_ref[c] = vt[:, c * TK:(c + 1) * TK]

    x = srcs[-1][...]
    for sgm in range(n_src - 2, -1, -1):
        x = jnp.where(i < bounds[sgm + 1], srcs[sgm][...], x)
    xn = _rms(x, ga_ref[...]).astype(jnp.bfloat16)
    h1 = x + _swiglu_half(xn, wg_ref, wu_ref, wd_ref)
    h1_ref[...] = h1
    xn2_ref[...] = _rms(h1, gb_ref[...]).astype(jnp.bfloat16)


def _outproj_ffn_kernel(h_ref, ona_ref, oda_ref, om_ref, wo1_ref, wo2_ref, wo3_ref, g_ref,
                        wg_ref, wu_ref, wd_ref, gf_ref, *o_refs, out_bounds):
    h = h_ref[...]
    h = h + jnp.dot(ona_ref[...], wo1_ref[...], preferred_element_type=jnp.float32)
    h = h + jnp.dot(oda_ref[...], wo2_ref[...], preferred_element_type=jnp.float32)
    h = h + jnp.dot(om_ref[...], wo3_ref[...], preferred_element_type=jnp.float32)
    xn = _rms(h, g_ref[...]).astype(jnp.bfloat16)
    h = h + _swiglu_half(xn, wg_ref, wu_ref, wd_ref)
    if out_bounds is None:
        o_refs[0][...] = h
    else:
        h = _rms(h, gf_ref[...])
        i = pl.program_id(0)
        for o_ref, lo, hi in zip(o_refs, out_bounds[:-1], out_bounds[1:]):
            @pl.when((i >= lo) & (i < hi))
            def _(o_ref=o_ref):
                o_ref[...] = h


def _const_spec(shape):
    nd = len(shape)
    return pl.BlockSpec(shape, lambda i: (0,) * nd, pipeline_mode=pl.Buffered(1))


def _ffn_inproj(srcs, bounds, tabs, tab_block, ga, gb, wg, wu, wd, win, gq, gkv, wuq, wukv):
    nt = bounds[-1]
    rows = nt * TM
    prev = lambda i: jnp.maximum(i - 1, 0)
    row_spec = lambda w: pl.BlockSpec((TM, w), lambda i: (prev(i), 0))
    tile_spec = lambda r: pl.BlockSpec((1, r, TM), lambda i: (prev(i), 0, 0))
    bf = jnp.bfloat16
    out_shape = (
        jax.ShapeDtypeStruct((rows, D_MODEL), jnp.float32),
        jax.ShapeDtypeStruct((nt, NA_W, TM), bf),
        jax.ShapeDtypeStruct((rows, NA_W), bf),
        jax.ShapeDtypeStruct((rows // LANES, NA_W, LANES), bf),
        jax.ShapeDtypeStruct((nt, DA_QK_W, TM), bf),
        jax.ShapeDtypeStruct((rows, DA_QK_W), bf),
        jax.ShapeDtypeStruct((rows // TK, DA_W, TK), bf),
        jax.ShapeDtypeStruct((nt, MLA_HEADS * LANES, TM), bf),
        jax.ShapeDtypeStruct((rows, 2 * MXU_DIM), bf),
        jax.ShapeDtypeStruct((rows // TK, MLA_W, TK), bf),
    )
    out_specs = (
        pl.BlockSpec((TM, D_MODEL), lambda i: (jnp.minimum(i, nt - 1), 0)),
        tile_spec(NA_W),
        row_spec(NA_W),
        pl.BlockSpec((TM // LANES, NA_W, LANES), lambda i: (prev(i), 0, 0)),
        tile_spec(DA_QK_W),
        row_spec(DA_QK_W),
        pl.BlockSpec((TM // TK, DA_W, TK), lambda i: (prev(i), 0, 0)),
        tile_spec(MLA_HEADS * LANES),
        row_spec(2 * MXU_DIM),
        pl.BlockSpec((TM // TK, MLA_W, TK), lambda i: (prev(i), 0, 0)),
    )
    in_specs = _segment_specs(bounds) + [pl.BlockSpec((TM, LANES), lambda i: (tab_block(prev(i)), 0))] * 4 + [
        _const_spec(a.shape) for a in (ga, gb, wg, wu, wd, win, gq, gkv, wuq, wukv)]
    return pl.pallas_call(
        functools.partial(_ffn_inproj_kernel, bounds=bounds),
        grid=(nt + 1,),
        in_specs=in_specs,
        out_specs=out_specs,
        out_shape=out_shape,
        scratch_shapes=[pltpu.VMEM((TM, D_MODEL), jnp.bfloat16)],
        compiler_params=pltpu.CompilerParams(dimension_semantics=("arbitrary",),
                                             vmem_limit_bytes=VMEM_LIMIT_BYTES),
        name="ffn_inproj",
    )(*srcs, *tabs, ga, gb, wg, wu, wd, win, gq, gkv, wuq, wukv)


def _outproj_ffn(h, ona, oda, om, wo1, wo2, wo3, g, wg, wu, wd, gf, out_bounds):
    rows = h.shape[0]
    row_spec = lambda w: pl.BlockSpec((TM, w), lambda i: (i, 0))
    if out_bounds is None:
        out_specs = row_spec(D_MODEL)
        out_shape = jax.ShapeDtypeStruct((rows, D_MODEL), jnp.float32)
    else:
        out_specs = tuple(_segment_specs(out_bounds))
        out_shape = tuple(jax.ShapeDtypeStruct(((hi - lo) * TM, D_MODEL), jnp.float32)
                          for lo, hi in zip(out_bounds[:-1], out_bounds[1:]))
    in_specs = [row_spec(D_MODEL), row_spec(NA_W), row_spec(DA_W), row_spec(MLA_W)] + [
        _const_spec(a.shape) for a in (wo1, wo2, wo3, g, wg, wu, wd, gf)]
    return pl.pallas_call(
        functools.partial(_outproj_ffn_kernel, out_bounds=out_bounds),
        grid=(rows // TM,),
        in_specs=in_specs,
        out_specs=out_specs,
        out_shape=out_shape,
        compiler_params=pltpu.CompilerParams(dimension_semantics=("arbitrary",),
                                             vmem_limit_bytes=VMEM_LIMIT_BYTES),
        name="outproj_ffn",
    )(h, ona, oda, om, wo1, wo2, wo3, g, wg, wu, wd, gf)


def _row_band(block, lo, hi):
    row = lax.broadcasted_iota(jnp.int32, block.shape, 0)
    return jnp.where((row >= lo) & (row < hi), block, jnp.zeros_like(block))


def _with_ones(vt):
    return jnp.concatenate([vt, jnp.ones((ONES_ROWS, vt.shape[1]), vt.dtype)], axis=0)


def _meta_key_mask(b):
    row = lax.broadcasted_iota(jnp.int32, (LANES, 1), 0)
    lo = (b % (LANES // N_META)) * N_META
    return jnp.where((row >= lo) & (row < lo + N_META), 0.0, NEG).astype(jnp.float32)


class FlashCfg(NamedTuple):
    n_soft: int
    group_w: int
    soft_per_group: int
    v_of_soft: tuple
    out_w: int
    unroll: int


DA_CFG = FlashCfg(2 * DA_HEADS, LANES, 4, tuple(i // 2 for i in range(2 * DA_HEADS)), DA_W, 16)
MLA_CFG = FlashCfg(MLA_HEADS, LANES, 1, tuple(range(MLA_HEADS)), MLA_W, 32)
NA_CFG = FlashCfg(NA_HEADS, LANES, 2, tuple(range(NA_HEADS)), NA_W, 1)


def _build_rhs(cfg, qt_ref, rhs_ref):
    for i in range(cfg.n_soft):
        g, j = i // cfg.soft_per_group, i % cfg.soft_per_group
        if cfg is DA_CFG:
            rhs_ref[i] = _row_band(qt_ref[0, g * LANES:(g + 1) * LANES, :], j * DA_HD, (j + 1) * DA_HD)
        else:
            rhs_ref[i] = qt_ref[0, i * LANES:(i + 1) * LANES, :]


def _flash_tiles(cfg, tiles, rhs_ref, m_ref, acc_ref, bounded):
    nt = len(tiles)
    if bounded:
        work = [(t, i) for i in range(cfg.n_soft) for t in range(nt)]
    else:
        work = [(t, i) for t in range(nt) for i in range(cfg.n_soft)]

    def score(w):
        kblk_of, _, mask = tiles[w[0]]
        s = jnp.dot(kblk_of(w[1] // cfg.soft_per_group), rhs_ref[w[1]], preferred_element_type=jnp.float32)
        return s if mask is None else s + mask

    pending = [score(w) for w in work[:LOOKAHEAD]]
    part = den = None
    for n, (t, i) in enumerate(work):
        vt = tiles[t][1](cfg.v_of_soft[i])
        s = pending.pop(0)
        if n + LOOKAHEAD < len(work):
            pending.append(score(work[n + LOOKAHEAD]))
        if bounded:
            p = jnp.exp2(s)
            pv = jnp.dot(vt, p.astype(jnp.bfloat16), preferred_element_type=jnp.float32)
            ps = jnp.sum(p, axis=0, keepdims=True)
            part, den = (pv, ps) if t == 0 else (part + pv, den + ps)
            if t == nt - 1:
                acc_ref[i, 0:DA_VD, :] = acc_ref[i, 0:DA_VD, :] + part
                acc_ref[i, DA_VD:DA_VD + 1, :] = acc_ref[i, DA_VD:DA_VD + 1, :] + den
        else:
            vt = _with_ones(vt)
            m_prev = m_ref[i]
            m_new = jnp.maximum(m_prev, jnp.max(s, axis=0, keepdims=True))
            alpha = jnp.exp2(m_prev - m_new)
            p = jnp.exp2(s - m_new).astype(jnp.bfloat16)
            acc_ref[i] = alpha * acc_ref[i] + jnp.dot(vt, p, preferred_element_type=jnp.float32)
            m_ref[i] = m_new


def _feature_indicator(cfg):
    f = lax.broadcasted_iota(jnp.int32, (cfg.group_w, LANES), 0)
    j = lax.broadcasted_iota(jnp.int32, (cfg.group_w, LANES), 1)
    band = cfg.group_w // cfg.soft_per_group
    hit = jnp.right_shift(f, int(math.log2(band))) == j
    return jnp.where(hit & (j < cfg.soft_per_group), 1.0, 0.0).astype(jnp.bfloat16)


def _max_key_norms(cfg, kf_ref, kfm_ref, n_keys, kn_ref):
    gw = cfg.group_w
    n_groups = cfg.n_soft // cfg.soft_per_group
    ind = _feature_indicator(cfg)

    def sq_norms(blk):
        x = blk.astype(jnp.float32)
        n2 = jnp.dot((x * x).astype(jnp.bfloat16), ind, preferred_element_type=jnp.float32)
        return jnp.max(n2, axis=0, keepdims=True)

    def body(c, carry):
        r0 = pl.multiple_of(c * TM, TM)
        return tuple(jnp.maximum(carry[g], sq_norms(kf_ref[pl.ds(r0, TM), g * gw:(g + 1) * gw]))
                     for g in range(n_groups))

    init = tuple(sq_norms(kfm_ref[:, g * gw:(g + 1) * gw]) for g in range(n_groups))
    kmax = lax.fori_loop(0, n_keys // TM, body, init)
    lane = lax.broadcasted_iota(jnp.int32, (1, LANES), 1)
    for i in range(cfg.n_soft):
        g, j = divmod(i, cfg.soft_per_group)
        v = jnp.max(jnp.where(lane == j, kmax[g], 0.0), axis=1, keepdims=True)
        kn_ref[i] = jnp.broadcast_to(v, (1, LANES))


def _scores_are_bounded(cfg, rhs_ref, kn_ref):
    ok = None
    for i in range(cfg.n_soft):
        r = rhs_ref[i].astype(jnp.float32)
        bound2 = jnp.sum(r * r, axis=0, keepdims=True) * kn_ref[i][:, 0:1] * NORM_SLACK
        good = bound2 <= SCORE_BOUND * SCORE_BOUND
        ok = good if ok is None else ok & good
    return jnp.min(jnp.where(ok, 1.0, 0.0)) > 0.5


def _flash_kernel(*refs, cfg, n_keys, lam_init, meta_queries):
    if cfg is DA_CFG:
        (qt_ref, kf_ref, vt_ref, kfm_ref, vtm_ref, lam_ref, gs_ref, o_ref,
         rhs_ref, m_ref, acc_ref, kn_ref, *rest) = refs
    else:
        qt_ref, kf_ref, vt_ref, kfm_ref, vtm_ref, o_ref, rhs_ref, m_ref, acc_ref, kn_ref, *rest = refs
    b = pl.program_id(0)
    tq = qt_ref.shape[2]
    vd = DA_VD
    gw = cfg.group_w
    if meta_queries:
        _max_key_norms(cfg, kf_ref, kfm_ref, n_keys, kn_ref)
    else:
        @pl.when(pl.program_id(1) == 0)
        def _():
            _max_key_norms(cfg, kf_ref, kfm_ref, n_keys, kn_ref)
    _build_rhs(cfg, qt_ref, rhs_ref)
    m_ref[...] = jnp.full(m_ref.shape, NEG, jnp.float32)
    acc_ref[...] = jnp.zeros(acc_ref.shape, jnp.float32)

    def all_keys(bounded, unroll):
        _flash_tiles(
            cfg,
            [(lambda g: kfm_ref[:, g * gw:(g + 1) * gw],
              lambda h: vtm_ref[0, h * vd:(h + 1) * vd, :],
              _meta_key_mask(b))],
            rhs_ref, m_ref, acc_ref, bounded)

        def body(it, carry):
            tiles = []
            for u in range(unroll):
                kt = it * unroll + u
                k0 = pl.multiple_of(kt * TK, TK)
                tiles.append((
                    lambda g, k0=k0: kf_ref[pl.ds(k0, TK), g * gw:(g + 1) * gw],
                    lambda h, kt=kt: vt_ref[kt, h * vd:(h + 1) * vd, :],
                    None))
            _flash_tiles(cfg, tiles, rhs_ref, m_ref, acc_ref, bounded)
            return carry

        lax.fori_loop(0, n_keys // (TK * unroll), body, 0)

    lax.cond(_scores_are_bounded(cfg, rhs_ref, kn_ref),
             lambda: all_keys(True, math.gcd(cfg.unroll, n_keys // TK)),
             lambda: all_keys(False, 1))

    heads = []
    if cfg is DA_CFG:
        lp = lam_ref[...]
        lam = (jnp.exp(jnp.sum(lp[0:1] * lp[1:2], axis=1, keepdims=True))
               - jnp.exp(jnp.sum(lp[2:3] * lp[3:4], axis=1, keepdims=True)) + lam_init)
        for h in range(DA_HEADS):
            a1, a2 = acc_ref[2 * h], acc_ref[2 * h + 1]
            o = a1[0:vd] / a1[vd:vd + 1] - lam * (a2[0:vd] / a2[vd:vd + 1])
            o = o * lax.rsqrt(jnp.mean(o * o, axis=0, keepdims=True) + EPS) * gs_ref[...]
            heads.append(o * (1.0 - lam_init))
    else:
        for h in range(MLA_HEADS):
            a = acc_ref[h]
            heads.append(a[0:vd] / a[vd:vd + 1])
    pairs = [jnp.concatenate(heads[2 * g:2 * g + 2], axis=0).T for g in range(len(heads) // 2)]
    out = jnp.concatenate(pairs, axis=1)
    if meta_queries:
        stage_ref = rest[0]
        stage_ref[...] = out
        r0 = pl.multiple_of((b % (LANES // N_META)) * N_META, N_META)
        o_ref[...] = stage_ref[pl.ds(r0, N_META), :].astype(o_ref.dtype)
    else:
        o_ref[...] = out.astype(o_ref.dtype)


def _flash_call(cfg, grp, qt, kf, vt, extra, o_prev, lam_init, meta_queries, name):
    n, bsz = grp.n, grp.batch
    qw, kw, vw = qt.shape[1], kf.shape[1], vt.shape[1]
    tiles_per_seq = n // TM
    meta_blk0 = grp.meta0 // LANES
    per_blk = LANES // N_META
    if meta_queries:
        grid = (bsz,)
        tq = LANES
        q_map = lambda b: ((meta_blk0 + b // per_blk) // (TM // LANES), 0, (meta_blk0 + b // per_blk) % (TM // LANES))
        o_spec = pl.BlockSpec((N_META, cfg.out_w), lambda b: (grp.meta0 // N_META + b, 0))
        fix = lambda f: (lambda b: f(b))
        sem = ("arbitrary",)
    else:
        grid = (bsz, tiles_per_seq)
        tq = TM
        q_map = lambda b, i: (grp.real0 // TM + b * tiles_per_seq + i, 0, 0)
        o_spec = pl.BlockSpec((TM, cfg.out_w), lambda b, i: (grp.real0 // TM + b * tiles_per_seq + i, 0))
        fix = lambda f: (lambda b, i: f(b))
        sem = ("arbitrary", "arbitrary")
    in_specs = [
        pl.BlockSpec((1, qw, tq), q_map),
        pl.BlockSpec((n, kw), fix(lambda b: (grp.real0 // n + b, 0))),
        pl.BlockSpec((n // TK, vw, TK), fix(lambda b: (grp.real0 // n + b, 0, 0))),
        pl.BlockSpec((LANES, kw), fix(lambda b: (meta_blk0 + b // per_blk, 0))),
        pl.BlockSpec((1, vw, LANES), fix(lambda b: ((meta_blk0 + b // per_blk) // (TK // LANES), 0,
                                                     (meta_blk0 + b // per_blk) % (TK // LANES)))),
    ]
    args = [qt, kf, vt, kf, vt]
    for a in extra:
        in_specs.append(pl.BlockSpec(a.shape, fix(lambda b, nd=a.ndim: (0,) * nd)))
        args.append(a)
    in_specs.append(pl.BlockSpec(memory_space=pl.ANY))
    args.append(o_prev)
    scratch = [
        pltpu.VMEM((cfg.n_soft, cfg.group_w, tq), jnp.bfloat16),
        pltpu.VMEM((cfg.n_soft, 1, tq), jnp.float32),
        pltpu.VMEM((cfg.n_soft, DA_VD + ONES_ROWS, tq), jnp.float32),
        pltpu.VMEM((cfg.n_soft, 1, LANES), jnp.float32),
    ]
    if meta_queries:
        scratch.append(pltpu.VMEM((LANES, cfg.out_w), jnp.float32))

    def body(*refs):
        n_in = len(args)
        ins, rest = refs[:n_in - 1], refs[n_in:]
        _flash_kernel(*ins, *rest, cfg=cfg, n_keys=n, lam_init=lam_init, meta_queries=meta_queries)

    return pl.pallas_call(
        body,
        grid=grid,
        in_specs=in_specs,
        out_specs=o_spec,
        out_shape=jax.ShapeDtypeStruct(o_prev.shape, o_prev.dtype),
        scratch_shapes=scratch,
        input_output_aliases={len(args) - 1: 0},
        compiler_params=pltpu.CompilerParams(dimension_semantics=sem, vmem_limit_bytes=VMEM_LIMIT_BYTES),
        name=name,
    )(*args)


def _na_kernel(qt_ref, kf_ref, vt_ref, kfm_ref, vtm_ref, bias_ref, bmax_ref, prev_ref, o_ref, kn_ref, *, grid_rows):
    del prev_ref
    b, i = pl.program_id(0), pl.program_id(1)

    @pl.when(i == 0)
    def _():
        _max_key_norms(NA_CFG, kf_ref, kfm_ref, grid_rows * GRID_W, kn_ref)

    ws = jnp.clip(i * NA_QROWS - WIN_R // 2, 0, grid_rows - NA_KROWS)
    kwin = kf_ref[pl.ds(pl.multiple_of(ws * GRID_W, 2 * GRID_W), NA_KROWS * GRID_W), :]
    vwin = vt_ref[pl.ds(ws // 2, NA_KROWS // 2)]
    kmeta = kfm_ref[...]
    mmask = _meta_key_mask(b)

    def rhs_of(h):
        g, j = h // 2, h % 2
        return _row_band(qt_ref[0, g * LANES:(g + 1) * LANES, :], j * NA_HD, (j + 1) * NA_HD)

    def scores(h):
        g, rhs = h // 2, rhs_of(h)
        s1 = jnp.dot(kwin[:, g * LANES:(g + 1) * LANES], rhs, preferred_element_type=jnp.float32) + bias_ref[0, h]
        s2 = jnp.dot(kmeta[:, g * LANES:(g + 1) * LANES], rhs, preferred_element_type=jnp.float32) + mmask
        return s1, s2

    def run(bounded):
        heads = []
        pending = [scores(h) for h in range(LOOKAHEAD)]
        for h in range(NA_HEADS):
            s1, s2 = pending.pop(0)
            if h + LOOKAHEAD < NA_HEADS:
                pending.append(scores(h + LOOKAHEAD))
            if not bounded:
                m = jnp.maximum(jnp.max(s1, axis=0, keepdims=True), jnp.max(s2, axis=0, keepdims=True))
                s1, s2 = s1 - m, s2 - m
            p1 = jnp.exp2(s1).astype(jnp.bfloat16)
            p2 = jnp.exp2(s2).astype(jnp.bfloat16)
            v1 = jnp.concatenate([vwin[c, h * NA_HD:(h + 1) * NA_HD, :] for c in range(NA_KROWS // 2)], axis=1)
            acc = (jnp.dot(_with_ones(v1), p1, preferred_element_type=jnp.float32)
                   + jnp.dot(_with_ones(vtm_ref[0, h * NA_HD:(h + 1) * NA_HD, :]), p2,
                             preferred_element_type=jnp.float32))
            heads.append(acc[0:NA_HD] / acc[NA_HD:NA_HD + 1])
        pairs = [jnp.concatenate(heads[2 * g:2 * g + 2], axis=0).T for g in range(NA_HEADS // 2)]
        o_ref[...] = jnp.concatenate(pairs, axis=1).astype(o_ref.dtype)

    limit = SCORE_BOUND - bmax_ref[...]
    ok = None
    for h in range(NA_HEADS):
        r = rhs_of(h).astype(jnp.float32)
        bound2 = jnp.sum(r * r, axis=0, keepdims=True) * kn_ref[h][:, 0:1] * NORM_SLACK
        good = (limit > 0.0) & (bound2 <= limit * limit)
        ok = good if ok is None else ok & good
    lax.cond(jnp.min(jnp.where(ok, 1.0, 0.0)) > 0.5, lambda: run(True), lambda: run(False))


def _na_meta_kernel(qt_ref, kf_ref, vt_ref, prev_ref, o_ref):
    del prev_ref
    kr = lax.broadcasted_iota(jnp.int32, (LANES, LANES), 0) // N_META
    qc = lax.broadcasted_iota(jnp.int32, (LANES, LANES), 1) // N_META
    mask = jnp.where(kr == qc, 0.0, NEG).astype(jnp.float32)
    kf = kf_ref[...]
    heads = []
    for h in range(NA_HEADS):
        g, j = h // 2, h % 2
        rhs = _row_band(qt_ref[0, g * LANES:(g + 1) * LANES, :], j * NA_HD, (j + 1) * NA_HD)
        s = jnp.dot(kf[:, g * LANES:(g + 1) * LANES], rhs, preferred_element_type=jnp.float32) + mask
        p = jnp.exp2(s - jnp.max(s, axis=0, keepdims=True)).astype(jnp.bfloat16)
        acc = jnp.dot(_with_ones(vt_ref[0, h * NA_HD:(h + 1) * NA_HD, :]), p, preferred_element_type=jnp.float32)
        heads.append(acc[0:NA_HD] / acc[NA_HD:NA_HD + 1])
    pairs = [jnp.concatenate(heads[2 * g:2 * g + 2], axis=0).T for g in range(NA_HEADS // 2)]
    o_ref[...] = jnp.concatenate(pairs, axis=1).astype(o_ref.dtype)


def _na_call(grp, qt, kf, vt, bias, bmax, o_prev):
    n, bsz = grp.n, grp.batch
    grid_rows = n // GRID_W
    steps = grid_rows // NA_QROWS
    tq = NA_QROWS * GRID_W
    meta_blk0 = grp.meta0 // LANES
    per_blk = LANES // N_META
    in_specs = [
        pl.BlockSpec((1, NA_W, tq), lambda b, i: (grp.real0 // TM + b * (n // TM) + i // (TM // tq), 0, i % (TM // tq))),
        pl.BlockSpec((n, NA_W), lambda b, i: (grp.real0 // n + b, 0)),
        pl.BlockSpec((n // LANES, NA_W, LANES), lambda b, i: (grp.real0 // n + b, 0, 0)),
        pl.BlockSpec((LANES, NA_W), lambda b, i: (meta_blk0 + b // per_blk, 0)),
        pl.BlockSpec((1, NA_W, LANES), lambda b, i: (meta_blk0 + b // per_blk, 0, 0)),
        pl.BlockSpec((1, NA_HEADS, NA_KROWS * GRID_W, tq),
                     lambda b, i: (jnp.where(i == 0, 0, jnp.where(i == steps - 1, 2, 1)), 0, 0, 0)),
        pl.BlockSpec((1, 1), lambda b, i: (0, 0)),
        pl.BlockSpec(memory_space=pl.ANY),
    ]
    return pl.pallas_call(
        functools.partial(_na_kernel, grid_rows=grid_rows),
        grid=(bsz, steps),
        in_specs=in_specs,
        out_specs=pl.BlockSpec((tq, NA_W), lambda b, i: (grp.real0 // tq + b * steps + i, 0)),
        out_shape=jax.ShapeDtypeStruct(o_prev.shape, o_prev.dtype),
        scratch_shapes=[pltpu.VMEM((NA_HEADS, 1, LANES), jnp.float32)],
        input_output_aliases={7: 0},
        compiler_params=pltpu.CompilerParams(dimension_semantics=("arbitrary", "arbitrary"),
                                             vmem_limit_bytes=VMEM_LIMIT_BYTES),
        name="na_real",
    )(qt, kf, vt, kf, vt, bias, bmax, o_prev)


def _na_meta_call(layout, qt, kf, vt, o_prev):
    blk0 = layout.groups[0].meta0 // LANES
    nblk = sum(_round_up(g.batch * N_META, LANES) for g in layout.groups) // LANES
    sub = TM // LANES
    return pl.pallas_call(
        _na_meta_kernel,
        grid=(nblk,),
        in_specs=[
            pl.BlockSpec((1, NA_W, LANES), lambda i: ((blk0 + i) // sub, 0, (blk0 + i) % sub)),
            pl.BlockSpec((LANES, NA_W), lambda i: (blk0 + i, 0)),
            pl.BlockSpec((1, NA_W, LANES), lambda i: (blk0 + i, 0, 0)),
            pl.BlockSpec(memory_space=pl.ANY),
        ],
        out_specs=pl.BlockSpec((LANES, NA_W), lambda i: (blk0 + i, 0)),
        out_shape=jax.ShapeDtypeStruct(o_prev.shape, o_prev.dtype),
        input_output_aliases={3: 0},
        compiler_params=pltpu.CompilerParams(dimension_semantics=("arbitrary",),
                                             vmem_limit_bytes=VMEM_LIMIT_BYTES),
        name="na_meta",
    )(qt, kf, vt, o_prev)


def _na_bias_tables(rel_bias):
    kc = np.arange(GRID_W)[:, None]
    qc = np.arange(GRID_W)[None, :]
    cs = np.clip(qc - WIN_C // 2, 0, GRID_W - WIN_C)
    col_ok = (kc >= cs) & (kc < cs + WIN_C)
    cidx = np.clip(kc - qc + WIN_C - 1, 0, 2 * WIN_C - 2)
    planes = jnp.where(col_ok[None, None], rel_bias[:, :, cidx] * LOG2E, NEG)
    masked = 2 * WIN_R - 1
    planes = jnp.concatenate([planes, jnp.full((NA_HEADS, 1, GRID_W, GRID_W), NEG, planes.dtype)], axis=1)
    kr = np.arange(NA_KROWS)[:, None]
    qr = np.arange(NA_QROWS)[None, :]
    ridx = []
    for win_lo, q_off in ((0 * qr, qr), (qr, qr + WIN_R // 2), (0 * qr + NA_KROWS - WIN_R, qr + NA_KROWS - NA_QROWS)):
        row_ok = (kr >= win_lo) & (kr < win_lo + WIN_R)
        ridx.append(np.where(row_ok, kr - q_off + WIN_R - 1, masked))
    ridx = np.stack(ridx)
    assert ridx.min() >= 0 and ridx.max() <= masked
    t = planes[:, ridx]
    t = jnp.transpose(t, (1, 0, 2, 4, 3, 5))
    return t.reshape(3, NA_HEADS, NA_KROWS * GRID_W, NA_QROWS * GRID_W).astype(jnp.float32)


def _rope_tables(layout):
    pos, starts, periods = [], [], []
    for g in layout.groups:
        starts.append(len(pos))
        periods.append(g.n // TM)
        for t in range(g.n // TM):
            pos.append(N_META + t * TM + np.arange(TM))
    tail0 = sum(g.batch * g.n for g in layout.groups)
    starts.append(len(pos))
    for t in range(tail0 // TM, layout.rows // TM):
        r = t * TM + np.arange(TM)
        p = np.zeros((TM,), np.int64)
        for g in layout.groups:
            inside = (r >= g.meta0) & (r < g.meta0 + g.batch * N_META)
            p = np.where(inside, (r - g.meta0) % N_META, p)
        pos.append(p)
    pos = jnp.asarray(np.concatenate(pos).astype(np.float32))
    bounds = [g.real0 // TM for g in layout.groups] + [tail0 // TM]

    def block_of_tile(i):
        blk = starts[-1] + (i - bounds[-1])
        for k in range(len(layout.groups) - 1, -1, -1):
            blk = jnp.where(i < bounds[k + 1], starts[k] + (i - bounds[k]) % periods[k], blk)
        return blk

    def tables(dim, period):
        inv = ROPE_THETA ** (-(jnp.arange(0, dim, 2, dtype=jnp.float32) / dim))
        ang = pos[:, None] * inv[None, :]
        cos, sin = jnp.cos(ang), jnp.sin(ang)
        rest = period - dim
        c = jnp.concatenate([cos, cos, jnp.ones((pos.shape[0], rest), jnp.float32)], axis=1)
        s = jnp.concatenate([-sin, sin, jnp.zeros((pos.shape[0], rest), jnp.float32)], axis=1)
        reps = LANES // period
        return jnp.tile(c, (1, reps)), jnp.tile(s, (1, reps))

    cda, sda = tables(DA_ROT, DA_HD)
    cm, sm = tables(MLA_ROPE, MLA_ROPE)
    return (cda, sda, cm, sm), block_of_tile


def _prep_layer(l, norm_g, ffn_w_gate, ffn_w_up, ffn_w_down, w_in, w_out, mla_q_norm_g, mla_kv_norm_g,
                mla_w_uq, mla_w_ukv, da_subln_g):
    bf = jnp.bfloat16
    kr0 = IN_W - MLA_ROPE
    win = jnp.concatenate([w_in[l, :, :kr0]] + [w_in[l, :, kr0:]] * (LANES // MLA_ROPE), axis=1).astype(bf)
    uq = mla_w_uq[l].reshape(Q_LORA, MLA_HEADS, MLA_NOPE + MLA_ROPE)
    uq = jnp.pad(uq, ((0, 0), (0, 0), (0, LANES - MLA_NOPE - MLA_ROPE))).reshape(Q_LORA, MLA_HEADS * LANES)
    ukv = mla_w_ukv[l].reshape(KV_LORA, MLA_HEADS, MLA_NOPE + MLA_VD)
    uk = jnp.pad(ukv[:, :, :MLA_NOPE], ((0, 0), (0, 0), (0, LANES - MLA_NOPE))).reshape(KV_LORA, MLA_HEADS * LANES)
    ukv = jnp.concatenate([uk, ukv[:, :, MLA_NOPE:].reshape(KV_LORA, -1)], axis=1)
    row = lambda v: v.reshape(1, -1)
    return dict(
        g=[row(norm_g[l, i]) for i in range(3)],
        ffn=[(ffn_w_gate[l, i].astype(bf), ffn_w_up[l, i].astype(bf), ffn_w_down[l, i].astype(bf)) for i in range(2)],
        win=win, uq=uq.astype(bf), ukv=ukv.astype(bf),
        gq=row(mla_q_norm_g[l]), gkv=row(mla_kv_norm_g[l]),
        wo=(w_out[l, :NA_W].astype(bf), w_out[l, NA_W:NA_W + DA_W].astype(bf), w_out[l, NA_W + DA_W:].astype(bf)),
        gs=da_subln_g[l].reshape(DA_VD, 1),
    )


def kernel(x_prompt, x_sample, meta_tokens, norm_g, final_norm_g, ffn_w_gate, ffn_w_up, ffn_w_down, w_in, w_out, na_rel_bias, da_lambda, da_subln_g, mla_q_norm_g, mla_kv_norm_g, mla_w_uq, mla_w_ukv):
    xs = (x_prompt, x_sample)
    layout = _make_layout([(x.shape[0], x.shape[1]) for x in xs])
    real_rows = sum(g.batch * g.n for g in layout.groups)
    tail = []
    for g in layout.groups:
        blk = jnp.tile(meta_tokens.astype(jnp.float32), (g.batch, 1))
        tail.append(jnp.pad(blk, ((0, _round_up(g.batch * N_META, LANES) - g.batch * N_META), (0, 0))))
    tail = jnp.concatenate(tail, axis=0)
    tail = jnp.pad(tail, ((0, layout.rows - real_rows - tail.shape[0]), (0, 0)))
    srcs = [x.reshape(-1, D_MODEL) for x in xs] + [tail]
    real_bounds = tuple(g.real0 // TM for g in layout.groups) + (real_rows // TM,)
    bounds = real_bounds + (layout.rows // TM,)
    tabs, tab_block = _rope_tables(layout)
    gf = final_norm_g.reshape(1, -1)

    o_na = jnp.zeros((layout.rows, NA_W), jnp.bfloat16)
    o_da = jnp.zeros((layout.rows, DA_W), jnp.bfloat16)
    o_m = jnp.zeros((layout.rows, MLA_W), jnp.bfloat16)
    for l in range(DEPTH):
        p = _prep_layer(l, norm_g, ffn_w_gate, ffn_w_up, ffn_w_down, w_in, w_out, mla_q_norm_g, mla_kv_norm_g,
                        mla_w_uq, mla_w_ukv, da_subln_g)
        lam_init = 0.8 - 0.6 * math.exp(-0.3 * l)
        (h, naq, nak, nav, daq, dak, dav, mq, mk, mv) = _ffn_inproj(
            srcs, bounds, tabs, tab_block, p["g"][0], p["g"][1], *p["ffn"][0], p["win"], p["gq"], p["gkv"], p["uq"], p["ukv"])
        bias = _na_bias_tables(na_rel_bias[l])
        bmax = (jnp.max(jnp.abs(na_rel_bias[l])) * LOG2E).reshape(1, 1)
        o_na = _na_meta_call(layout, naq, nak, nav, o_na)
        for gi, grp in enumerate(layout.groups):
            o_na = _na_call(grp, naq, nak, nav, bias, bmax, o_na)
            for meta_q in (False, True):
                tag = f"g{gi}_{'meta' if meta_q else 'real'}"
                o_da = _flash_call(DA_CFG, grp, daq, dak, dav, (da_lambda[l], p["gs"]), o_da, lam_init, meta_q,
                                   "da_" + tag)
                o_m = _flash_call(MLA_CFG, grp, mq, mk, mv, (), o_m, lam_init, meta_q, "mla_" + tag)
        last = l == DEPTH - 1
        h = _outproj_ffn(h, o_na, o_da, o_m, *p["wo"], p["g"][2], *p["ffn"][1], gf, real_bounds if last else None)
        srcs, bounds = [h], (0, layout.rows // TM)

    return tuple(y.reshape(x.shape) for x, y in zip(xs, h))
```

```python
import functools
import math
from typing import NamedTuple

import jax
import jax.numpy as jnp
import numpy as np
from jax import lax
from jax.experimental import pallas as pl
from jax.experimental.pallas import tpu as pltpu

D_MODEL = 1024
DEPTH = 2
GRID_W = 64
N_META = 16
WIN_R = 8
WIN_C = 16
NA_HEADS = 6
NA_HD = 64
DA_HEADS = 6
DA_HD = 32
DA_VD = 64
DA_ROT = DA_HD // 4
MLA_HEADS = 4
MLA_NOPE = 64
MLA_ROPE = 32
MLA_VD = 64
Q_LORA = 256
KV_LORA = 128
ROPE_THETA = 500000.0
D_FF = 2816
EPS = 1e-6
NA_W = NA_HEADS * NA_HD
DA_W = DA_HEADS * DA_VD
MLA_W = MLA_HEADS * MLA_VD
DA_QK_W = DA_HEADS * 2 * DA_HD
IN_W = 3 * NA_W + 2 * DA_QK_W + DA_W + Q_LORA + KV_LORA + MLA_ROPE

LANES = 128
MXU_DIM = 256
VMEM_LIMIT_BYTES = 60 * 1024 * 1024

TM = 512
TK = 256
NA_QROWS = 4
NA_KROWS = NA_QROWS + WIN_R
LOOKAHEAD = 2
FF_SPLIT = 1536
IN_W_PAD = IN_W - MLA_ROPE + LANES
MLA_QK_W = MLA_HEADS * LANES
assert MLA_VD == DA_VD and IN_W_PAD % LANES == 0

LOG2E = 1.4426950408889634
NEG = -1e30
ONES_ROWS = 16
SCORE_BOUND = 60.0
NORM_SLACK = 1.05


def _round_up(x, m):
    return (x + m - 1) // m * m


class Group(NamedTuple):
    batch: int
    n: int
    real0: int
    meta0: int


class Layout(NamedTuple):
    groups: tuple
    rows: int


def _make_layout(shapes):
    row = 0
    real0 = []
    for b, n in shapes:
        assert n % TM == 0 and row % n == 0 and n % GRID_W == 0
        assert (n // GRID_W) % NA_QROWS == 0 and n // GRID_W >= NA_KROWS
        real0.append(row)
        row += b * n
    groups = []
    for (b, n), r0 in zip(shapes, real0):
        groups.append(Group(b, n, r0, row))
        row += _round_up(b * N_META, LANES)
    return Layout(tuple(groups), _round_up(row, TM))


def _rms(x, g):
    return x * lax.rsqrt(jnp.mean(x * x, axis=-1, keepdims=True) + EPS) * g


def _swiglu_half(xn, wg_ref, wu_ref, wd_ref):
    acc = None
    for lo, hi in ((0, FF_SPLIT), (FF_SPLIT, D_FF)):
        gate = jnp.dot(xn, wg_ref[:, lo:hi], preferred_element_type=jnp.float32)
        up = jnp.dot(xn, wu_ref[:, lo:hi], preferred_element_type=jnp.float32)
        hm = (gate * jax.nn.sigmoid(gate) * up).astype(jnp.bfloat16)
        part = jnp.dot(hm, wd_ref[lo:hi, :], preferred_element_type=jnp.float32)
        acc = part if acc is None else acc + part
    return 0.5 * acc


def _rope_chunk(x, c, s, half, period):
    lane = lax.broadcasted_iota(jnp.int32, x.shape, 1)
    lo = (lane & (period - 1)) < half
    partner = jnp.where(lo, pltpu.roll(x, LANES - half, 1), pltpu.roll(x, half, 1))
    return x * c + partner * s


def _segment_specs(bounds):
    return [pl.BlockSpec((TM, D_MODEL), lambda i, lo=lo, hi=hi: (jnp.clip(i - lo, 0, hi - lo - 1), 0))
            for lo, hi in zip(bounds[:-1], bounds[1:])]


def _ffn_inproj_kernel(*refs, bounds):
    n_src = len(bounds) - 1
    srcs = refs[:n_src]
    (cda_ref, sda_ref, cm_ref, sm_ref, ga_ref, gb_ref,
     wg_ref, wu_ref, wd_ref, win_ref, gq_ref, gkv_ref, wuq_ref, wukv_ref,
     h1_ref, naq_ref, nak_ref, nav_ref, daq_ref, dak_ref, dav_ref,
     mq_ref, mk_ref, mv_ref, xn2_ref) = refs[n_src:]
    i = pl.program_id(0)

    @pl.when(i == 0)
    def _():
        xn2_ref[...] = jnp.zeros(xn2_ref.shape, xn2_ref.dtype)

    xn2 = xn2_ref[...]

    u = jnp.dot(xn2, win_ref[:, 0:3 * NA_W], preferred_element_type=jnp.float32)
    naq_ref[0] = (u[:, 0:NA_W] * (NA_HD ** -0.5 * LOG2E)).T.astype(jnp.bfloat16)
    nak_ref[...] = u[:, NA_W:2 * NA_W].astype(jnp.bfloat16)
    vt = u[:, 2 * NA_W:3 * NA_W].T.astype(jnp.bfloat16)
    for c in range(TM // LANES):
        nav_ref[c] = vt[:, c * LANES:(c + 1) * LANES]

    o0 = 3 * NA_W
    u = jnp.dot(xn2, win_ref[:, o0:o0 + 2 * DA_QK_W + DA_W], preferred_element_type=jnp.float32)
    cda, sda = cda_ref[...], sda_ref[...]
    q = jnp.concatenate([_rope_chunk(u[:, c * LANES:(c + 1) * LANES], cda, sda, DA_ROT // 2, DA_HD)
                         for c in range(DA_QK_W // LANES)], axis=1)
    k = jnp.concatenate([_rope_chunk(u[:, DA_QK_W + c * LANES:DA_QK_W + (c + 1) * LANES], cda, sda, DA_ROT // 2, DA_HD)
                         for c in range(DA_QK_W // LANES)], axis=1)
    daq_ref[0] = (q * (DA_HD ** -0.5 * LOG2E)).T.astype(jnp.bfloat16)
    dak_ref[...] = k.astype(jnp.bfloat16)
    vt = u[:, 2 * DA_QK_W:2 * DA_QK_W + DA_W].T.astype(jnp.bfloat16)
    for c in range(TM // TK):
        dav_ref[c] = vt[:, c * TK:(c + 1) * TK]

    o1 = o0 + 2 * DA_QK_W + DA_W
    u = jnp.dot(xn2, win_ref[:, o1:IN_W_PAD], preferred_element_type=jnp.float32)
    cm, sm = cm_ref[...], sm_ref[...]
    lane = lax.broadcasted_iota(jnp.int32, cm.shape, 1)
    band = (lane >= MLA_NOPE) & (lane < MLA_NOPE + MLA_ROPE)
    cb, sb = jnp.where(band, cm, 1.0), jnp.where(band, sm, 0.0)
    cq = _rms(u[:, 0:Q_LORA], gq_ref[...]).astype(jnp.bfloat16)
    qm = jnp.dot(cq, wuq_ref[...], preferred_element_type=jnp.float32)
    qm = jnp.concatenate([_rope_chunk(qm[:, h * LANES:(h + 1) * LANES], cb, sb, MLA_ROPE // 2, MLA_ROPE)
                          for h in range(MLA_HEADS)], axis=1) * ((MLA_NOPE + MLA_ROPE) ** -0.5 * LOG2E)
    mq_ref[0] = qm.T.astype(jnp.bfloat16)
    ckv = _rms(u[:, Q_LORA:Q_LORA + KV_LORA], gkv_ref[...]).astype(jnp.bfloat16)
    kv = jnp.dot(ckv, wukv_ref[...], preferred_element_type=jnp.float32)
    kr = _rope_chunk(u[:, Q_LORA + KV_LORA:Q_LORA + KV_LORA + LANES], cm, sm, MLA_ROPE // 2, MLA_ROPE)
    kr = jnp.where(band, kr, 0.0)
    kf = jnp.concatenate([kv[:, h * LANES:(h + 1) * LANES] + kr for h in range(MLA_HEADS)], axis=1)
    mk_ref[...] = kf.astype(jnp.bfloat16)
    vt = kv[:, MLA_HEADS * LANES:MLA_HEADS * LANES + MLA_W].T.astype(jnp.bfloat16)
    for c in range(TM // TK):
        mv_ref[c] = vt[:, c * TK:(c + 1) * TK]

    x = srcs[-1][...]
    for sgm in range(n_src - 2, -1, -1):
        x = jnp.where(i < bounds[sgm + 1], srcs[sgm][...], x)
    xn = _rms(x, ga_ref[...]).astype(jnp.bfloat16)
    h1 = x + _swiglu_half(xn, wg_ref, wu_ref, wd_ref)
    h1_ref[...] = h1
    xn2_ref[...] = _rms(h1, gb_ref[...]).astype(jnp.bfloat16)


def _outproj_ffn_kernel(h_ref, ona_ref, oda_ref, om_ref, wo1_ref, wo2_ref, wo3_ref, g_ref,
                        wg_ref, wu_ref, wd_ref, gf_ref, *o_refs, out_bounds):
    h = h_ref[...]
    h = h + jnp.dot(ona_ref[...], wo1_ref[...], preferred_element_type=jnp.float32)
    h = h + jnp.dot(oda_ref[...], wo2_ref[...], preferred_element_type=jnp.float32)
    h = h + jnp.dot(om_ref[...], wo3_ref[...], preferred_element_type=jnp.float32)
    xn = _rms(h, g_ref[...]).astype(jnp.bfloat16)
    h = h + _swiglu_half(xn, wg_ref, wu_ref, wd_ref)
    if out_bounds is None:
        o_refs[0][...] = h
    else:
        h = _rms(h, gf_ref[...])
        i = pl.program_id(0)
        for o_ref, lo, hi in zip(o_refs, out_bounds[:-1], out_bounds[1:]):
            @pl.when((i >= lo) & (i < hi))
            def _(o_ref=o_ref):
                o_ref[...] = h


def _const_spec(shape):
    nd = len(shape)
    return pl.BlockSpec(shape, lambda i: (0,) * nd, pipeline_mode=pl.Buffered(1))


def _ffn_inproj(srcs, bounds, tabs, tab_block, ga, gb, wg, wu, wd, win, gq, gkv, wuq, wukv):
    nt = bounds[-1]
    rows = nt * TM
    prev = lambda i: jnp.maximum(i - 1, 0)
    row_spec = lambda w: pl.BlockSpec((TM, w), lambda i: (prev(i), 0))
    tile_spec = lambda r: pl.BlockSpec((1, r, TM), lambda i: (prev(i), 0, 0))
    bf = jnp.bfloat16
    out_shape = (
        jax.ShapeDtypeStruct((rows, D_MODEL), jnp.float32),
        jax.ShapeDtypeStruct((nt, NA_W, TM), bf),
        jax.ShapeDtypeStruct((rows, NA_W), bf),
        jax.ShapeDtypeStruct((rows // LANES, NA_W, LANES), bf),
        jax.ShapeDtypeStruct((nt, DA_QK_W, TM), bf),
        jax.ShapeDtypeStruct((rows, DA_QK_W), bf),
        jax.ShapeDtypeStruct((rows // TK, DA_W, TK), bf),
        jax.ShapeDtypeStruct((nt, MLA_HEADS * LANES, TM), bf),
        jax.ShapeDtypeStruct((rows, MLA_QK_W), bf),
        jax.ShapeDtypeStruct((rows // TK, MLA_W, TK), bf),
    )
    out_specs = (
        pl.BlockSpec((TM, D_MODEL), lambda i: (jnp.minimum(i, nt - 1), 0)),
        tile_spec(NA_W),
        row_spec(NA_W),
        pl.BlockSpec((TM // LANES, NA_W, LANES), lambda i: (prev(i), 0, 0)),
        tile_spec(DA_QK_W),
        row_spec(DA_QK_W),
        pl.BlockSpec((TM // TK, DA_W, TK), lambda i: (prev(i), 0, 0)),
        tile_spec(MLA_HEADS * LANES),
        row_spec(MLA_QK_W),
        pl.BlockSpec((TM // TK, MLA_W, TK), lambda i: (prev(i), 0, 0)),
    )
    in_specs = _segment_specs(bounds) + [pl.BlockSpec((TM, LANES), lambda i: (tab_block(prev(i)), 0))] * 4 + [
        _const_spec(a.shape) for a in (ga, gb, wg, wu, wd, win, gq, gkv, wuq, wukv)]
    return pl.pallas_call(
        functools.partial(_ffn_inproj_kernel, bounds=bounds),
        grid=(nt + 1,),
        in_specs=in_specs,
        out_specs=out_specs,
        out_shape=out_shape,
        scratch_shapes=[pltpu.VMEM((TM, D_MODEL), jnp.bfloat16)],
        compiler_params=pltpu.CompilerParams(dimension_semantics=("arbitrary",),
                                             vmem_limit_bytes=VMEM_LIMIT_BYTES),
        name="ffn_inproj",
    )(*srcs, *tabs, ga, gb, wg, wu, wd, win, gq, gkv, wuq, wukv)


def _outproj_ffn(h, ona, oda, om, wo1, wo2, wo3, g, wg, wu, wd, gf, out_bounds):
    rows = h.shape[0]
    row_spec = lambda w: pl.BlockSpec((TM, w), lambda i: (i, 0))
    if out_bounds is None:
        out_specs = row_spec(D_MODEL)
        out_shape = jax.ShapeDtypeStruct((rows, D_MODEL), jnp.float32)
    else:
        out_specs = tuple(_segment_specs(out_bounds))
        out_shape = tuple(jax.ShapeDtypeStruct(((hi - lo) * TM, D_MODEL), jnp.float32)
                          for lo, hi in zip(out_bounds[:-1], out_bounds[1:]))
    in_specs = [row_spec(D_MODEL), row_spec(NA_W), row_spec(DA_W), row_spec(MLA_W)] + [
        _const_spec(a.shape) for a in (wo1, wo2, wo3, g, wg, wu, wd, gf)]
    return pl.pallas_call(
        functools.partial(_outproj_ffn_kernel, out_bounds=out_bounds),
        grid=(rows // TM,),
        in_specs=in_specs,
        out_specs=out_specs,
        out_shape=out_shape,
        compiler_params=pltpu.CompilerParams(dimension_semantics=("arbitrary",),
                                             vmem_limit_bytes=VMEM_LIMIT_BYTES),
        name="outproj_ffn",
    )(h, ona, oda, om, wo1, wo2, wo3, g, wg, wu, wd, gf)


def _row_band(block, lo, hi):
    row = lax.broadcasted_iota(jnp.int32, block.shape, 0)
    return jnp.where((row >= lo) & (row < hi), block, jnp.zeros_like(block))


def _with_ones(vt):
    return jnp.concatenate([vt, jnp.ones((ONES_ROWS, vt.shape[1]), vt.dtype)], axis=0)


def _meta_key_mask(b):
    row = lax.broadcasted_iota(jnp.int32, (LANES, 1), 0)
    lo = (b % (LANES // N_META)) * N_META
    return jnp.where((row >= lo) & (row < lo + N_META), 0.0, NEG).astype(jnp.float32)


class FlashCfg(NamedTuple):
    n_soft: int
    group_w: int
    soft_per_group: int
    v_of_soft: tuple
    out_w: int
    unroll: int


DA_CFG = FlashCfg(2 * DA_HEADS, LANES, 4, tuple(i // 2 for i in range(2 * DA_HEADS)), DA_W, 16)
MLA_CFG = FlashCfg(MLA_HEADS, LANES, 1, tuple(range(MLA_HEADS)), MLA_W, 32)
NA_CFG = FlashCfg(NA_HEADS, LANES, 2, tuple(range(NA_HEADS)), NA_W, 1)


def _build_rhs(cfg, qt_ref, rhs_ref):
    for i in range(cfg.n_soft):
        g, j = i // cfg.soft_per_group, i % cfg.soft_per_group
        if cfg is DA_CFG:
            rhs_ref[i] = _row_band(qt_ref[0, g * LANES:(g + 1) * LANES, :], j * DA_HD, (j + 1) * DA_HD)
        else:
            rhs_ref[i] = qt_ref[0, i * LANES:(i + 1) * LANES, :]


def _flash_tiles(cfg, tiles, rhs_ref, m_ref, acc_ref, bounded):
    nt = len(tiles)
    if bounded:
        work = [(t, i) for i in range(cfg.n_soft) for t in range(nt)]
    else:
        work = [(t, i) for t in range(nt) for i in range(cfg.n_soft)]

    def score(w):
        kblk_of, _, mask = tiles[w[0]]
        s = jnp.dot(kblk_of(w[1] // cfg.soft_per_group), rhs_ref[w[1]], preferred_element_type=jnp.float32)
        return s if mask is None else s + mask

    pending = [score(w) for w in work[:LOOKAHEAD]]
    part = den = None
    for n, (t, i) in enumerate(work):
        vt = tiles[t][1](cfg.v_of_soft[i])
        s = pending.pop(0)
        if n + LOOKAHEAD < len(work):
            pending.append(score(work[n + LOOKAHEAD]))
        if bounded:
            p = jnp.exp2(s)
            pv = jnp.dot(vt, p.astype(jnp.bfloat16), preferred_element_type=jnp.float32)
            ps = jnp.sum(p, axis=0, keepdims=True)
            part, den = (pv, ps) if t == 0 else (part + pv, den + ps)
            if t == nt - 1:
                acc_ref[i, 0:DA_VD, :] = acc_ref[i, 0:DA_VD, :] + part
                acc_ref[i, DA_VD:DA_VD + 1, :] = acc_ref[i, DA_VD:DA_VD + 1, :] + den
        else:
            vt = _with_ones(vt)
            m_prev = m_ref[i]
            m_new = jnp.maximum(m_prev, jnp.max(s, axis=0, keepdims=True))
            alpha = jnp.exp2(m_prev - m_new)
            p = jnp.exp2(s - m_new).astype(jnp.bfloat16)
            acc_ref[i] = alpha * acc_ref[i] + jnp.dot(vt, p, preferred_element_type=jnp.float32)
            m_ref[i] = m_new


def _feature_indicator(cfg):
    f = lax.broadcasted_iota(jnp.int32, (cfg.group_w, LANES), 0)
    j = lax.broadcasted_iota(jnp.int32, (cfg.group_w, LANES), 1)
    band = cfg.group_w // cfg.soft_per_group
    hit = jnp.right_shift(f, int(math.log2(band))) == j
    return jnp.where(hit & (j < cfg.soft_per_group), 1.0, 0.0).astype(jnp.bfloat16)


def _max_key_norms(cfg, kf_ref, kfm_ref, n_keys, kn_ref):
    gw = cfg.group_w
    n_groups = cfg.n_soft // cfg.soft_per_group
    ind = _feature_indicator(cfg)

    def sq_norms(blk):
        x = blk.astype(jnp.float32)
        n2 = jnp.dot((x * x).astype(jnp.bfloat16), ind, preferred_element_type=jnp.float32)
        return jnp.max(n2, axis=0, keepdims=True)

    def body(c, carry):
        r0 = pl.multiple_of(c * TM, TM)
        return tuple(jnp.maximum(carry[g], sq_norms(kf_ref[pl.ds(r0, TM), g * gw:(g + 1) * gw]))
                     for g in range(n_groups))

    init = tuple(sq_norms(kfm_ref[:, g * gw:(g + 1) * gw]) for g in range(n_groups))
    kmax = lax.fori_loop(0, n_keys // TM, body, init)
    lane = lax.broadcasted_iota(jnp.int32, (1, LANES), 1)
    for i in range(cfg.n_soft):
        g, j = divmod(i, cfg.soft_per_group)
        v = jnp.max(jnp.where(lane == j, kmax[g], 0.0), axis=1, keepdims=True)
        kn_ref[i] = jnp.broadcast_to(v, (1, LANES))


def _scores_are_bounded(cfg, rhs_ref, kn_ref):
    ok = None
    for i in range(cfg.n_soft):
        r = rhs_ref[i].astype(jnp.float32)
        bound2 = jnp.sum(r * r, axis=0, keepdims=True) * kn_ref[i][:, 0:1] * NORM_SLACK
        good = bound2 <= SCORE_BOUND * SCORE_BOUND
        ok = good if ok is None else ok & good
    return jnp.min(jnp.where(ok, 1.0, 0.0)) > 0.5


def _flash_kernel(*refs, cfg, n_keys, lam_init, meta_queries):
    if cfg is DA_CFG:
        (qt_ref, kf_ref, vt_ref, kfm_ref, vtm_ref, lam_ref, gs_ref, o_ref,
         rhs_ref, m_ref, acc_ref, kn_ref, *rest) = refs
    else:
        qt_ref, kf_ref, vt_ref, kfm_ref, vtm_ref, o_ref, rhs_ref, m_ref, acc_ref, kn_ref, *rest = refs
    b = pl.program_id(0)
    tq = qt_ref.shape[2]
    vd = DA_VD
    gw = cfg.group_w
    if meta_queries:
        _max_key_norms(cfg, kf_ref, kfm_ref, n_keys, kn_ref)
    else:
        @pl.when(pl.program_id(1) == 0)
        def _():
            _max_key_norms(cfg, kf_ref, kfm_ref, n_keys, kn_ref)
    _build_rhs(cfg, qt_ref, rhs_ref)
    m_ref[...] = jnp.full(m_ref.shape, NEG, jnp.float32)
    acc_ref[...] = jnp.zeros(acc_ref.shape, jnp.float32)

    def all_keys(bounded, unroll):
        _flash_tiles(
            cfg,
            [(lambda g: kfm_ref[:, g * gw:(g + 1) * gw],
              lambda h: vtm_ref[0, h * vd:(h + 1) * vd, :],
              _meta_key_mask(b))],
            rhs_ref, m_ref, acc_ref, bounded)

        def body(it, carry):
            tiles = []
            for u in range(unroll):
                kt = it * unroll + u
                k0 = pl.multiple_of(kt * TK, TK)
                tiles.append((
                    lambda g, k0=k0: kf_ref[pl.ds(k0, TK), g * gw:(g + 1) * gw],
                    lambda h, kt=kt: vt_ref[kt, h * vd:(h + 1) * vd, :],
                    None))
            _flash_tiles(cfg, tiles, rhs_ref, m_ref, acc_ref, bounded)
            return carry

        lax.fori_loop(0, n_keys // (TK * unroll), body, 0)

    lax.cond(_scores_are_bounded(cfg, rhs_ref, kn_ref),
             lambda: all_keys(True, math.gcd(cfg.unroll, n_keys // TK)),
             lambda: all_keys(False, 1))

    heads = []
    if cfg is DA_CFG:
        lp = lam_ref[...]
        lam = (jnp.exp(jnp.sum(lp[0:1] * lp[1:2], axis=1, keepdims=True))
               - jnp.exp(jnp.sum(lp[2:3] * lp[3:4], axis=1, keepdims=True)) + lam_init)
        for h in range(DA_HEADS):
            a1, a2 = acc_ref[2 * h], acc_ref[2 * h + 1]
            o = a1[0:vd] / a1[vd:vd + 1] - lam * (a2[0:vd] / a2[vd:vd + 1])
            o = o * lax.rsqrt(jnp.mean(o * o, axis=0, keepdims=True) + EPS) * gs_ref[...]
            heads.append(o * (1.0 - lam_init))
    else:
        for h in range(MLA_HEADS):
            a = acc_ref[h]
            heads.append(a[0:vd] / a[vd:vd + 1])
    pairs = [jnp.concatenate(heads[2 * g:2 * g + 2], axis=0).T for g in range(len(heads) // 2)]
    out = jnp.concatenate(pairs, axis=1)
    if meta_queries:
        stage_ref = rest[0]
        stage_ref[...] = out
        r0 = pl.multiple_of((b % (LANES // N_META)) * N_META, N_META)
        o_ref[...] = stage_ref[pl.ds(r0, N_META), :].astype(o_ref.dtype)
    else:
        o_ref[...] = out.astype(o_ref.dtype)


def _flash_call(cfg, grp, qt, kf, vt, extra, o_prev, lam_init, meta_queries, name):
    n, bsz = grp.n, grp.batch
    qw, kw, vw = qt.shape[1], kf.shape[1], vt.shape[1]
    tiles_per_seq = n // TM
    meta_blk0 = grp.meta0 // LANES
    per_blk = LANES // N_META
    if meta_queries:
        grid = (bsz,)
        tq = LANES
        q_map = lambda b: ((meta_blk0 + b // per_blk) // (TM // LANES), 0, (meta_blk0 + b // per_blk) % (TM // LANES))
        o_spec = pl.BlockSpec((N_META, cfg.out_w), lambda b: (grp.meta0 // N_META + b, 0))
        fix = lambda f: (lambda b: f(b))
        sem = ("arbitrary",)
    else:
        grid = (bsz, tiles_per_seq)
        tq = TM
        q_map = lambda b, i: (grp.real0 // TM + b * tiles_per_seq + i, 0, 0)
        o_spec = pl.BlockSpec((TM, cfg.out_w), lambda b, i: (grp.real0 // TM + b * tiles_per_seq + i, 0))
        fix = lambda f: (lambda b, i: f(b))
        sem = ("arbitrary", "arbitrary")
    in_specs = [
        pl.BlockSpec((1, qw, tq), q_map),
        pl.BlockSpec((n, kw), fix(lambda b: (grp.real0 // n + b, 0))),
        pl.BlockSpec((n // TK, vw, TK), fix(lambda b: (grp.real0 // n + b, 0, 0))),
        pl.BlockSpec((LANES, kw), fix(lambda b: (meta_blk0 + b // per_blk, 0))),
        pl.BlockSpec((1, vw, LANES), fix(lambda b: ((meta_blk0 + b // per_blk) // (TK // LANES), 0,
                                                     (meta_blk0 + b // per_blk) % (TK // LANES)))),
    ]
    args = [qt, kf, vt, kf, vt]
    for a in extra:
        in_specs.append(pl.BlockSpec(a.shape, fix(lambda b, nd=a.ndim: (0,) * nd)))
        args.append(a)
    in_specs.append(pl.BlockSpec(memory_space=pl.ANY))
    args.append(o_prev)
    scratch = [
        pltpu.VMEM((cfg.n_soft, cfg.group_w, tq), jnp.bfloat16),
        pltpu.VMEM((cfg.n_soft, 1, tq), jnp.float32),
        pltpu.VMEM((cfg.n_soft, DA_VD + ONES_ROWS, tq), jnp.float32),
        pltpu.VMEM((cfg.n_soft, 1, LANES), jnp.float32),
    ]
    if meta_queries:
        scratch.append(pltpu.VMEM((LANES, cfg.out_w), jnp.float32))

    def body(*refs):
        n_in = len(args)
        ins, rest = refs[:n_in - 1], refs[n_in:]
        _flash_kernel(*ins, *rest, cfg=cfg, n_keys=n, lam_init=lam_init, meta_queries=meta_queries)

    return pl.pallas_call(
        body,
        grid=grid,
        in_specs=in_specs,
        out_specs=o_spec,
        out_shape=jax.ShapeDtypeStruct(o_prev.shape, o_prev.dtype),
        scratch_shapes=scratch,
        input_output_aliases={len(args) - 1: 0},
        compiler_params=pltpu.CompilerParams(dimension_semantics=sem, vmem_limit_bytes=VMEM_LIMIT_BYTES),
        name=name,
    )(*args)


def _na_kernel(qt_ref, kf_ref, vt_ref, kfm_ref, vtm_ref, bias_ref, bmax_ref, prev_ref, o_ref, kn_ref, *, grid_rows):
    del prev_ref
    b, i = pl.program_id(0), pl.program_id(1)

    @pl.when(i == 0)
    def _():
        _max_key_norms(NA_CFG, kf_ref, kfm_ref, grid_rows * GRID_W, kn_ref)

    ws = jnp.clip(i * NA_QROWS - WIN_R // 2, 0, grid_rows - NA_KROWS)
    kwin = kf_ref[pl.ds(pl.multiple_of(ws * GRID_W, 2 * GRID_W), NA_KROWS * GRID_W), :]
    vwin = vt_ref[pl.ds(ws // 2, NA_KROWS // 2)]
    kmeta = kfm_ref[...]
    mmask = _meta_key_mask(b)

    def rhs_of(h):
        g, j = h // 2, h % 2
        return _row_band(qt_ref[0, g * LANES:(g + 1) * LANES, :], j * NA_HD, (j + 1) * NA_HD)

    def scores(h):
        g, rhs = h // 2, rhs_of(h)
        s1 = jnp.dot(kwin[:, g * LANES:(g + 1) * LANES], rhs, preferred_element_type=jnp.float32) + bias_ref[0, h]
        s2 = jnp.dot(kmeta[:, g * LANES:(g + 1) * LANES], rhs, preferred_element_type=jnp.float32) + mmask
        return s1, s2

    def run(bounded):
        heads = []
        pending = [scores(h) for h in range(LOOKAHEAD)]
        for h in range(NA_HEADS):
            s1, s2 = pending.pop(0)
            if h + LOOKAHEAD < NA_HEADS:
                pending.append(scores(h + LOOKAHEAD))
            if not bounded:
                m = jnp.maximum(jnp.max(s1, axis=0, keepdims=True), jnp.max(s2, axis=0, keepdims=True))
                s1, s2 = s1 - m, s2 - m
            p1 = jnp.exp2(s1).astype(jnp.bfloat16)
            p2 = jnp.exp2(s2).astype(jnp.bfloat16)
            v1 = jnp.concatenate([vwin[c, h * NA_HD:(h + 1) * NA_HD, :] for c in range(NA_KROWS // 2)], axis=1)
            acc = (jnp.dot(_with_ones(v1), p1, preferred_element_type=jnp.float32)
                   + jnp.dot(_with_ones(vtm_ref[0, h * NA_HD:(h + 1) * NA_HD, :]), p2,
                             preferred_element_type=jnp.float32))
            heads.append(acc[0:NA_HD] / acc[NA_HD:NA_HD + 1])
        pairs = [jnp.concatenate(heads[2 * g:2 * g + 2], axis=0).T for g in range(NA_HEADS // 2)]
        o_ref[...] = jnp.concatenate(pairs, axis=1).astype(o_ref.dtype)

    limit = SCORE_BOUND - bmax_ref[...]
    ok = None
    for h in range(NA_HEADS):
        r = rhs_of(h).astype(jnp.float32)
        bound2 = jnp.sum(r * r, axis=0, keepdims=True) * kn_ref[h][:, 0:1] * NORM_SLACK
        good = (limit > 0.0) & (bound2 <= limit * limit)
        ok = good if ok is None else ok & good
    lax.cond(jnp.min(jnp.where(ok, 1.0, 0.0)) > 0.5, lambda: run(True), lambda: run(False))


def _na_meta_kernel(qt_ref, kf_ref, vt_ref, prev_ref, o_ref):
    del prev_ref
    kr = lax.broadcasted_iota(jnp.int32, (LANES, LANES), 0) // N_META
    qc = lax.broadcasted_iota(jnp.int32, (LANES, LANES), 1) // N_META
    mask = jnp.where(kr == qc, 0.0, NEG).astype(jnp.float32)
    kf = kf_ref[...]
    heads = []
    for h in range(NA_HEADS):
        g, j = h // 2, h % 2
        rhs = _row_band(qt_ref[0, g * LANES:(g + 1) * LANES, :], j * NA_HD, (j + 1) * NA_HD)
        s = jnp.dot(kf[:, g * LANES:(g + 1) * LANES], rhs, preferred_element_type=jnp.float32) + mask
        p = jnp.exp2(s - jnp.max(s, axis=0, keepdims=True)).astype(jnp.bfloat16)
        acc = jnp.dot(_with_ones(vt_ref[0, h * NA_HD:(h + 1) * NA_HD, :]), p, preferred_element_type=jnp.float32)
        heads.append(acc[0:NA_HD] / acc[NA_HD:NA_HD + 1])
    pairs = [jnp.concatenate(heads[2 * g:2 * g + 2], axis=0).T for g in range(NA_HEADS // 2)]
    o_ref[...] = jnp.concatenate(pairs, axis=1).astype(o_ref.dtype)


def _na_call(grp, qt, kf, vt, bias, bmax, o_prev):
    n, bsz = grp.n, grp.batch
    grid_rows = n // GRID_W
    steps = grid_rows // NA_QROWS
    tq = NA_QROWS * GRID_W
    meta_blk0 = grp.meta0 // LANES
    per_blk = LANES // N_META
    in_specs = [
        pl.BlockSpec((1, NA_W, tq), lambda b, i: (grp.real0 // TM + b * (n // TM) + i // (TM // tq), 0, i % (TM // tq))),
        pl.BlockSpec((n, NA_W), lambda b, i: (grp.real0 // n + b, 0)),
        pl.BlockSpec((n // LANES, NA_W, LANES), lambda b, i: (grp.real0 // n + b, 0, 0)),
        pl.BlockSpec((LANES, NA_W), lambda b, i: (meta_blk0 + b // per_blk, 0)),
        pl.BlockSpec((1, NA_W, LANES), lambda b, i: (meta_blk0 + b // per_blk, 0, 0)),
        pl.BlockSpec((1, NA_HEADS, NA_KROWS * GRID_W, tq),
                     lambda b, i: (jnp.where(i == 0, 0, jnp.where(i == steps - 1, 2, 1)), 0, 0, 0)),
        pl.BlockSpec((1, 1), lambda b, i: (0, 0)),
        pl.BlockSpec(memory_space=pl.ANY),
    ]
    return pl.pallas_call(
        functools.partial(_na_kernel, grid_rows=grid_rows),
        grid=(bsz, steps),
        in_specs=in_specs,
        out_specs=pl.BlockSpec((tq, NA_W), lambda b, i: (grp.real0 // tq + b * steps + i, 0)),
        out_shape=jax.ShapeDtypeStruct(o_prev.shape, o_prev.dtype),
        scratch_shapes=[pltpu.VMEM((NA_HEADS, 1, LANES), jnp.float32)],
        input_output_aliases={7: 0},
        compiler_params=pltpu.CompilerParams(dimension_semantics=("arbitrary", "arbitrary"),
                                             vmem_limit_bytes=VMEM_LIMIT_BYTES),
        name="na_real",
    )(qt, kf, vt, kf, vt, bias, bmax, o_prev)


def _na_meta_call(layout, qt, kf, vt, o_prev):
    blk0 = layout.groups[0].meta0 // LANES
    nblk = sum(_round_up(g.batch * N_META, LANES) for g in layout.groups) // LANES
    sub = TM // LANES
    return pl.pallas_call(
        _na_meta_kernel,
        grid=(nblk,),
        in_specs=[
            pl.BlockSpec((1, NA_W, LANES), lambda i: ((blk0 + i) // sub, 0, (blk0 + i) % sub)),
            pl.BlockSpec((LANES, NA_W), lambda i: (blk0 + i, 0)),
            pl.BlockSpec((1, NA_W, LANES), lambda i: (blk0 + i, 0, 0)),
            pl.BlockSpec(memory_space=pl.ANY),
        ],
        out_specs=pl.BlockSpec((LANES, NA_W), lambda i: (blk0 + i, 0)),
        out_shape=jax.ShapeDtypeStruct(o_prev.shape, o_prev.dtype),
        input_output_aliases={3: 0},
        compiler_params=pltpu.CompilerParams(dimension_semantics=("arbitrary",),
                                             vmem_limit_bytes=VMEM_LIMIT_BYTES),
        name="na_meta",
    )(qt, kf, vt, o_prev)


def _na_bias_tables(rel_bias):
    kc = np.arange(GRID_W)[:, None]
    qc = np.arange(GRID_W)[None, :]
    cs = np.clip(qc - WIN_C // 2, 0, GRID_W - WIN_C)
    col_ok = (kc >= cs) & (kc < cs + WIN_C)
    cidx = np.clip(kc - qc + WIN_C - 1, 0, 2 * WIN_C - 2)
    planes = jnp.where(col_ok[None, None], rel_bias[:, :, cidx] * LOG2E, NEG)
    masked = 2 * WIN_R - 1
    planes = jnp.concatenate([planes, jnp.full((NA_HEADS, 1, GRID_W, GRID_W), NEG, planes.dtype)], axis=1)
    kr = np.arange(NA_KROWS)[:, None]
    qr = np.arange(NA_QROWS)[None, :]
    ridx = []
    for win_lo, q_off in ((0 * qr, qr), (qr, qr + WIN_R // 2), (0 * qr + NA_KROWS - WIN_R, qr + NA_KROWS - NA_QROWS)):
        row_ok = (kr >= win_lo) & (kr < win_lo + WIN_R)
        ridx.append(np.where(row_ok, kr - q_off + WIN_R - 1, masked))
    ridx = np.stack(ridx)
    assert ridx.min() >= 0 and ridx.max() <= masked
    t = planes[:, ridx]
    t = jnp.transpose(t, (1, 0, 2, 4, 3, 5))
    return t.reshape(3, NA_HEADS, NA_KROWS * GRID_W, NA_QROWS * GRID_W).astype(jnp.float32)


def _rope_tables(layout):
    pos, starts, periods = [], [], []
    for g in layout.groups:
        starts.append(len(pos))
        periods.append(g.n // TM)
        for t in range(g.n // TM):
            pos.append(N_META + t * TM + np.arange(TM))
    tail0 = sum(g.batch * g.n for g in layout.groups)
    starts.append(len(pos))
    for t in range(tail0 // TM, layout.rows // TM):
        r = t * TM + np.arange(TM)
        p = np.zeros((TM,), np.int64)
        for g in layout.groups:
            inside = (r >= g.meta0) & (r < g.meta0 + g.batch * N_META)
            p = np.where(inside, (r - g.meta0) % N_META, p)
        pos.append(p)
    pos = jnp.asarray(np.concatenate(pos).astype(np.float32))
    bounds = [g.real0 // TM for g in layout.groups] + [tail0 // TM]

    def block_of_tile(i):
        blk = starts[-1] + (i - bounds[-1])
        for k in range(len(layout.groups) - 1, -1, -1):
            blk = jnp.where(i < bounds[k + 1], starts[k] + (i - bounds[k]) % periods[k], blk)
        return blk

    def tables(dim, period):
        inv = ROPE_THETA ** (-(jnp.arange(0, dim, 2, dtype=jnp.float32) / dim))
        ang = pos[:, None] * inv[None, :]
        cos, sin = jnp.cos(ang), jnp.sin(ang)
        rest = period - dim
        c = jnp.concatenate([cos, cos, jnp.ones((pos.shape[0], rest), jnp.float32)], axis=1)
        s = jnp.concatenate([-sin, sin, jnp.zeros((pos.shape[0], rest), jnp.float32)], axis=1)
        reps = LANES // period
        return jnp.tile(c, (1, reps)), jnp.tile(s, (1, reps))

    cda, sda = tables(DA_ROT, DA_HD)
    cm, sm = tables(MLA_ROPE, MLA_ROPE)
    return (cda, sda, cm, sm), block_of_tile


def _prep_layer(l, norm_g, ffn_w_gate, ffn_w_up, ffn_w_down, w_in, w_out, mla_q_norm_g, mla_kv_norm_g,
                mla_w_uq, mla_w_ukv, da_subln_g):
    bf = jnp.bfloat16
    kr0 = IN_W - MLA_ROPE
    win = jnp.concatenate([w_in[l, :, :kr0]] + [w_in[l, :, kr0:]] * (LANES // MLA_ROPE), axis=1).astype(bf)
    uq = mla_w_uq[l].reshape(Q_LORA, MLA_HEADS, MLA_NOPE + MLA_ROPE)
    uq = jnp.pad(uq, ((0, 0), (0, 0), (0, LANES - MLA_NOPE - MLA_ROPE))).reshape(Q_LORA, MLA_HEADS * LANES)
    ukv = mla_w_ukv[l].reshape(KV_LORA, MLA_HEADS, MLA_NOPE + MLA_VD)
    uk = jnp.pad(ukv[:, :, :MLA_NOPE], ((0, 0), (0, 0), (0, LANES - MLA_NOPE))).reshape(KV_LORA, MLA_HEADS * LANES)
    ukv = jnp.concatenate([uk, ukv[:, :, MLA_NOPE:].reshape(KV_LORA, -1)], axis=1)
    row = lambda v: v.reshape(1, -1)
    return dict(
        g=[row(norm_g[l, i]) for i in range(3)],
        ffn=[(ffn_w_gate[l, i].astype(bf), ffn_w_up[l, i].astype(bf), ffn_w_down[l, i].astype(bf)) for i in range(2)],
        win=win, uq=uq.astype(bf), ukv=ukv.astype(bf),
        gq=row(mla_q_norm_g[l]), gkv=row(mla_kv_norm_g[l]),
        wo=(w_out[l, :NA_W].astype(bf), w_out[l, NA_W:NA_W + DA_W].astype(bf), w_out[l, NA_W + DA_W:].astype(bf)),
        gs=da_subln_g[l].reshape(DA_VD, 1),
    )


def kernel(x_prompt, x_sample, meta_tokens, norm_g, final_norm_g, ffn_w_gate, ffn_w_up, ffn_w_down, w_in, w_out, na_rel_bias, da_lambda, da_subln_g, mla_q_norm_g, mla_kv_norm_g, mla_w_uq, mla_w_ukv):
    xs = (x_prompt, x_sample)
    layout = _make_layout([(x.shape[0], x.shape[1]) for x in xs])
    real_rows = sum(g.batch * g.n for g in layout.groups)
    tail = []
    for g in layout.groups:
        blk = jnp.tile(meta_tokens.astype(jnp.float32), (g.batch, 1))
        tail.append(jnp.pad(blk, ((0, _round_up(g.batch * N_META, LANES) - g.batch * N_META), (0, 0))))
    tail = jnp.concatenate(tail, axis=0)
    tail = jnp.pad(tail, ((0, layout.rows - real_rows - tail.shape[0]), (0, 0)))
    srcs = [x.reshape(-1, D_MODEL) for x in xs] + [tail]
    real_bounds = tuple(g.real0 // TM for g in layout.groups) + (real_rows // TM,)
    bounds = real_bounds + (layout.rows // TM,)
    tabs, tab_block = _rope_tables(layout)
    gf = final_norm_g.reshape(1, -1)

    o_na = jnp.zeros((layout.rows, NA_W), jnp.bfloat16)
    o_da = jnp.zeros((layout.rows, DA_W), jnp.bfloat16)
    o_m = jnp.zeros((layout.rows, MLA_W), jnp.bfloat16)
    for l in range(DEPTH):
        p = _prep_layer(l, norm_g, ffn_w_gate, ffn_w_up, ffn_w_down, w_in, w_out, mla_q_norm_g, mla_kv_norm_g,
                        mla_w_uq, mla_w_ukv, da_subln_g)
        lam_init = 0.8 - 0.6 * math.exp(-0.3 * l)
        (h, naq, nak, nav, daq, dak, dav, mq, mk, mv) = _ffn_inproj(
            srcs, bounds, tabs, tab_block, p["g"][0], p["g"][1], *p["ffn"][0], p["win"], p["gq"], p["gkv"], p["uq"], p["ukv"])
        bias = _na_bias_tables(na_rel_bias[l])
        bmax = (jnp.max(jnp.abs(na_rel_bias[l])) * LOG2E).reshape(1, 1)
        o_na = _na_meta_call(layout, naq, nak, nav, o_na)
        for gi, grp in enumerate(layout.groups):
            o_na = _na_call(grp, naq, nak, nav, bias, bmax, o_na)
            for meta_q in (False, True):
                tag = f"g{gi}_{'meta' if meta_q else 'real'}"
                o_da = _flash_call(DA_CFG, grp, daq, dak, dav, (da_lambda[l], p["gs"]), o_da, lam_init, meta_q,
                                   "da_" + tag)
                o_m = _flash_call(MLA_CFG, grp, mq, mk, mv, (), o_m, lam_init, meta_q, "mla_" + tag)
        last = l == DEPTH - 1
        h = _outproj_ffn(h, o_na, o_da, o_m, *p["wo"], p["g"][2], *p["ffn"][1], gf, real_bounds if last else None)
        srcs, bounds = [h], (0, layout.rows // TM)

    return tuple(y.reshape(x.shape) for x, y in zip(xs, h))
```

```python
import functools
import math
from typing import NamedTuple

import jax
import jax.numpy as jnp
import numpy as np
from jax import lax
from jax.experimental import pallas as pl
from jax.experimental.pallas import tpu as pltpu

D_MODEL = 1024
DEPTH = 2
GRID_W = 64
N_META = 16
WIN_R = 8
WIN_C = 16
NA_HEADS = 6
NA_HD = 64
DA_HEADS = 6
DA_HD = 32
DA_VD = 64
DA_ROT = DA_HD // 4
MLA_HEADS = 4
MLA_NOPE = 64
MLA_ROPE = 32
MLA_VD = 64
Q_LORA = 256
KV_LORA = 128
ROPE_THETA = 500000.0
D_FF = 2816
EPS = 1e-6
NA_W = NA_HEADS * NA_HD
DA_W = DA_HEADS * DA_VD
MLA_W = MLA_HEADS * MLA_VD
DA_QK_W = DA_HEADS * 2 * DA_HD
IN_W = 3 * NA_W + 2 * DA_QK_W + DA_W + Q_LORA + KV_LORA + MLA_ROPE

LANES = 128
MXU_DIM = 256
VMEM_LIMIT_BYTES = 60 * 1024 * 1024

TM = 512
TK = 256
META_TQ = MXU_DIM
NA_QROWS = 4
NA_KROWS = NA_QROWS + WIN_R
LOOKAHEAD = 2
FF_SPLIT = 1536
IN_W_PAD = IN_W - MLA_ROPE + LANES
MLA_QK_W = MLA_HEADS * LANES
assert MLA_VD == DA_VD and IN_W_PAD % LANES == 0

LOG2E = 1.4426950408889634
NEG = -1e30
ONES_ROWS = 16
SCORE_BOUND = 60.0
NORM_SLACK = 1.05


def _round_up(x, m):
    return (x + m - 1) // m * m


class Group(NamedTuple):
    batch: int
    n: int
    real0: int
    meta0: int


class Layout(NamedTuple):
    groups: tuple
    rows: int


def _make_layout(shapes):
    row = 0
    real0 = []
    for b, n in shapes:
        assert n % TM == 0 and row % n == 0 and n % GRID_W == 0
        assert (n // GRID_W) % NA_QROWS == 0 and n // GRID_W >= NA_KROWS
        real0.append(row)
        row += b * n
    groups = []
    for (b, n), r0 in zip(shapes, real0):
        groups.append(Group(b, n, r0, row))
        row += _round_up(b * N_META, LANES)
    return Layout(tuple(groups), _round_up(row, TM))


def _rms(x, g):
    return x * lax.rsqrt(jnp.mean(x * x, axis=-1, keepdims=True) + EPS) * g


def _swiglu_half(xn, wg_ref, wu_ref, wd_ref):
    acc = None
    for lo, hi in ((0, FF_SPLIT), (FF_SPLIT, D_FF)):
        gate = jnp.dot(xn, wg_ref[:, lo:hi], preferred_element_type=jnp.float32)
        up = jnp.dot(xn, wu_ref[:, lo:hi], preferred_element_type=jnp.float32)
        hm = (gate * jax.nn.sigmoid(gate) * up).astype(jnp.bfloat16)
        part = jnp.dot(hm, wd_ref[lo:hi, :], preferred_element_type=jnp.float32)
        acc = part if acc is None else acc + part
    return 0.5 * acc


def _rope_chunk(x, c, s, half, period):
    lane = lax.broadcasted_iota(jnp.int32, x.shape, 1)
    lo = (lane & (period - 1)) < half
    partner = jnp.where(lo, pltpu.roll(x, LANES - half, 1), pltpu.roll(x, half, 1))
    return x * c + partner * s


def _segment_specs(bounds):
    return [pl.BlockSpec((TM, D_MODEL), lambda i, lo=lo, hi=hi: (jnp.clip(i - lo, 0, hi - lo - 1), 0))
            for lo, hi in zip(bounds[:-1], bounds[1:])]


def _ffn_inproj_kernel(*refs, bounds):
    n_src = len(bounds) - 1
    srcs = refs[:n_src]
    (cda_ref, sda_ref, cm_ref, sm_ref, ga_ref, gb_ref,
     wg_ref, wu_ref, wd_ref, win_ref, gq_ref, gkv_ref, wuq_ref, wukv_ref,
     h1_ref, naq_ref, nak_ref, nav_ref, daq_ref, dak_ref, dav_ref,
     mq_ref, mk_ref, mv_ref, xn2_ref) = refs[n_src:]
    i = pl.program_id(0)

    @pl.when(i == 0)
    def _():
        xn2_ref[...] = jnp.zeros(xn2_ref.shape, xn2_ref.dtype)

    xn2 = xn2_ref[...]

    u = jnp.dot(xn2, win_ref[:, 0:3 * NA_W], preferred_element_type=jnp.float32)
    naq_ref[0] = (u[:, 0:NA_W] * (NA_HD ** -0.5 * LOG2E)).T.astype(jnp.bfloat16)
    nak_ref[...] = u[:, NA_W:2 * NA_W].astype(jnp.bfloat16)
    vt = u[:, 2 * NA_W:3 * NA_W].T.astype(jnp.bfloat16)
    for c in range(TM // LANES):
        nav_ref[c] = vt[:, c * LANES:(c + 1) * LANES]

    o0 = 3 * NA_W
    u = jnp.dot(xn2, win_ref[:, o0:o0 + 2 * DA_QK_W + DA_W], preferred_element_type=jnp.float32)
    cda, sda = cda_ref[...], sda_ref[...]
    q = jnp.concatenate([_rope_chunk(u[:, c * LANES:(c + 1) * LANES], cda, sda, DA_ROT // 2, DA_HD)
                         for c in range(DA_QK_W // LANES)], axis=1)
    k = jnp.concatenate([_rope_chunk(u[:, DA_QK_W + c * LANES:DA_QK_W + (c + 1) * LANES], cda, sda, DA_ROT // 2, DA_HD)
                         for c in range(DA_QK_W // LANES)], axis=1)
    daq_ref[0] = (q * (DA_HD ** -0.5 * LOG2E)).T.astype(jnp.bfloat16)
    dak_ref[...] = k.astype(jnp.bfloat16)
    vt = u[:, 2 * DA_QK_W:2 * DA_QK_W + DA_W].T.astype(jnp.bfloat16)
    for c in range(TM // TK):
        dav_ref[c] = vt[:, c * TK:(c + 1) * TK]

    o1 = o0 + 2 * DA_QK_W + DA_W
    u = jnp.dot(xn2, win_ref[:, o1:IN_W_PAD], preferred_element_type=jnp.float32)
    cm, sm = cm_ref[...], sm_ref[...]
    lane = lax.broadcasted_iota(jnp.int32, cm.shape, 1)
    band = (lane >= MLA_NOPE) & (lane < MLA_NOPE + MLA_ROPE)
    cb, sb = jnp.where(band, cm, 1.0), jnp.where(band, sm, 0.0)
    cq = _rms(u[:, 0:Q_LORA], gq_ref[...]).astype(jnp.bfloat16)
    qm = jnp.dot(cq, wuq_ref[...], preferred_element_type=jnp.float32)
    qm = jnp.concatenate([_rope_chunk(qm[:, h * LANES:(h + 1) * LANES], cb, sb, MLA_ROPE // 2, MLA_ROPE)
                          for h in range(MLA_HEADS)], axis=1) * ((MLA_NOPE + MLA_ROPE) ** -0.5 * LOG2E)
    mq_ref[0] = qm.T.astype(jnp.bfloat16)
    ckv = _rms(u[:, Q_LORA:Q_LORA + KV_LORA], gkv_ref[...]).astype(jnp.bfloat16)
    kv = jnp.dot(ckv, wukv_ref[...], preferred_element_type=jnp.float32)
    kr = _rope_chunk(u[:, Q_LORA + KV_LORA:Q_LORA + KV_LORA + LANES], cm, sm, MLA_ROPE // 2, MLA_ROPE)
    kr = jnp.where(band, kr, 0.0)
    kf = jnp.concatenate([kv[:, h * LANES:(h + 1) * LANES] + kr for h in range(MLA_HEADS)], axis=1)
    mk_ref[...] = kf.astype(jnp.bfloat16)
    vt = kv[:, MLA_HEADS * LANES:MLA_HEADS * LANES + MLA_W].T.astype(jnp.bfloat16)
    for c in range(TM // TK):
        mv_ref[c] = vt[:, c * TK:(c + 1) * TK]

    x = srcs[-1][...]
    for sgm in range(n_src - 2, -1, -1):
        x = jnp.where(i < bounds[sgm + 1], srcs[sgm][...], x)
    xn = _rms(x, ga_ref[...]).astype(jnp.bfloat16)
    h1 = x + _swiglu_half(xn, wg_ref, wu_ref, wd_ref)
    h1_ref[...] = h1
    xn2_ref[...] = _rms(h1, gb_ref[...]).astype(jnp.bfloat16)


def _outproj_ffn_kernel(h_ref, ona_ref, oda_ref, om_ref, wo1_ref, wo2_ref, wo3_ref, g_ref,
                        wg_ref, wu_ref, wd_ref, gf_ref, *o_refs, out_bounds):
    h = h_ref[...]
    h = h + jnp.dot(ona_ref[...], wo1_ref[...], preferred_element_type=jnp.float32)
    h = h + jnp.dot(oda_ref[...], wo2_ref[...], preferred_element_type=jnp.float32)
    h = h + jnp.dot(om_ref[...], wo3_ref[...], preferred_element_type=jnp.float32)
    xn = _rms(h, g_ref[...]).astype(jnp.bfloat16)
    h = h + _swiglu_half(xn, wg_ref, wu_ref, wd_ref)
    if out_bounds is None:
        o_refs[0][...] = h
    else:
        h = _rms(h, gf_ref[...])
        i = pl.program_id(0)
        for o_ref, lo, hi in zip(o_refs, out_bounds[:-1], out_bounds[1:]):
            @pl.when((i >= lo) & (i < hi))
            def _(o_ref=o_ref):
                o_ref[...] = h


def _const_spec(shape):
    nd = len(shape)
    return pl.BlockSpec(shape, lambda i: (0,) * nd, pipeline_mode=pl.Buffered(1))


def _ffn_inproj(srcs, bounds, tabs, tab_block, ga, gb, wg, wu, wd, win, gq, gkv, wuq, wukv):
    nt = bounds[-1]
    rows = nt * TM
    prev = lambda i: jnp.maximum(i - 1, 0)
    row_spec = lambda w: pl.BlockSpec((TM, w), lambda i: (prev(i), 0))
    tile_spec = lambda r: pl.BlockSpec((1, r, TM), lambda i: (prev(i), 0, 0))
    bf = jnp.bfloat16
    out_shape = (
        jax.ShapeDtypeStruct((rows, D_MODEL), jnp.float32),
        jax.ShapeDtypeStruct((nt, NA_W, TM), bf),
        jax.ShapeDtypeStruct((rows, NA_W), bf),
        jax.ShapeDtypeStruct((rows // LANES, NA_W, LANES), bf),
        jax.ShapeDtypeStruct((nt, DA_QK_W, TM), bf),
        jax.ShapeDtypeStruct((rows, DA_QK_W), bf),
        jax.ShapeDtypeStruct((rows // TK, DA_W, TK), bf),
        jax.ShapeDtypeStruct((nt, MLA_HEADS * LANES, TM), bf),
        jax.ShapeDtypeStruct((rows, MLA_QK_W), bf),
        jax.ShapeDtypeStruct((rows // TK, MLA_W, TK), bf),
    )
    out_specs = (
        pl.BlockSpec((TM, D_MODEL), lambda i: (jnp.minimum(i, nt - 1), 0)),
        tile_spec(NA_W),
        row_spec(NA_W),
        pl.BlockSpec((TM // LANES, NA_W, LANES), lambda i: (prev(i), 0, 0)),
        tile_spec(DA_QK_W),
        row_spec(DA_QK_W),
        pl.BlockSpec((TM // TK, DA_W, TK), lambda i: (prev(i), 0, 0)),
        tile_spec(MLA_HEADS * LANES),
        row_spec(MLA_QK_W),
        pl.BlockSpec((TM // TK, MLA_W, TK), lambda i: (prev(i), 0, 0)),
    )
    in_specs = _segment_specs(bounds) + [pl.BlockSpec((TM, LANES), lambda i: (tab_block(prev(i)), 0))] * 4 + [
        _const_spec(a.shape) for a in (ga, gb, wg, wu, wd, win, gq, gkv, wuq, wukv)]
    return pl.pallas_call(
        functools.partial(_ffn_inproj_kernel, bounds=bounds),
        grid=(nt + 1,),
        in_specs=in_specs,
        out_specs=out_specs,
        out_shape=out_shape,
        scratch_shapes=[pltpu.VMEM((TM, D_MODEL), jnp.bfloat16)],
        compiler_params=pltpu.CompilerParams(dimension_semantics=("arbitrary",),
                                             vmem_limit_bytes=VMEM_LIMIT_BYTES),
        name="ffn_inproj",
    )(*srcs, *tabs, ga, gb, wg, wu, wd, win, gq, gkv, wuq, wukv)


def _outproj_ffn(h, ona, oda, om, wo1, wo2, wo3, g, wg, wu, wd, gf, out_bounds):
    rows = h.shape[0]
    row_spec = lambda w: pl.BlockSpec((TM, w), lambda i: (i, 0))
    if out_bounds is None:
        out_specs = row_spec(D_MODEL)
        out_shape = jax.ShapeDtypeStruct((rows, D_MODEL), jnp.float32)
    else:
        out_specs = tuple(_segment_specs(out_bounds))
        out_shape = tuple(jax.ShapeDtypeStruct(((hi - lo) * TM, D_MODEL), jnp.float32)
                          for lo, hi in zip(out_bounds[:-1], out_bounds[1:]))
    in_specs = [row_spec(D_MODEL), row_spec(NA_W), row_spec(DA_W), row_spec(MLA_W)] + [
        _const_spec(a.shape) for a in (wo1, wo2, wo3, g, wg, wu, wd, gf)]
    return pl.pallas_call(
        functools.partial(_outproj_ffn_kernel, out_bounds=out_bounds),
        grid=(rows // TM,),
        in_specs=in_specs,
        out_specs=out_specs,
        out_shape=out_shape,
        compiler_params=pltpu.CompilerParams(dimension_semantics=("arbitrary",),
                                             vmem_limit_bytes=VMEM_LIMIT_BYTES),
        name="outproj_ffn",
    )(h, ona, oda, om, wo1, wo2, wo3, g, wg, wu, wd, gf)


def _row_band(block, lo, hi):
    row = lax.broadcasted_iota(jnp.int32, block.shape, 0)
    return jnp.where((row >= lo) & (row < hi), block, jnp.zeros_like(block))


def _with_ones(vt):
    return jnp.concatenate([vt, jnp.ones((ONES_ROWS, vt.shape[1]), vt.dtype)], axis=0)


def _meta_key_mask(b):
    row = lax.broadcasted_iota(jnp.int32, (LANES, 1), 0)
    lo = (b % (LANES // N_META)) * N_META
    return jnp.where((row >= lo) & (row < lo + N_META), 0.0, NEG).astype(jnp.float32)


class FlashCfg(NamedTuple):
    n_soft: int
    group_w: int
    soft_per_group: int
    v_of_soft: tuple
    out_w: int
    unroll: int


DA_CFG = FlashCfg(2 * DA_HEADS, LANES, 4, tuple(i // 2 for i in range(2 * DA_HEADS)), DA_W, 16)
MLA_CFG = FlashCfg(MLA_HEADS, LANES, 1, tuple(range(MLA_HEADS)), MLA_W, 32)
NA_CFG = FlashCfg(NA_HEADS, LANES, 2, tuple(range(NA_HEADS)), NA_W, 1)


def _build_rhs(cfg, qt_ref, rhs_ref):
    for i in range(cfg.n_soft):
        g, j = i // cfg.soft_per_group, i % cfg.soft_per_group
        if cfg is DA_CFG:
            rhs_ref[i] = _row_band(qt_ref[0, g * LANES:(g + 1) * LANES, :], j * DA_HD, (j + 1) * DA_HD)
        else:
            rhs_ref[i] = qt_ref[0, i * LANES:(i + 1) * LANES, :]


def _flash_tiles(cfg, tiles, rhs_ref, m_ref, acc_ref, bounded):
    nt = len(tiles)
    if bounded:
        work = [(t, i) for i in range(cfg.n_soft) for t in range(nt)]
    else:
        work = [(t, i) for t in range(nt) for i in range(cfg.n_soft)]

    def score(w):
        kblk_of, _, mask = tiles[w[0]]
        s = jnp.dot(kblk_of(w[1] // cfg.soft_per_group), rhs_ref[w[1]], preferred_element_type=jnp.float32)
        return s if mask is None else s + mask

    pending = [score(w) for w in work[:LOOKAHEAD]]
    part = den = None
    for n, (t, i) in enumerate(work):
        vt = tiles[t][1](cfg.v_of_soft[i])
        s = pending.pop(0)
        if n + LOOKAHEAD < len(work):
            pending.append(score(work[n + LOOKAHEAD]))
        if bounded:
            p = jnp.exp2(s)
            pv = jnp.dot(vt, p.astype(jnp.bfloat16), preferred_element_type=jnp.float32)
            ps = jnp.sum(p, axis=0, keepdims=True)
            part, den = (pv, ps) if t == 0 else (part + pv, den + ps)
            if t == nt - 1:
                acc_ref[i, 0:DA_VD, :] = acc_ref[i, 0:DA_VD, :] + part
                acc_ref[i, DA_VD:DA_VD + 1, :] = acc_ref[i, DA_VD:DA_VD + 1, :] + den
        else:
            vt = _with_ones(vt)
            m_prev = m_ref[i]
            m_new = jnp.maximum(m_prev, jnp.max(s, axis=0, keepdims=True))
            alpha = jnp.exp2(m_prev - m_new)
            p = jnp.exp2(s - m_new).astype(jnp.bfloat16)
            acc_ref[i] = alpha * acc_ref[i] + jnp.dot(vt, p, preferred_element_type=jnp.float32)
            m_ref[i] = m_new


def _feature_indicator(cfg):
    f = lax.broadcasted_iota(jnp.int32, (cfg.group_w, LANES), 0)
    j = lax.broadcasted_iota(jnp.int32, (cfg.group_w, LANES), 1)
    band = cfg.group_w // cfg.soft_per_group
    hit = jnp.right_shift(f, int(math.log2(band))) == j
    return jnp.where(hit & (j < cfg.soft_per_group), 1.0, 0.0).astype(jnp.bfloat16)


def _max_key_norms(cfg, kf_ref, kfm_ref, n_keys, kn_ref):
    gw = cfg.group_w
    n_groups = cfg.n_soft // cfg.soft_per_group
    ind = _feature_indicator(cfg)

    def sq_norms(blk):
        x = blk.astype(jnp.float32)
        n2 = jnp.dot((x * x).astype(jnp.bfloat16), ind, preferred_element_type=jnp.float32)
        return jnp.max(n2, axis=0, keepdims=True)

    def body(c, carry):
        r0 = pl.multiple_of(c * TM, TM)
        return tuple(jnp.maximum(carry[g], sq_norms(kf_ref[pl.ds(r0, TM), g * gw:(g + 1) * gw]))
                     for g in range(n_groups))

    init = tuple(sq_norms(kfm_ref[:, g * gw:(g + 1) * gw]) for g in range(n_groups))
    kmax = lax.fori_loop(0, n_keys // TM, body, init)
    lane = lax.broadcasted_iota(jnp.int32, (1, LANES), 1)
    for i in range(cfg.n_soft):
        g, j = divmod(i, cfg.soft_per_group)
        v = jnp.max(jnp.where(lane == j, kmax[g], 0.0), axis=1, keepdims=True)
        kn_ref[i] = jnp.broadcast_to(v, (1, LANES))


def _scores_are_bounded(cfg, rhs_ref, kn_ref):
    ok = None
    for i in range(cfg.n_soft):
        r = rhs_ref[i].astype(jnp.float32)
        bound2 = jnp.sum(r * r, axis=0, keepdims=True) * kn_ref[i][:, 0:1] * NORM_SLACK
        good = bound2 <= SCORE_BOUND * SCORE_BOUND
        ok = good if ok is None else ok & good
    return jnp.min(jnp.where(ok, 1.0, 0.0)) > 0.5


def _flash_kernel(*refs, cfg, n_keys, lam_init, meta_queries, meta_blk0):
    if cfg is DA_CFG:
        (qt_ref, kf_ref, vt_ref, kfm_ref, vtm_ref, lam_ref, gs_ref, o_ref,
         rhs_ref, m_ref, acc_ref, kn_ref, *rest) = refs
    else:
        qt_ref, kf_ref, vt_ref, kfm_ref, vtm_ref, o_ref, rhs_ref, m_ref, acc_ref, kn_ref, *rest = refs
    b = pl.program_id(0)
    tq = qt_ref.shape[2]
    vd = DA_VD
    gw = cfg.group_w
    if meta_queries:
        _max_key_norms(cfg, kf_ref, kfm_ref, n_keys, kn_ref)
    else:
        @pl.when(pl.program_id(1) == 0)
        def _():
            _max_key_norms(cfg, kf_ref, kfm_ref, n_keys, kn_ref)
    _build_rhs(cfg, qt_ref, rhs_ref)
    m_ref[...] = jnp.full(m_ref.shape, NEG, jnp.float32)
    acc_ref[...] = jnp.zeros(acc_ref.shape, jnp.float32)

    def all_keys(bounded, unroll):
        _flash_tiles(
            cfg,
            [(lambda g: kfm_ref[:, g * gw:(g + 1) * gw],
              lambda h: vtm_ref[0, h * vd:(h + 1) * vd, :],
              _meta_key_mask(b))],
            rhs_ref, m_ref, acc_ref, bounded)

        def body(it, carry):
            tiles = []
            for u in range(unroll):
                kt = it * unroll + u
                k0 = pl.multiple_of(kt * TK, TK)
                tiles.append((
                    lambda g, k0=k0: kf_ref[pl.ds(k0, TK), g * gw:(g + 1) * gw],
                    lambda h, kt=kt: vt_ref[kt, h * vd:(h + 1) * vd, :],
                    None))
            _flash_tiles(cfg, tiles, rhs_ref, m_ref, acc_ref, bounded)
            return carry

        lax.fori_loop(0, n_keys // (TK * unroll), body, 0)

    lax.cond(_scores_are_bounded(cfg, rhs_ref, kn_ref),
             lambda: all_keys(True, math.gcd(cfg.unroll, n_keys // TK)),
             lambda: all_keys(False, 1))

    heads = []
    if cfg is DA_CFG:
        lp = lam_ref[...]
        lam = (jnp.exp(jnp.sum(lp[0:1] * lp[1:2], axis=1, keepdims=True))
               - jnp.exp(jnp.sum(lp[2:3] * lp[3:4], axis=1, keepdims=True)) + lam_init)
        for h in range(DA_HEADS):
            a1, a2 = acc_ref[2 * h], acc_ref[2 * h + 1]
            o = a1[0:vd] / a1[vd:vd + 1] - lam * (a2[0:vd] / a2[vd:vd + 1])
            o = o * lax.rsqrt(jnp.mean(o * o, axis=0, keepdims=True) + EPS) * gs_ref[...]
            heads.append(o * (1.0 - lam_init))
    else:
        for h in range(MLA_HEADS):
            a = acc_ref[h]
            heads.append(a[0:vd] / a[vd:vd + 1])
    pairs = [jnp.concatenate(heads[2 * g:2 * g + 2], axis=0).T for g in range(len(heads) // 2)]
    out = jnp.concatenate(pairs, axis=1)
    if meta_queries:
        stage_ref = rest[0]
        stage_ref[...] = out
        per_blk = LANES // N_META
        lane_blk = (meta_blk0 + b // per_blk) % (META_TQ // LANES)
        r0 = pl.multiple_of(lane_blk * LANES + (b % per_blk) * N_META, N_META)
        o_ref[...] = stage_ref[pl.ds(r0, N_META), :].astype(o_ref.dtype)
    else:
        o_ref[...] = out.astype(o_ref.dtype)


def _flash_call(cfg, grp, qt, kf, vt, extra, o_prev, lam_init, meta_queries, name):
    n, bsz = grp.n, grp.batch
    qw, kw, vw = qt.shape[1], kf.shape[1], vt.shape[1]
    tiles_per_seq = n // TM
    meta_blk0 = grp.meta0 // LANES
    per_blk = LANES // N_META
    if meta_queries:
        grid = (bsz,)
        tq = META_TQ
        q_map = lambda b: ((meta_blk0 + b // per_blk) // (TM // LANES), 0,
                           ((meta_blk0 + b // per_blk) % (TM // LANES)) // (META_TQ // LANES))
        o_spec = pl.BlockSpec((N_META, cfg.out_w), lambda b: (grp.meta0 // N_META + b, 0))
        fix = lambda f: (lambda b: f(b))
        sem = ("arbitrary",)
    else:
        grid = (bsz, tiles_per_seq)
        tq = TM
        q_map = lambda b, i: (grp.real0 // TM + b * tiles_per_seq + i, 0, 0)
        o_spec = pl.BlockSpec((TM, cfg.out_w), lambda b, i: (grp.real0 // TM + b * tiles_per_seq + i, 0))
        fix = lambda f: (lambda b, i: f(b))
        sem = ("arbitrary", "arbitrary")
    in_specs = [
        pl.BlockSpec((1, qw, tq), q_map),
        pl.BlockSpec((n, kw), fix(lambda b: (grp.real0 // n + b, 0))),
        pl.BlockSpec((n // TK, vw, TK), fix(lambda b: (grp.real0 // n + b, 0, 0))),
        pl.BlockSpec((LANES, kw), fix(lambda b: (meta_blk0 + b // per_blk, 0))),
        pl.BlockSpec((1, vw, LANES), fix(lambda b: ((meta_blk0 + b // per_blk) // (TK // LANES), 0,
                                                     (meta_blk0 + b // per_blk) % (TK // LANES)))),
    ]
    args = [qt, kf, vt, kf, vt]
    for a in extra:
        in_specs.append(pl.BlockSpec(a.shape, fix(lambda b, nd=a.ndim: (0,) * nd)))
        args.append(a)
    in_specs.append(pl.BlockSpec(memory_space=pl.ANY))
    args.append(o_prev)
    scratch = [
        pltpu.VMEM((cfg.n_soft, cfg.group_w, tq), jnp.bfloat16),
        pltpu.VMEM((cfg.n_soft, 1, tq), jnp.float32),
        pltpu.VMEM((cfg.n_soft, DA_VD + ONES_ROWS, tq), jnp.float32),
        pltpu.VMEM((cfg.n_soft, 1, LANES), jnp.float32),
    ]
    if meta_queries:
        scratch.append(pltpu.VMEM((META_TQ, cfg.out_w), jnp.float32))

    def body(*refs):
        n_in = len(args)
        ins, rest = refs[:n_in - 1], refs[n_in:]
        _flash_kernel(*ins, *rest, cfg=cfg, n_keys=n, lam_init=lam_init, meta_queries=meta_queries,
                      meta_blk0=meta_blk0)

    return pl.pallas_call(
        body,
        grid=grid,
        in_specs=in_specs,
        out_specs=o_spec,
        out_shape=jax.ShapeDtypeStruct(o_prev.shape, o_prev.dtype),
        scratch_shapes=scratch,
        input_output_aliases={len(args) - 1: 0},
        compiler_params=pltpu.CompilerParams(dimension_semantics=sem, vmem_limit_bytes=VMEM_LIMIT_BYTES),
        name=name,
    )(*args)


def _na_kernel(qt_ref, kf_ref, vt_ref, kfm_ref, vtm_ref, bias_ref, bmax_ref, prev_ref, o_ref, kn_ref, *, grid_rows):
    del prev_ref
    b, i = pl.program_id(0), pl.program_id(1)

    @pl.when(i == 0)
    def _():
        _max_key_norms(NA_CFG, kf_ref, kfm_ref, grid_rows * GRID_W, kn_ref)

    ws = jnp.clip(i * NA_QROWS - WIN_R // 2, 0, grid_rows - NA_KROWS)
    kwin = kf_ref[pl.ds(pl.multiple_of(ws * GRID_W, 2 * GRID_W), NA_KROWS * GRID_W), :]
    vwin = vt_ref[pl.ds(ws // 2, NA_KROWS // 2)]
    kmeta = kfm_ref[...]
    mmask = _meta_key_mask(b)

    def rhs_of(h):
        g, j = h // 2, h % 2
        return _row_band(qt_ref[0, g * LANES:(g + 1) * LANES, :], j * NA_HD, (j + 1) * NA_HD)

    def scores(h):
        g, rhs = h // 2, rhs_of(h)
        s1 = jnp.dot(kwin[:, g * LANES:(g + 1) * LANES], rhs, preferred_element_type=jnp.float32) + bias_ref[0, h]
        s2 = jnp.dot(kmeta[:, g * LANES:(g + 1) * LANES], rhs, preferred_element_type=jnp.float32) + mmask
        return s1, s2

    def run(bounded):
        heads = []
        pending = [scores(h) for h in range(LOOKAHEAD)]
        for h in range(NA_HEADS):
            s1, s2 = pending.pop(0)
            if h + LOOKAHEAD < NA_HEADS:
                pending.append(scores(h + LOOKAHEAD))
            if not bounded:
                m = jnp.maximum(jnp.max(s1, axis=0, keepdims=True), jnp.max(s2, axis=0, keepdims=True))
                s1, s2 = s1 - m, s2 - m
            p1 = jnp.exp2(s1).astype(jnp.bfloat16)
            p2 = jnp.exp2(s2).astype(jnp.bfloat16)
            v1 = jnp.concatenate([vwin[c, h * NA_HD:(h + 1) * NA_HD, :] for c in range(NA_KROWS // 2)], axis=1)
            acc = (jnp.dot(_with_ones(v1), p1, preferred_element_type=jnp.float32)
                   + jnp.dot(_with_ones(vtm_ref[0, h * NA_HD:(h + 1) * NA_HD, :]), p2,
                             preferred_element_type=jnp.float32))
            heads.append(acc[0:NA_HD] / acc[NA_HD:NA_HD + 1])
        pairs = [jnp.concatenate(heads[2 * g:2 * g + 2], axis=0).T for g in range(NA_HEADS // 2)]
        o_ref[...] = jnp.concatenate(pairs, axis=1).astype(o_ref.dtype)

    limit = SCORE_BOUND - bmax_ref[...]
    ok = None
    for h in range(NA_HEADS):
        r = rhs_of(h).astype(jnp.float32)
        bound2 = jnp.sum(r * r, axis=0, keepdims=True) * kn_ref[h][:, 0:1] * NORM_SLACK
        good = (limit > 0.0) & (bound2 <= limit * limit)
        ok = good if ok is None else ok & good
    lax.cond(jnp.min(jnp.where(ok, 1.0, 0.0)) > 0.5, lambda: run(True), lambda: run(False))


def _na_meta_kernel(qt_ref, kf_ref, vt_ref, prev_ref, o_ref):
    del prev_ref
    kr = lax.broadcasted_iota(jnp.int32, (LANES, LANES), 0) // N_META
    qc = lax.broadcasted_iota(jnp.int32, (LANES, LANES), 1) // N_META
    mask = jnp.where(kr == qc, 0.0, NEG).astype(jnp.float32)
    kf = kf_ref[...]
    heads = []
    for h in range(NA_HEADS):
        g, j = h // 2, h % 2
        rhs = _row_band(qt_ref[0, g * LANES:(g + 1) * LANES, :], j * NA_HD, (j + 1) * NA_HD)
        s = jnp.dot(kf[:, g * LANES:(g + 1) * LANES], rhs, preferred_element_type=jnp.float32) + mask
        p = jnp.exp2(s - jnp.max(s, axis=0, keepdims=True)).astype(jnp.bfloat16)
        acc = jnp.dot(_with_ones(vt_ref[0, h * NA_HD:(h + 1) * NA_HD, :]), p, preferred_element_type=jnp.float32)
        heads.append(acc[0:NA_HD] / acc[NA_HD:NA_HD + 1])
    pairs = [jnp.concatenate(heads[2 * g:2 * g + 2], axis=0).T for g in range(NA_HEADS // 2)]
    o_ref[...] = jnp.concatenate(pairs, axis=1).astype(o_ref.dtype)


def _na_call(grp, qt, kf, vt, bias, bmax, o_prev):
    n, bsz = grp.n, grp.batch
    grid_rows = n // GRID_W
    steps = grid_rows // NA_QROWS
    tq = NA_QROWS * GRID_W
    meta_blk0 = grp.meta0 // LANES
    per_blk = LANES // N_META
    in_specs = [
        pl.BlockSpec((1, NA_W, tq), lambda b, i: (grp.real0 // TM + b * (n // TM) + i // (TM // tq), 0, i % (TM // tq))),
        pl.BlockSpec((n, NA_W), lambda b, i: (grp.real0 // n + b, 0)),
        pl.BlockSpec((n // LANES, NA_W, LANES), lambda b, i: (grp.real0 // n + b, 0, 0)),
        pl.BlockSpec((LANES, NA_W), lambda b, i: (meta_blk0 + b // per_blk, 0)),
        pl.BlockSpec((1, NA_W, LANES), lambda b, i: (meta_blk0 + b // per_blk, 0, 0)),
        pl.BlockSpec((1, NA_HEADS, NA_KROWS * GRID_W, tq),
                     lambda b, i: (jnp.where(i == 0, 0, jnp.where(i == steps - 1, 2, 1)), 0, 0, 0)),
        pl.BlockSpec((1, 1), lambda b, i: (0, 0)),
        pl.BlockSpec(memory_space=pl.ANY),
    ]
    return pl.pallas_call(
        functools.partial(_na_kernel, grid_rows=grid_rows),
        grid=(bsz, steps),
        in_specs=in_specs,
        out_specs=pl.BlockSpec((tq, NA_W), lambda b, i: (grp.real0 // tq + b * steps + i, 0)),
        out_shape=jax.ShapeDtypeStruct(o_prev.shape, o_prev.dtype),
        scratch_shapes=[pltpu.VMEM((NA_HEADS, 1, LANES), jnp.float32)],
        input_output_aliases={7: 0},
        compiler_params=pltpu.CompilerParams(dimension_semantics=("arbitrary", "arbitrary"),
                                             vmem_limit_bytes=VMEM_LIMIT_BYTES),
        name="na_real",
    )(qt, kf, vt, kf, vt, bias, bmax, o_prev)


def _na_meta_call(layout, qt, kf, vt, o_prev):
    blk0 = layout.groups[0].meta0 // LANES
    nblk = sum(_round_up(g.batch * N_META, LANES) for g in layout.groups) // LANES
    sub = TM // LANES
    return pl.pallas_call(
        _na_meta_kernel,
        grid=(nblk,),
        in_specs=[
            pl.BlockSpec((1, NA_W, LANES), lambda i: ((blk0 + i) // sub, 0, (blk0 + i) % sub)),
            pl.BlockSpec((LANES, NA_W), lambda i: (blk0 + i, 0)),
            pl.BlockSpec((1, NA_W, LANES), lambda i: (blk0 + i, 0, 0)),
            pl.BlockSpec(memory_space=pl.ANY),
        ],
        out_specs=pl.BlockSpec((LANES, NA_W), lambda i: (blk0 + i, 0)),
        out_shape=jax.ShapeDtypeStruct(o_prev.shape, o_prev.dtype),
        input_output_aliases={3: 0},
        compiler_params=pltpu.CompilerParams(dimension_semantics=("arbitrary",),
                                             vmem_limit_bytes=VMEM_LIMIT_BYTES),
        name="na_meta",
    )(qt, kf, vt, o_prev)


def _na_bias_tables(rel_bias):
    kc = np.arange(GRID_W)[:, None]
    qc = np.arange(GRID_W)[None, :]
    cs = np.clip(qc - WIN_C // 2, 0, GRID_W - WIN_C)
    col_ok = (kc >= cs) & (kc < cs + WIN_C)
    cidx = np.clip(kc - qc + WIN_C - 1, 0, 2 * WIN_C - 2)
    planes = jnp.where(col_ok[None, None], rel_bias[:, :, cidx] * LOG2E, NEG)
    masked = 2 * WIN_R - 1
    planes = jnp.concatenate([planes, jnp.full((NA_HEADS, 1, GRID_W, GRID_W), NEG, planes.dtype)], axis=1)
    kr = np.arange(NA_KROWS)[:, None]
    qr = np.arange(NA_QROWS)[None, :]
    ridx = []
    for win_lo, q_off in ((0 * qr, qr), (qr, qr + WIN_R // 2), (0 * qr + NA_KROWS - WIN_R, qr + NA_KROWS - NA_QROWS)):
        row_ok = (kr >= win_lo) & (kr < win_lo + WIN_R)
        ridx.append(np.where(row_ok, kr - q_off + WIN_R - 1, masked))
    ridx = np.stack(ridx)
    assert ridx.min() >= 0 and ridx.max() <= masked
    t = planes[:, ridx]
    t = jnp.transpose(t, (1, 0, 2, 4, 3, 5))
    return t.reshape(3, NA_HEADS, NA_KROWS * GRID_W, NA_QROWS * GRID_W).astype(jnp.float32)


def _rope_tables(layout):
    pos, starts, periods = [], [], []
    for g in layout.groups:
        starts.append(len(pos))
        periods.append(g.n // TM)
        for t in range(g.n // TM):
            pos.append(N_META + t * TM + np.arange(TM))
    tail0 = sum(g.batch * g.n for g in layout.groups)
    starts.append(len(pos))
    for t in range(tail0 // TM, layout.rows // TM):
        r = t * TM + np.arange(TM)
        p = np.zeros((TM,), np.int64)
        for g in layout.groups:
            inside = (r >= g.meta0) & (r < g.meta0 + g.batch * N_META)
            p = np.where(inside, (r - g.meta0) % N_META, p)
        pos.append(p)
    pos = jnp.asarray(np.concatenate(pos).astype(np.float32))
    bounds = [g.real0 // TM for g in layout.groups] + [tail0 // TM]

    def block_of_tile(i):
        blk = starts[-1] + (i - bounds[-1])
        for k in range(len(layout.groups) - 1, -1, -1):
            blk = jnp.where(i < bounds[k + 1], starts[k] + (i - bounds[k]) % periods[k], blk)
        return blk

    def tables(dim, period):
        inv = ROPE_THETA ** (-(jnp.arange(0, dim, 2, dtype=jnp.float32) / dim))
        ang = pos[:, None] * inv[None, :]
        cos, sin = jnp.cos(ang), jnp.sin(ang)
        rest = period - dim
        c = jnp.concatenate([cos, cos, jnp.ones((pos.shape[0], rest), jnp.float32)], axis=1)
        s = jnp.concatenate([-sin, sin, jnp.zeros((pos.shape[0], rest), jnp.float32)], axis=1)
        reps = LANES // period
        return jnp.tile(c, (1, reps)), jnp.tile(s, (1, reps))

    cda, sda = tables(DA_ROT, DA_HD)
    cm, sm = tables(MLA_ROPE, MLA_ROPE)
    return (cda, sda, cm, sm), block_of_tile


def _prep_layer(l, norm_g, ffn_w_gate, ffn_w_up, ffn_w_down, w_in, w_out, mla_q_norm_g, mla_kv_norm_g,
                mla_w_uq, mla_w_ukv, da_subln_g):
    bf = jnp.bfloat16
    kr0 = IN_W - MLA_ROPE
    win = jnp.concatenate([w_in[l, :, :kr0]] + [w_in[l, :, kr0:]] * (LANES // MLA_ROPE), axis=1).astype(bf)
    uq = mla_w_uq[l].reshape(Q_LORA, MLA_HEADS, MLA_NOPE + MLA_ROPE)
    uq = jnp.pad(uq, ((0, 0), (0, 0), (0, LANES - MLA_NOPE - MLA_ROPE))).reshape(Q_LORA, MLA_HEADS * LANES)
    ukv = mla_w_ukv[l].reshape(KV_LORA, MLA_HEADS, MLA_NOPE + MLA_VD)
    uk = jnp.pad(ukv[:, :, :MLA_NOPE], ((0, 0), (0, 0), (0, LANES - MLA_NOPE))).reshape(KV_LORA, MLA_HEADS * LANES)
    ukv = jnp.concatenate([uk, ukv[:, :, MLA_NOPE:].reshape(KV_LORA, -1)], axis=1)
    row = lambda v: v.reshape(1, -1)
    return dict(
        g=[row(norm_g[l, i]) for i in range(3)],
        ffn=[(ffn_w_gate[l, i].astype(bf), ffn_w_up[l, i].astype(bf), ffn_w_down[l, i].astype(bf)) for i in range(2)],
        win=win, uq=uq.astype(bf), ukv=ukv.astype(bf),
        gq=row(mla_q_norm_g[l]), gkv=row(mla_kv_norm_g[l]),
        wo=(w_out[l, :NA_W].astype(bf), w_out[l, NA_W:NA_W + DA_W].astype(bf), w_out[l, NA_W + DA_W:].astype(bf)),
        gs=da_subln_g[l].reshape(DA_VD, 1),
    )


def kernel(x_prompt, x_sample, meta_tokens, norm_g, final_norm_g, ffn_w_gate, ffn_w_up, ffn_w_down, w_in, w_out, na_rel_bias, da_lambda, da_subln_g, mla_q_norm_g, mla_kv_norm_g, mla_w_uq, mla_w_ukv):
    xs = (x_prompt, x_sample)
    layout = _make_layout([(x.shape[0], x.shape[1]) for x in xs])
    real_rows = sum(g.batch * g.n for g in layout.groups)
    tail = []
    for g in layout.groups:
        blk = jnp.tile(meta_tokens.astype(jnp.float32), (g.batch, 1))
        tail.append(jnp.pad(blk, ((0, _round_up(g.batch * N_META, LANES) - g.batch * N_META), (0, 0))))
    tail = jnp.concatenate(tail, axis=0)
    tail = jnp.pad(tail, ((0, layout.rows - real_rows - tail.shape[0]), (0, 0)))
    srcs = [x.reshape(-1, D_MODEL) for x in xs] + [tail]
    real_bounds = tuple(g.real0 // TM for g in layout.groups) + (real_rows // TM,)
    bounds = real_bounds + (layout.rows // TM,)
    tabs, tab_block = _rope_tables(layout)
    gf = final_norm_g.reshape(1, -1)

    o_na = jnp.zeros((layout.rows, NA_W), jnp.bfloat16)
    o_da = jnp.zeros((layout.rows, DA_W), jnp.bfloat16)
    o_m = jnp.zeros((layout.rows, MLA_W), jnp.bfloat16)
    for l in range(DEPTH):
        p = _prep_layer(l, norm_g, ffn_w_gate, ffn_w_up, ffn_w_down, w_in, w_out, mla_q_norm_g, mla_kv_norm_g,
                        mla_w_uq, mla_w_ukv, da_subln_g)
        lam_init = 0.8 - 0.6 * math.exp(-0.3 * l)
        (h, naq, nak, nav, daq, dak, dav, mq, mk, mv) = _ffn_inproj(
            srcs, bounds, tabs, tab_block, p["g"][0], p["g"][1], *p["ffn"][0], p["win"], p["gq"], p["gkv"], p["uq"], p["ukv"])
        bias = _na_bias_tables(na_rel_bias[l])
        bmax = (jnp.max(jnp.abs(na_rel_bias[l])) * LOG2E).reshape(1, 1)
        o_na = _na_meta_call(layout, naq, nak, nav, o_na)
        for gi, grp in enumerate(layout.groups):
            o_na = _na_call(grp, naq, nak, nav, bias, bmax, o_na)
            for meta_q in (False, True):
                tag = f"g{gi}_{'meta' if meta_q else 'real'}"
                o_da = _flash_call(DA_CFG, grp, daq, dak, dav, (da_lambda[l], p["gs"]), o_da, lam_init, meta_q,
                                   "da_" + tag)
                o_m = _flash_call(MLA_CFG, grp, mq, mk, mv, (), o_m, lam_init, meta_q, "mla_" + tag)
        last = l == DEPTH - 1
        h = _outproj_ffn(h, o_na, o_da, o_m, *p["wo"], p["g"][2], *p["ffn"][1], gf, real_bounds if last else None)
        srcs, bounds = [h], (0, layout.rows // TM)

    return tuple(y.reshape(x.shape) for x, y in zip(xs, h))
```

```python
import functools
import math
from typing import NamedTuple

import jax
import jax.numpy as jnp
import numpy as np
from jax import lax
from jax.experimental import pallas as pl
from jax.experimental.pallas import tpu as pltpu

D_MODEL = 1024
DEPTH = 2
GRID_W = 64
N_META = 16
WIN_R = 8
WIN_C = 16
NA_HEADS = 6
NA_HD = 64
DA_HEADS = 6
DA_HD = 32
DA_VD = 64
DA_ROT = DA_HD // 4
MLA_HEADS = 4
MLA_NOPE = 64
MLA_ROPE = 32
MLA_VD = 64
Q_LORA = 256
KV_LORA = 128
ROPE_THETA = 500000.0
D_FF = 2816
EPS = 1e-6
NA_W = NA_HEADS * NA_HD
DA_W = DA_HEADS * DA_VD
MLA_W = MLA_HEADS * MLA_VD
DA_QK_W = DA_HEADS * 2 * DA_HD
IN_W = 3 * NA_W + 2 * DA_QK_W + DA_W + Q_LORA + KV_LORA + MLA_ROPE

LANES = 128
MXU_DIM = 256
VMEM_LIMIT_BYTES = 60 * 1024 * 1024

TM = 512
TK = 256
NA_QROWS = 4
NA_KROWS = NA_QROWS + WIN_R
LOOKAHEAD = 2
FF_SPLIT = 1536
IN_W_PAD = IN_W - MLA_ROPE + LANES
MLA_QK_W = MLA_HEADS * LANES
assert MLA_VD == DA_VD and IN_W_PAD % LANES == 0

LOG2E = 1.4426950408889634
NEG = -1e30
ONES_ROWS = 16
SCORE_BOUND = 60.0
NORM_SLACK = 1.05


def _round_up(x, m):
    return (x + m - 1) // m * m


class Group(NamedTuple):
    batch: int
    n: int
    real0: int
    meta0: int


class Layout(NamedTuple):
    groups: tuple
    rows: int


def _make_layout(shapes):
    row = 0
    real0 = []
    for b, n in shapes:
        assert n % TM == 0 and row % n == 0 and n % GRID_W == 0
        assert (n // GRID_W) % NA_QROWS == 0 and n // GRID_W >= NA_KROWS
        real0.append(row)
        row += b * n
    groups = []
    for (b, n), r0 in zip(shapes, real0):
        groups.append(Group(b, n, r0, row))
        row += _round_up(b * N_META, LANES)
    return Layout(tuple(groups), _round_up(row, TM))


def _rms(x, g):
    return x * lax.rsqrt(jnp.mean(x * x, axis=-1, keepdims=True) + EPS) * g


def _swiglu_half(xn, wg_ref, wu_ref, wd_ref):
    acc = None
    for lo, hi in ((0, FF_SPLIT), (FF_SPLIT, D_FF)):
        gate = jnp.dot(xn, wg_ref[:, lo:hi], preferred_element_type=jnp.float32)
        up = jnp.dot(xn, wu_ref[:, lo:hi], preferred_element_type=jnp.float32)
        hm = (gate * jax.nn.sigmoid(gate) * up).astype(jnp.bfloat16)
        part = jnp.dot(hm, wd_ref[lo:hi, :], preferred_element_type=jnp.float32)
        acc = part if acc is None else acc + part
    return 0.5 * acc


def _rope_chunk(x, c, s, half, period):
    lane = lax.broadcasted_iota(jnp.int32, x.shape, 1)
    lo = (lane & (period - 1)) < half
    partner = jnp.where(lo, pltpu.roll(x, LANES - half, 1), pltpu.roll(x, half, 1))
    return x * c + partner * s


def _segment_specs(bounds):
    return [pl.BlockSpec((TM, D_MODEL), lambda i, lo=lo, hi=hi: (jnp.clip(i - lo, 0, hi - lo - 1), 0))
            for lo, hi in zip(bounds[:-1], bounds[1:])]


def _ffn_inproj_kernel(*refs, bounds):
    n_src = len(bounds) - 1
    srcs = refs[:n_src]
    (cda_ref, sda_ref, cm_ref, sm_ref, ga_ref, gb_ref,
     wg_ref, wu_ref, wd_ref, win_ref, gq_ref, gkv_ref, wuq_ref, wukv_ref,
     h1_ref, naq_ref, nak_ref, nav_ref, daq_ref, dak_ref, dav_ref,
     mq_ref, mk_ref, mv_ref, xn2_ref) = refs[n_src:]
    i = pl.program_id(0)

    @pl.when(i == 0)
    def _():
        xn2_ref[...] = jnp.zeros(xn2_ref.shape, xn2_ref.dtype)

    xn2 = xn2_ref[...]

    u = jnp.dot(xn2, win_ref[:, 0:3 * NA_W], preferred_element_type=jnp.float32)
    naq_ref[0] = (u[:, 0:NA_W] * (NA_HD ** -0.5 * LOG2E)).T.astype(jnp.bfloat16)
    nak_ref[...] = u[:, NA_W:2 * NA_W].astype(jnp.bfloat16)
    vt = u[:, 2 * NA_W:3 * NA_W].T.astype(jnp.bfloat16)
    for c in range(TM // LANES):
        nav_ref[c] = vt[:, c * LANES:(c + 1) * LANES]

    o0 = 3 * NA_W
    u = jnp.dot(xn2, win_ref[:, o0:o0 + 2 * DA_QK_W + DA_W], preferred_element_type=jnp.float32)
    cda, sda = cda_ref[...], sda_ref[...]
    q = jnp.concatenate([_rope_chunk(u[:, c * LANES:(c + 1) * LANES], cda, sda, DA_ROT // 2, DA_HD)
                         for c in range(DA_QK_W // LANES)], axis=1)
    k = jnp.concatenate([_rope_chunk(u[:, DA_QK_W + c * LANES:DA_QK_W + (c + 1) * LANES], cda, sda, DA_ROT // 2, DA_HD)
                         for c in range(DA_QK_W // LANES)], axis=1)
    daq_ref[0] = (q * (DA_HD ** -0.5 * LOG2E)).T.astype(jnp.bfloat16)
    dak_ref[...] = k.astype(jnp.bfloat16)
    vt = u[:, 2 * DA_QK_W:2 * DA_QK_W + DA_W].T.astype(jnp.bfloat16)
    for c in range(TM // TK):
        dav_ref[c] = vt[:, c * TK:(c + 1) * TK]

    o1 = o0 + 2 * DA_QK_W + DA_W
    u = jnp.dot(xn2, win_ref[:, o1:IN_W_PAD], preferred_element_type=jnp.float32)
    cm, sm = cm_ref[...], sm_ref[...]
    lane = lax.broadcasted_iota(jnp.int32, cm.shape, 1)
    band = (lane >= MLA_NOPE) & (lane < MLA_NOPE + MLA_ROPE)
    cb, sb = jnp.where(band, cm, 1.0), jnp.where(band, sm, 0.0)
    cq = _rms(u[:, 0:Q_LORA], gq_ref[...]).astype(jnp.bfloat16)
    qm = jnp.dot(cq, wuq_ref[...], preferred_element_type=jnp.float32)
    qm = jnp.concatenate([_rope_chunk(qm[:, h * LANES:(h + 1) * LANES], cb, sb, MLA_ROPE // 2, MLA_ROPE)
                          for h in range(MLA_HEADS)], axis=1) * ((MLA_NOPE + MLA_ROPE) ** -0.5 * LOG2E)
    mq_ref[0] = qm.T.astype(jnp.bfloat16)
    ckv = _rms(u[:, Q_LORA:Q_LORA + KV_LORA], gkv_ref[...]).astype(jnp.bfloat16)
    kv = jnp.dot(ckv, wukv_ref[...], preferred_element_type=jnp.float32)
    kr = _rope_chunk(u[:, Q_LORA + KV_LORA:Q_LORA + KV_LORA + LANES], cm, sm, MLA_ROPE // 2, MLA_ROPE)
    kr = jnp.where(band, kr, 0.0)
    kf = jnp.concatenate([kv[:, h * LANES:(h + 1) * LANES] + kr for h in range(MLA_HEADS)], axis=1)
    mk_ref[...] = kf.astype(jnp.bfloat16)
    vt = kv[:, MLA_HEADS * LANES:MLA_HEADS * LANES + MLA_W].T.astype(jnp.bfloat16)
    for c in range(TM // TK):
        mv_ref[c] = vt[:, c * TK:(c + 1) * TK]

    x = srcs[-1][...]
    for sgm in range(n_src - 2, -1, -1):
        x = jnp.where(i < bounds[sgm + 1], srcs[sgm][...], x)
    xn = _rms(x, ga_ref[...]).astype(jnp.bfloat16)
    h1 = x + _swiglu_half(xn, wg_ref, wu_ref, wd_ref)
    h1_ref[...] = h1
    xn2_ref[...] = _rms(h1, gb_ref[...]).astype(jnp.bfloat16)


def _outproj_ffn_kernel(h_ref, ona_ref, oda_ref, om_ref, wo1_ref, wo2_ref, wo3_ref, g_ref,
                        wg_ref, wu_ref, wd_ref, gf_ref, *o_refs, out_bounds):
    h = h_ref[...]
    h = h + jnp.dot(ona_ref[...], wo1_ref[...], preferred_element_type=jnp.float32)
    h = h + jnp.dot(oda_ref[...], wo2_ref[...], preferred_element_type=jnp.float32)
    h = h + jnp.dot(om_ref[...], wo3_ref[...], preferred_element_type=jnp.float32)
    xn = _rms(h, g_ref[...]).astype(jnp.bfloat16)
    h = h + _swiglu_half(xn, wg_ref, wu_ref, wd_ref)
    if out_bounds is None:
        o_refs[0][...] = h
    else:
        h = _rms(h, gf_ref[...])
        i = pl.program_id(0)
        for o_ref, lo, hi in zip(o_refs, out_bounds[:-1], out_bounds[1:]):
            @pl.when((i >= lo) & (i < hi))
            def _(o_ref=o_ref):
                o_ref[...] = h


def _const_spec(shape):
    nd = len(shape)
    return pl.BlockSpec(shape, lambda i: (0,) * nd, pipeline_mode=pl.Buffered(1))


def _ffn_inproj(srcs, bounds, tabs, tab_block, ga, gb, wg, wu, wd, win, gq, gkv, wuq, wukv):
    nt = bounds[-1]
    rows = nt * TM
    prev = lambda i: jnp.maximum(i - 1, 0)
    row_spec = lambda w: pl.BlockSpec((TM, w), lambda i: (prev(i), 0))
    tile_spec = lambda r: pl.BlockSpec((1, r, TM), lambda i: (prev(i), 0, 0))
    bf = jnp.bfloat16
    out_shape = (
        jax.ShapeDtypeStruct((rows, D_MODEL), jnp.float32),
        jax.ShapeDtypeStruct((nt, NA_W, TM), bf),
        jax.ShapeDtypeStruct((rows, NA_W), bf),
        jax.ShapeDtypeStruct((rows // LANES, NA_W, LANES), bf),
        jax.ShapeDtypeStruct((nt, DA_QK_W, TM), bf),
        jax.ShapeDtypeStruct((rows, DA_QK_W), bf),
        jax.ShapeDtypeStruct((rows // TK, DA_W, TK), bf),
        jax.ShapeDtypeStruct((nt, MLA_HEADS * LANES, TM), bf),
        jax.ShapeDtypeStruct((rows, MLA_QK_W), bf),
        jax.ShapeDtypeStruct((rows // TK, MLA_W, TK), bf),
    )
    out_specs = (
        pl.BlockSpec((TM, D_MODEL), lambda i: (jnp.minimum(i, nt - 1), 0)),
        tile_spec(NA_W),
        row_spec(NA_W),
        pl.BlockSpec((TM // LANES, NA_W, LANES), lambda i: (prev(i), 0, 0)),
        tile_spec(DA_QK_W),
        row_spec(DA_QK_W),
        pl.BlockSpec((TM // TK, DA_W, TK), lambda i: (prev(i), 0, 0)),
        tile_spec(MLA_HEADS * LANES),
        row_spec(MLA_QK_W),
        pl.BlockSpec((TM // TK, MLA_W, TK), lambda i: (prev(i), 0, 0)),
    )
    in_specs = _segment_specs(bounds) + [pl.BlockSpec((TM, LANES), lambda i: (tab_block(prev(i)), 0))] * 4 + [
        _const_spec(a.shape) for a in (ga, gb, wg, wu, wd, win, gq, gkv, wuq, wukv)]
    return pl.pallas_call(
        functools.partial(_ffn_inproj_kernel, bounds=bounds),
        grid=(nt + 1,),
        in_specs=in_specs,
        out_specs=out_specs,
        out_shape=out_shape,
        scratch_shapes=[pltpu.VMEM((TM, D_MODEL), jnp.bfloat16)],
        compiler_params=pltpu.CompilerParams(dimension_semantics=("arbitrary",),
                                             vmem_limit_bytes=VMEM_LIMIT_BYTES),
        name="ffn_inproj",
    )(*srcs, *tabs, ga, gb, wg, wu, wd, win, gq, gkv, wuq, wukv)


def _outproj_ffn(h, ona, oda, om, wo1, wo2, wo3, g, wg, wu, wd, gf, out_bounds):
    rows = h.shape[0]
    row_spec = lambda w: pl.BlockSpec((TM, w), lambda i: (i, 0))
    if out_bounds is None:
        out_specs = row_spec(D_MODEL)
        out_shape = jax.ShapeDtypeStruct((rows, D_MODEL), jnp.float32)
    else:
        out_specs = tuple(_segment_specs(out_bounds))
        out_shape = tuple(jax.ShapeDtypeStruct(((hi - lo) * TM, D_MODEL), jnp.float32)
                          for lo, hi in zip(out_bounds[:-1], out_bounds[1:]))
    in_specs = [row_spec(D_MODEL), row_spec(NA_W), row_spec(DA_W), row_spec(MLA_W)] + [
        _const_spec(a.shape) for a in (wo1, wo2, wo3, g, wg, wu, wd, gf)]
    return pl.pallas_call(
        functools.partial(_outproj_ffn_kernel, out_bounds=out_bounds),
        grid=(rows // TM,),
        in_specs=in_specs,
        out_specs=out_specs,
        out_shape=out_shape,
        compiler_params=pltpu.CompilerParams(dimension_semantics=("arbitrary",),
                                             vmem_limit_bytes=VMEM_LIMIT_BYTES),
        name="outproj_ffn",
    )(h, ona, oda, om, wo1, wo2, wo3, g, wg, wu, wd, gf)


def _row_band(block, lo, hi):
    row = lax.broadcasted_iota(jnp.int32, block.shape, 0)
    return jnp.where((row >= lo) & (row < hi), block, jnp.zeros_like(block))


def _with_ones(vt):
    return jnp.concatenate([vt, jnp.ones((ONES_ROWS, vt.shape[1]), vt.dtype)], axis=0)


def _meta_key_mask(b):
    row = lax.broadcasted_iota(jnp.int32, (LANES, 1), 0)
    lo = (b % (LANES // N_META)) * N_META
    return jnp.where((row >= lo) & (row < lo + N_META), 0.0, NEG).astype(jnp.float32)


class FlashCfg(NamedTuple):
    n_soft: int
    group_w: int
    soft_per_group: int
    v_of_soft: tuple
    out_w: int
    unroll: int


DA_CFG = FlashCfg(2 * DA_HEADS, LANES, 4, tuple(i // 2 for i in range(2 * DA_HEADS)), DA_W, 16)
MLA_CFG = FlashCfg(MLA_HEADS, LANES, 1, tuple(range(MLA_HEADS)), MLA_W, 32)
NA_CFG = FlashCfg(NA_HEADS, LANES, 2, tuple(range(NA_HEADS)), NA_W, 1)


def _build_rhs(cfg, qt_ref, rhs_ref):
    for i in range(cfg.n_soft):
        g, j = i // cfg.soft_per_group, i % cfg.soft_per_group
        if cfg is DA_CFG:
            rhs_ref[i] = _row_band(qt_ref[0, g * LANES:(g + 1) * LANES, :], j * DA_HD, (j + 1) * DA_HD)
        else:
            rhs_ref[i] = qt_ref[0, i * LANES:(i + 1) * LANES, :]


def _flash_tiles(cfg, tiles, rhs_ref, m_ref, acc_ref, bounded):
    nt = len(tiles)
    if bounded:
        work = [(t, i) for i in range(cfg.n_soft) for t in range(nt)]
    else:
        work = [(t, i) for t in range(nt) for i in range(cfg.n_soft)]

    def score(w):
        kblk_of, _, mask = tiles[w[0]]
        s = jnp.dot(kblk_of(w[1] // cfg.soft_per_group), rhs_ref[w[1]], preferred_element_type=jnp.float32)
        return s if mask is None else s + mask

    pending = [score(w) for w in work[:LOOKAHEAD]]
    part = den = None
    for n, (t, i) in enumerate(work):
        vt = tiles[t][1](cfg.v_of_soft[i])
        s = pending.pop(0)
        if n + LOOKAHEAD < len(work):
            pending.append(score(work[n + LOOKAHEAD]))
        if bounded:
            p = jnp.exp2(s)
            pv = jnp.dot(vt, p.astype(jnp.bfloat16), preferred_element_type=jnp.float32)
            ps = jnp.sum(p, axis=0, keepdims=True)
            part, den = (pv, ps) if t == 0 else (part + pv, den + ps)
            if t == nt - 1:
                acc_ref[i, 0:DA_VD, :] = acc_ref[i, 0:DA_VD, :] + part
                acc_ref[i, DA_VD:DA_VD + 1, :] = acc_ref[i, DA_VD:DA_VD + 1, :] + den
        else:
            vt = _with_ones(vt)
            m_prev = m_ref[i]
            m_new = jnp.maximum(m_prev, jnp.max(s, axis=0, keepdims=True))
            alpha = jnp.exp2(m_prev - m_new)
            p = jnp.exp2(s - m_new).astype(jnp.bfloat16)
            acc_ref[i] = alpha * acc_ref[i] + jnp.dot(vt, p, preferred_element_type=jnp.float32)
            m_ref[i] = m_new


def _feature_indicator(cfg):
    f = lax.broadcasted_iota(jnp.int32, (cfg.group_w, LANES), 0)
    j = lax.broadcasted_iota(jnp.int32, (cfg.group_w, LANES), 1)
    band = cfg.group_w // cfg.soft_per_group
    hit = jnp.right_shift(f, int(math.log2(band))) == j
    return jnp.where(hit & (j < cfg.soft_per_group), 1.0, 0.0).astype(jnp.bfloat16)


def _max_key_norms(cfg, kf_ref, kfm_ref, n_keys, kn_ref):
    gw = cfg.group_w
    n_groups = cfg.n_soft // cfg.soft_per_group
    ind = _feature_indicator(cfg)

    def sq_norms(blk):
        x = blk.astype(jnp.float32)
        n2 = jnp.dot((x * x).astype(jnp.bfloat16), ind, preferred_element_type=jnp.float32)
        return jnp.max(n2, axis=0, keepdims=True)

    def body(c, carry):
        r0 = pl.multiple_of(c * TM, TM)
        return tuple(jnp.maximum(carry[g], sq_norms(kf_ref[pl.ds(r0, TM), g * gw:(g + 1) * gw]))
                     for g in range(n_groups))

    init = tuple(sq_norms(kfm_ref[:, g * gw:(g + 1) * gw]) for g in range(n_groups))
    kmax = lax.fori_loop(0, n_keys // TM, body, init)
    lane = lax.broadcasted_iota(jnp.int32, (1, LANES), 1)
    for i in range(cfg.n_soft):
        g, j = divmod(i, cfg.soft_per_group)
        v = jnp.max(jnp.where(lane == j, kmax[g], 0.0), axis=1, keepdims=True)
        kn_ref[i] = jnp.broadcast_to(v, (1, LANES))


def _scores_are_bounded(cfg, rhs_ref, kn_ref):
    ok = None
    for i in range(cfg.n_soft):
        r = rhs_ref[i].astype(jnp.float32)
        bound2 = jnp.sum(r * r, axis=0, keepdims=True) * kn_ref[i][:, 0:1] * NORM_SLACK
        good = bound2 <= SCORE_BOUND * SCORE_BOUND
        ok = good if ok is None else ok & good
    return jnp.min(jnp.where(ok, 1.0, 0.0)) > 0.5


def _flash_kernel(*refs, cfg, n_keys, lam_init, meta_queries):
    if cfg is DA_CFG:
        (qt_ref, kf_ref, vt_ref, kfm_ref, vtm_ref, lam_ref, gs_ref, o_ref,
         rhs_ref, m_ref, acc_ref, kn_ref, *rest) = refs
    else:
        qt_ref, kf_ref, vt_ref, kfm_ref, vtm_ref, o_ref, rhs_ref, m_ref, acc_ref, kn_ref, *rest = refs
    b = pl.program_id(0)
    tq = qt_ref.shape[2]
    vd = DA_VD
    gw = cfg.group_w
    if meta_queries:
        _max_key_norms(cfg, kf_ref, kfm_ref, n_keys, kn_ref)
    else:
        @pl.when(pl.program_id(1) == 0)
        def _():
            _max_key_norms(cfg, kf_ref, kfm_ref, n_keys, kn_ref)
    _build_rhs(cfg, qt_ref, rhs_ref)
    m_ref[...] = jnp.full(m_ref.shape, NEG, jnp.float32)
    acc_ref[...] = jnp.zeros(acc_ref.shape, jnp.float32)

    def all_keys(bounded, unroll):
        _flash_tiles(
            cfg,
            [(lambda g: kfm_ref[:, g * gw:(g + 1) * gw],
              lambda h: vtm_ref[0, h * vd:(h + 1) * vd, :],
              _meta_key_mask(b))],
            rhs_ref, m_ref, acc_ref, bounded)

        def body(it, carry):
            tiles = []
            for u in range(unroll):
                kt = it * unroll + u
                k0 = pl.multiple_of(kt * TK, TK)
                tiles.append((
                    lambda g, k0=k0: kf_ref[pl.ds(k0, TK), g * gw:(g + 1) * gw],
                    lambda h, kt=kt: vt_ref[kt, h * vd:(h + 1) * vd, :],
                    None))
            _flash_tiles(cfg, tiles, rhs_ref, m_ref, acc_ref, bounded)
            return carry

        lax.fori_loop(0, n_keys // (TK * unroll), body, 0)

    lax.cond(_scores_are_bounded(cfg, rhs_ref, kn_ref),
             lambda: all_keys(True, math.gcd(cfg.unroll, n_keys // TK)),
             lambda: all_keys(False, 1))

    heads = []
    if cfg is DA_CFG:
        lp = lam_ref[...]
        lam = (jnp.exp(jnp.sum(lp[0:1] * lp[1:2], axis=1, keepdims=True))
               - jnp.exp(jnp.sum(lp[2:3] * lp[3:4], axis=1, keepdims=True)) + lam_init)
        for h in range(DA_HEADS):
            a1, a2 = acc_ref[2 * h], acc_ref[2 * h + 1]
            o = a1[0:vd] * (1.0 / a1[vd:vd + 1]) - a2[0:vd] * (lam / a2[vd:vd + 1])
            o = o * lax.rsqrt(jnp.mean(o * o, axis=0, keepdims=True) + EPS) * gs_ref[...]
            heads.append(o * (1.0 - lam_init))
    else:
        for h in range(MLA_HEADS):
            a = acc_ref[h]
            heads.append(a[0:vd] * (1.0 / a[vd:vd + 1]))
    pairs = [jnp.concatenate(heads[2 * g:2 * g + 2], axis=0).T for g in range(len(heads) // 2)]
    out = jnp.concatenate(pairs, axis=1)
    if meta_queries:
        stage_ref = rest[0]
        stage_ref[...] = out
        r0 = pl.multiple_of((b % (LANES // N_META)) * N_META, N_META)
        o_ref[...] = stage_ref[pl.ds(r0, N_META), :].astype(o_ref.dtype)
    else:
        o_ref[...] = out.astype(o_ref.dtype)


def _flash_call(cfg, grp, qt, kf, vt, extra, o_prev, lam_init, meta_queries, name):
    n, bsz = grp.n, grp.batch
    qw, kw, vw = qt.shape[1], kf.shape[1], vt.shape[1]
    tiles_per_seq = n // TM
    meta_blk0 = grp.meta0 // LANES
    per_blk = LANES // N_META
    if meta_queries:
        grid = (bsz,)
        tq = LANES
        q_map = lambda b: ((meta_blk0 + b // per_blk) // (TM // LANES), 0, (meta_blk0 + b // per_blk) % (TM // LANES))
        o_spec = pl.BlockSpec((N_META, cfg.out_w), lambda b: (grp.meta0 // N_META + b, 0))
        fix = lambda f: (lambda b: f(b))
        sem = ("arbitrary",)
    else:
        grid = (bsz, tiles_per_seq)
        tq = TM
        q_map = lambda b, i: (grp.real0 // TM + b * tiles_per_seq + i, 0, 0)
        o_spec = pl.BlockSpec((TM, cfg.out_w), lambda b, i: (grp.real0 // TM + b * tiles_per_seq + i, 0))
        fix = lambda f: (lambda b, i: f(b))
        sem = ("arbitrary", "arbitrary")
    in_specs = [
        pl.BlockSpec((1, qw, tq), q_map),
        pl.BlockSpec((n, kw), fix(lambda b: (grp.real0 // n + b, 0))),
        pl.BlockSpec((n // TK, vw, TK), fix(lambda b: (grp.real0 // n + b, 0, 0))),
        pl.BlockSpec((LANES, kw), fix(lambda b: (meta_blk0 + b // per_blk, 0))),
        pl.BlockSpec((1, vw, LANES), fix(lambda b: ((meta_blk0 + b // per_blk) // (TK // LANES), 0,
                                                     (meta_blk0 + b // per_blk) % (TK // LANES)))),
    ]
    args = [qt, kf, vt, kf, vt]
    for a in extra:
        in_specs.append(pl.BlockSpec(a.shape, fix(lambda b, nd=a.ndim: (0,) * nd)))
        args.append(a)
    in_specs.append(pl.BlockSpec(memory_space=pl.ANY))
    args.append(o_prev)
    scratch = [
        pltpu.VMEM((cfg.n_soft, cfg.group_w, tq), jnp.bfloat16),
        pltpu.VMEM((cfg.n_soft, 1, tq), jnp.float32),
        pltpu.VMEM((cfg.n_soft, DA_VD + ONES_ROWS, tq), jnp.float32),
        pltpu.VMEM((cfg.n_soft, 1, LANES), jnp.float32),
    ]
    if meta_queries:
        scratch.append(pltpu.VMEM((LANES, cfg.out_w), jnp.float32))

    def body(*refs):
        n_in = len(args)
        ins, rest = refs[:n_in - 1], refs[n_in:]
        _flash_kernel(*ins, *rest, cfg=cfg, n_keys=n, lam_init=lam_init, meta_queries=meta_queries)

    return pl.pallas_call(
        body,
        grid=grid,
        in_specs=in_specs,
        out_specs=o_spec,
        out_shape=jax.ShapeDtypeStruct(o_prev.shape, o_prev.dtype),
        scratch_shapes=scratch,
        input_output_aliases={len(args) - 1: 0},
        compiler_params=pltpu.CompilerParams(dimension_semantics=sem, vmem_limit_bytes=VMEM_LIMIT_BYTES),
        name=name,
    )(*args)


def _na_kernel(qt_ref, kf_ref, vt_ref, kfm_ref, vtm_ref, bias_ref, bmax_ref, prev_ref, o_ref, kn_ref, *, grid_rows):
    del prev_ref
    b, i = pl.program_id(0), pl.program_id(1)

    @pl.when(i == 0)
    def _():
        _max_key_norms(NA_CFG, kf_ref, kfm_ref, grid_rows * GRID_W, kn_ref)

    ws = jnp.clip(i * NA_QROWS - WIN_R // 2, 0, grid_rows - NA_KROWS)
    kwin = kf_ref[pl.ds(pl.multiple_of(ws * GRID_W, 2 * GRID_W), NA_KROWS * GRID_W), :]
    vwin = vt_ref[pl.ds(ws // 2, NA_KROWS // 2)]
    kmeta = kfm_ref[...]
    mmask = _meta_key_mask(b)

    def rhs_of(h):
        g, j = h // 2, h % 2
        return _row_band(qt_ref[0, g * LANES:(g + 1) * LANES, :], j * NA_HD, (j + 1) * NA_HD)

    def scores(h):
        g, rhs = h // 2, rhs_of(h)
        s1 = jnp.dot(kwin[:, g * LANES:(g + 1) * LANES], rhs, preferred_element_type=jnp.float32) + bias_ref[0, h]
        s2 = jnp.dot(kmeta[:, g * LANES:(g + 1) * LANES], rhs, preferred_element_type=jnp.float32) + mmask
        return s1, s2

    def run(bounded):
        heads = []
        pending = [scores(h) for h in range(LOOKAHEAD)]
        for h in range(NA_HEADS):
            s1, s2 = pending.pop(0)
            if h + LOOKAHEAD < NA_HEADS:
                pending.append(scores(h + LOOKAHEAD))
            if not bounded:
                m = jnp.maximum(jnp.max(s1, axis=0, keepdims=True), jnp.max(s2, axis=0, keepdims=True))
                s1, s2 = s1 - m, s2 - m
            p1 = jnp.exp2(s1).astype(jnp.bfloat16)
            p2 = jnp.exp2(s2).astype(jnp.bfloat16)
            v1 = jnp.concatenate([vwin[c, h * NA_HD:(h + 1) * NA_HD, :] for c in range(NA_KROWS // 2)], axis=1)
            acc = (jnp.dot(_with_ones(v1), p1, preferred_element_type=jnp.float32)
                   + jnp.dot(_with_ones(vtm_ref[0, h * NA_HD:(h + 1) * NA_HD, :]), p2,
                             preferred_element_type=jnp.float32))
            heads.append(acc[0:NA_HD] * (1.0 / acc[NA_HD:NA_HD + 1]))
        pairs = [jnp.concatenate(heads[2 * g:2 * g + 2], axis=0).T for g in range(NA_HEADS // 2)]
        o_ref[...] = jnp.concatenate(pairs, axis=1).astype(o_ref.dtype)

    limit = SCORE_BOUND - bmax_ref[...]
    ok = None
    for h in range(NA_HEADS):
        r = rhs_of(h).astype(jnp.float32)
        bound2 = jnp.sum(r * r, axis=0, keepdims=True) * kn_ref[h][:, 0:1] * NORM_SLACK
        good = (limit > 0.0) & (bound2 <= limit * limit)
        ok = good if ok is None else ok & good
    lax.cond(jnp.min(jnp.where(ok, 1.0, 0.0)) > 0.5, lambda: run(True), lambda: run(False))


def _na_meta_kernel(qt_ref, kf_ref, vt_ref, prev_ref, o_ref):
    del prev_ref
    kr = lax.broadcasted_iota(jnp.int32, (LANES, LANES), 0) // N_META
    qc = lax.broadcasted_iota(jnp.int32, (LANES, LANES), 1) // N_META
    mask = jnp.where(kr == qc, 0.0, NEG).astype(jnp.float32)
    kf = kf_ref[...]
    heads = []
    for h in range(NA_HEADS):
        g, j = h // 2, h % 2
        rhs = _row_band(qt_ref[0, g * LANES:(g + 1) * LANES, :], j * NA_HD, (j + 1) * NA_HD)
        s = jnp.dot(kf[:, g * LANES:(g + 1) * LANES], rhs, preferred_element_type=jnp.float32) + mask
        p = jnp.exp2(s - jnp.max(s, axis=0, keepdims=True)).astype(jnp.bfloat16)
        acc = jnp.dot(_with_ones(vt_ref[0, h * NA_HD:(h + 1) * NA_HD, :]), p, preferred_element_type=jnp.float32)
        heads.append(acc[0:NA_HD] * (1.0 / acc[NA_HD:NA_HD + 1]))
    pairs = [jnp.concatenate(heads[2 * g:2 * g + 2], axis=0).T for g in range(NA_HEADS // 2)]
    o_ref[...] = jnp.concatenate(pairs, axis=1).astype(o_ref.dtype)


def _na_call(grp, qt, kf, vt, bias, bmax, o_prev):
    n, bsz = grp.n, grp.batch
    grid_rows = n // GRID_W
    steps = grid_rows // NA_QROWS
    tq = NA_QROWS * GRID_W
    meta_blk0 = grp.meta0 // LANES
    per_blk = LANES // N_META
    in_specs = [
        pl.BlockSpec((1, NA_W, tq), lambda b, i: (grp.real0 // TM + b * (n // TM) + i // (TM // tq), 0, i % (TM // tq))),
        pl.BlockSpec((n, NA_W), lambda b, i: (grp.real0 // n + b, 0)),
        pl.BlockSpec((n // LANES, NA_W, LANES), lambda b, i: (grp.real0 // n + b, 0, 0)),
        pl.BlockSpec((LANES, NA_W), lambda b, i: (meta_blk0 + b // per_blk, 0)),
        pl.BlockSpec((1, NA_W, LANES), lambda b, i: (meta_blk0 + b // per_blk, 0, 0)),
        pl.BlockSpec((1, NA_HEADS, NA_KROWS * GRID_W, tq),
                     lambda b, i: (jnp.where(i == 0, 0, jnp.where(i == steps - 1, 2, 1)), 0, 0, 0)),
        pl.BlockSpec((1, 1), lambda b, i: (0, 0)),
        pl.BlockSpec(memory_space=pl.ANY),
    ]
    return pl.pallas_call(
        functools.partial(_na_kernel, grid_rows=grid_rows),
        grid=(bsz, steps),
        in_specs=in_specs,
        out_specs=pl.BlockSpec((tq, NA_W), lambda b, i: (grp.real0 // tq + b * steps + i, 0)),
        out_shape=jax.ShapeDtypeStruct(o_prev.shape, o_prev.dtype),
        scratch_shapes=[pltpu.VMEM((NA_HEADS, 1, LANES), jnp.float32)],
        input_output_aliases={7: 0},
        compiler_params=pltpu.CompilerParams(dimension_semantics=("arbitrary", "arbitrary"),
                                             vmem_limit_bytes=VMEM_LIMIT_BYTES),
        name="na_real",
    )(qt, kf, vt, kf, vt, bias, bmax, o_prev)


def _na_meta_call(layout, qt, kf, vt, o_prev):
    blk0 = layout.groups[0].meta0 // LANES
    nblk = sum(_round_up(g.batch * N_META, LANES) for g in layout.groups) // LANES
    sub = TM // LANES
    return pl.pallas_call(
        _na_meta_kernel,
        grid=(nblk,),
        in_specs=[
            pl.BlockSpec((1, NA_W, LANES), lambda i: ((blk0 + i) // sub, 0, (blk0 + i) % sub)),
            pl.BlockSpec((LANES, NA_W), lambda i: (blk0 + i, 0)),
            pl.BlockSpec((1, NA_W, LANES), lambda i: (blk0 + i, 0, 0)),
            pl.BlockSpec(memory_space=pl.ANY),
        ],
        out_specs=pl.BlockSpec((LANES, NA_W), lambda i: (blk0 + i, 0)),
        out_shape=jax.ShapeDtypeStruct(o_prev.shape, o_prev.dtype),
        input_output_aliases={3: 0},
        compiler_params=pltpu.CompilerParams(dimension_semantics=("arbitrary",),
                                             vmem_limit_bytes=VMEM_LIMIT_BYTES),
        name="na_meta",
    )(qt, kf, vt, o_prev)


def _na_bias_tables(rel_bias):
    kc = np.arange(GRID_W)[:, None]
    qc = np.arange(GRID_W)[None, :]
    cs = np.clip(qc - WIN_C // 2, 0, GRID_W - WIN_C)
    col_ok = (kc >= cs) & (kc < cs + WIN_C)
    cidx = np.clip(kc - qc + WIN_C - 1, 0, 2 * WIN_C - 2)
    planes = jnp.where(col_ok[None, None], rel_bias[:, :, cidx] * LOG2E, NEG)
    masked = 2 * WIN_R - 1
    planes = jnp.concatenate([planes, jnp.full((NA_HEADS, 1, GRID_W, GRID_W), NEG, planes.dtype)], axis=1)
    kr = np.arange(NA_KROWS)[:, None]
    qr = np.arange(NA_QROWS)[None, :]
    ridx = []
    for win_lo, q_off in ((0 * qr, qr), (qr, qr + WIN_R // 2), (0 * qr + NA_KROWS - WIN_R, qr + NA_KROWS - NA_QROWS)):
        row_ok = (kr >= win_lo) & (kr < win_lo + WIN_R)
        ridx.append(np.where(row_ok, kr - q_off + WIN_R - 1, masked))
    ridx = np.stack(ridx)
    assert ridx.min() >= 0 and ridx.max() <= masked
    t = planes[:, ridx]
    t = jnp.transpose(t, (1, 0, 2, 4, 3, 5))
    return t.reshape(3, NA_HEADS, NA_KROWS * GRID_W, NA_QROWS * GRID_W).astype(jnp.float32)


def _rope_tables(layout):
    pos, starts, periods = [], [], []
    for g in layout.groups:
        starts.append(len(pos))
        periods.append(g.n // TM)
        for t in range(g.n // TM):
            pos.append(N_META + t * TM + np.arange(TM))
    tail0 = sum(g.batch * g.n for g in layout.groups)
    starts.append(len(pos))
    for t in range(tail0 // TM, layout.rows // TM):
        r = t * TM + np.arange(TM)
        p = np.zeros((TM,), np.int64)
        for g in layout.groups:
            inside = (r >= g.meta0) & (r < g.meta0 + g.batch * N_META)
            p = np.where(inside, (r - g.meta0) % N_META, p)
        pos.append(p)
    pos = jnp.asarray(np.concatenate(pos).astype(np.float32))
    bounds = [g.real0 // TM for g in layout.groups] + [tail0 // TM]

    def block_of_tile(i):
        blk = starts[-1] + (i - bounds[-1])
        for k in range(len(layout.groups) - 1, -1, -1):
            blk = jnp.where(i < bounds[k + 1], starts[k] + (i - bounds[k]) % periods[k], blk)
        return blk

    def tables(dim, period):
        inv = ROPE_THETA ** (-(jnp.arange(0, dim, 2, dtype=jnp.float32) / dim))
        ang = pos[:, None] * inv[None, :]
        cos, sin = jnp.cos(ang), jnp.sin(ang)
        rest = period - dim
        c = jnp.concatenate([cos, cos, jnp.ones((pos.shape[0], rest), jnp.float32)], axis=1)
        s = jnp.concatenate([-sin, sin, jnp.zeros((pos.shape[0], rest), jnp.float32)], axis=1)
        reps = LANES // period
        return jnp.tile(c, (1, reps)), jnp.tile(s, (1, reps))

    cda, sda = tables(DA_ROT, DA_HD)
    cm, sm = tables(MLA_ROPE, MLA_ROPE)
    return (cda, sda, cm, sm), block_of_tile


def _prep_layer(l, norm_g, ffn_w_gate, ffn_w_up, ffn_w_down, w_in, w_out, mla_q_norm_g, mla_kv_norm_g,
                mla_w_uq, mla_w_ukv, da_subln_g):
    bf = jnp.bfloat16
    kr0 = IN_W - MLA_ROPE
    win = jnp.concatenate([w_in[l, :, :kr0]] + [w_in[l, :, kr0:]] * (LANES // MLA_ROPE), axis=1).astype(bf)
    uq = mla_w_uq[l].reshape(Q_LORA, MLA_HEADS, MLA_NOPE + MLA_ROPE)
    uq = jnp.pad(uq, ((0, 0), (0, 0), (0, LANES - MLA_NOPE - MLA_ROPE))).reshape(Q_LORA, MLA_HEADS * LANES)
    ukv = mla_w_ukv[l].reshape(KV_LORA, MLA_HEADS, MLA_NOPE + MLA_VD)
    uk = jnp.pad(ukv[:, :, :MLA_NOPE], ((0, 0), (0, 0), (0, LANES - MLA_NOPE))).reshape(KV_LORA, MLA_HEADS * LANES)
    ukv = jnp.concatenate([uk, ukv[:, :, MLA_NOPE:].reshape(KV_LORA, -1)], axis=1)
    row = lambda v: v.reshape(1, -1)
    return dict(
        g=[row(norm_g[l, i]) for i in range(3)],
        ffn=[(ffn_w_gate[l, i].astype(bf), ffn_w_up[l, i].astype(bf), ffn_w_down[l, i].astype(bf)) for i in range(2)],
        win=win, uq=uq.astype(bf), ukv=ukv.astype(bf),
        gq=row(mla_q_norm_g[l]), gkv=row(mla_kv_norm_g[l]),
        wo=(w_out[l, :NA_W].astype(bf), w_out[l, NA_W:NA_W + DA_W].astype(bf), w_out[l, NA_W + DA_W:].astype(bf)),
        gs=da_subln_g[l].reshape(DA_VD, 1),
    )


def kernel(x_prompt, x_sample, meta_tokens, norm_g, final_norm_g, ffn_w_gate, ffn_w_up, ffn_w_down, w_in, w_out, na_rel_bias, da_lambda, da_subln_g, mla_q_norm_g, mla_kv_norm_g, mla_w_uq, mla_w_ukv):
    xs = (x_prompt, x_sample)
    layout = _make_layout([(x.shape[0], x.shape[1]) for x in xs])
    real_rows = sum(g.batch * g.n for g in layout.groups)
    tail = []
    for g in layout.groups:
        blk = jnp.tile(meta_tokens.astype(jnp.float32), (g.batch, 1))
        tail.append(jnp.pad(blk, ((0, _round_up(g.batch * N_META, LANES) - g.batch * N_META), (0, 0))))
    tail = jnp.concatenate(tail, axis=0)
    tail = jnp.pad(tail, ((0, layout.rows - real_rows - tail.shape[0]), (0, 0)))
    srcs = [x.reshape(-1, D_MODEL) for x in xs] + [tail]
    real_bounds = tuple(g.real0 // TM for g in layout.groups) + (real_rows // TM,)
    bounds = real_bounds + (layout.rows // TM,)
    tabs, tab_block = _rope_tables(layout)
    gf = final_norm_g.reshape(1, -1)

    o_na = jnp.zeros((layout.rows, NA_W), jnp.bfloat16)
    o_da = jnp.zeros((layout.rows, DA_W), jnp.bfloat16)
    o_m = jnp.zeros((layout.rows, MLA_W), jnp.bfloat16)
    for l in range(DEPTH):
        p = _prep_layer(l, norm_g, ffn_w_gate, ffn_w_up, ffn_w_down, w_in, w_out, mla_q_norm_g, mla_kv_norm_g,
                        mla_w_uq, mla_w_ukv, da_subln_g)
        lam_init = 0.8 - 0.6 * math.exp(-0.3 * l)
        (h, naq, nak, nav, daq, dak, dav, mq, mk, mv) = _ffn_inproj(
            srcs, bounds, tabs, tab_block, p["g"][0], p["g"][1], *p["ffn"][0], p["win"], p["gq"], p["gkv"], p["uq"], p["ukv"])
        bias = _na_bias_tables(na_rel_bias[l])
        bmax = (jnp.max(jnp.abs(na_rel_bias[l])) * LOG2E).reshape(1, 1)
        o_na = _na_meta_call(layout, naq, nak, nav, o_na)
        for gi, grp in enumerate(layout.groups):
            o_na = _na_call(grp, naq, nak, nav, bias, bmax, o_na)
            for meta_q in (False, True):
                tag = f"g{gi}_{'meta' if meta_q else 'real'}"
                o_da = _flash_call(DA_CFG, grp, daq, dak, dav, (da_lambda[l], p["gs"]), o_da, lam_init, meta_q,
                                   "da_" + tag)
                o_m = _flash_call(MLA_CFG, grp, mq, mk, mv, (), o_m, lam_init, meta_q, "mla_" + tag)
        last = l == DEPTH - 1
        h = _outproj_ffn(h, o_na, o_da, o_m, *p["wo"], p["g"][2], *p["ffn"][1], gf, real_bounds if last else None)
        srcs, bounds = [h], (0, layout.rows // TM)

    return tuple(y.reshape(x.shape) for x, y in zip(xs, h))
```

```python
import functools
import math
from typing import NamedTuple

import jax
import jax.numpy as jnp
import numpy as np
from jax import lax
from jax.experimental import pallas as pl
from jax.experimental.pallas import tpu as pltpu

D_MODEL = 1024
DEPTH = 2
GRID_W = 64
N_META = 16
WIN_R = 8
WIN_C = 16
NA_HEADS = 6
NA_HD = 64
DA_HEADS = 6
DA_HD = 32
DA_VD = 64
DA_ROT = DA_HD // 4
MLA_HEADS = 4
MLA_NOPE = 64
MLA_ROPE = 32
MLA_VD = 64
Q_LORA = 256
KV_LORA = 128
ROPE_THETA = 500000.0
D_FF = 2816
EPS = 1e-6
NA_W = NA_HEADS * NA_HD
DA_W = DA_HEADS * DA_VD
MLA_W = MLA_HEADS * MLA_VD
DA_QK_W = DA_HEADS * 2 * DA_HD
IN_W = 3 * NA_W + 2 * DA_QK_W + DA_W + Q_LORA + KV_LORA + MLA_ROPE

LANES = 128
MXU_DIM = 256
VMEM_LIMIT_BYTES = 60 * 1024 * 1024

TM = 512
TK = 256
NA_QROWS = 4
NA_KROWS = NA_QROWS + WIN_R
LOOKAHEAD = 2
FF_SPLIT = 1536
IN_W_PAD = IN_W - MLA_ROPE + LANES
MLA_QK_W = MLA_HEADS * LANES
assert MLA_VD == DA_VD and IN_W_PAD % LANES == 0

LOG2E = 1.4426950408889634
NEG = -1e30
ONES_ROWS = 16
SCORE_BOUND = 60.0
NORM_SLACK = 1.05


def _round_up(x, m):
    return (x + m - 1) // m * m


class Group(NamedTuple):
    batch: int
    n: int
    real0: int
    meta0: int


class Layout(NamedTuple):
    groups: tuple
    rows: int


def _make_layout(shapes):
    row = 0
    real0 = []
    for b, n in shapes:
        assert n % TM == 0 and row % n == 0 and n % GRID_W == 0
        assert (n // GRID_W) % NA_QROWS == 0 and n // GRID_W >= NA_KROWS
        real0.append(row)
        row += b * n
    groups = []
    for (b, n), r0 in zip(shapes, real0):
        groups.append(Group(b, n, r0, row))
        row += _round_up(b * N_META, LANES)
    return Layout(tuple(groups), _round_up(row, TM))


def _rms(x, g):
    return x * lax.rsqrt(jnp.mean(x * x, axis=-1, keepdims=True) + EPS) * g


def _swiglu_half(xn, wg_ref, wu_ref, wd_ref):
    acc = None
    for lo, hi in ((0, FF_SPLIT), (FF_SPLIT, D_FF)):
        gate = jnp.dot(xn, wg_ref[:, lo:hi], preferred_element_type=jnp.float32)
        up = jnp.dot(xn, wu_ref[:, lo:hi], preferred_element_type=jnp.float32)
        hm = (gate * jax.nn.sigmoid(gate) * up).astype(jnp.bfloat16)
        part = jnp.dot(hm, wd_ref[lo:hi, :], preferred_element_type=jnp.float32)
        acc = part if acc is None else acc + part
    return 0.5 * acc


def _rope_chunk(x, c, s, half, period):
    lane = lax.broadcasted_iota(jnp.int32, x.shape, 1)
    lo = (lane & (period - 1)) < half
    partner = jnp.where(lo, pltpu.roll(x, LANES - half, 1), pltpu.roll(x, half, 1))
    return x * c + partner * s


def _segment_specs(bounds):
    return [pl.BlockSpec((TM, D_MODEL), lambda i, lo=lo, hi=hi: (jnp.clip(i - lo, 0, hi - lo - 1), 0))
            for lo, hi in zip(bounds[:-1], bounds[1:])]


def _ffn_inproj_kernel(*refs, bounds):
    n_src = len(bounds) - 1
    srcs = refs[:n_src]
    (cda_ref, sda_ref, cm_ref, sm_ref, ga_ref, gb_ref,
     wg_ref, wu_ref, wd_ref, win_ref, gq_ref, gkv_ref, wuq_ref, wukv_ref,
     h1_ref, naq_ref, nak_ref, nav_ref, daq_ref, dak_ref, dav_ref,
     mq_ref, mk_ref, mv_ref, xn2_ref) = refs[n_src:]
    i = pl.program_id(0)

    @pl.when(i == 0)
    def _():
        xn2_ref[...] = jnp.zeros(xn2_ref.shape, xn2_ref.dtype)

    xn2 = xn2_ref[...]

    u = jnp.dot(xn2, win_ref[:, 0:3 * NA_W], preferred_element_type=jnp.float32)
    naq_ref[0] = (u[:, 0:NA_W] * (NA_HD ** -0.5 * LOG2E)).T.astype(jnp.bfloat16)
    nak_ref[...] = u[:, NA_W:2 * NA_W].astype(jnp.bfloat16)
    vt = u[:, 2 * NA_W:3 * NA_W].T.astype(jnp.bfloat16)
    for c in range(TM // LANES):
        nav_ref[c] = vt[:, c * LANES:(c + 1) * LANES]

    o0 = 3 * NA_W
    u = jnp.dot(xn2, win_ref[:, o0:o0 + 2 * DA_QK_W + DA_W], preferred_element_type=jnp.float32)
    cda, sda = cda_ref[...], sda_ref[...]
    q = jnp.concatenate([_rope_chunk(u[:, c * LANES:(c + 1) * LANES], cda, sda, DA_ROT // 2, DA_HD)
                         for c in range(DA_QK_W // LANES)], axis=1)
    k = jnp.concatenate([_rope_chunk(u[:, DA_QK_W + c * LANES:DA_QK_W + (c + 1) * LANES], cda, sda, DA_ROT // 2, DA_HD)
                         for c in range(DA_QK_W // LANES)], axis=1)
    daq_ref[0] = (q * (DA_HD ** -0.5 * LOG2E)).T.astype(jnp.bfloat16)
    dak_ref[...] = k.astype(jnp.bfloat16)
    vt = u[:, 2 * DA_QK_W:2 * DA_QK_W + DA_W].T.astype(jnp.bfloat16)
    for c in range(TM // TK):
        dav_ref[c] = vt[:, c * TK:(c + 1) * TK]

    o1 = o0 + 2 * DA_QK_W + DA_W
    u = jnp.dot(xn2, win_ref[:, o1:IN_W_PAD], preferred_element_type=jnp.float32)
    cm, sm = cm_ref[...], sm_ref[...]
    lane = lax.broadcasted_iota(jnp.int32, cm.shape, 1)
    band = (lane >= MLA_NOPE) & (lane < MLA_NOPE + MLA_ROPE)
    cb, sb = jnp.where(band, cm, 1.0), jnp.where(band, sm, 0.0)
    cq = _rms(u[:, 0:Q_LORA], gq_ref[...]).astype(jnp.bfloat16)
    qm = jnp.dot(cq, wuq_ref[...], preferred_element_type=jnp.float32)
    qm = jnp.concatenate([_rope_chunk(qm[:, h * LANES:(h + 1) * LANES], cb, sb, MLA_ROPE // 2, MLA_ROPE)
                          for h in range(MLA_HEADS)], axis=1) * ((MLA_NOPE + MLA_ROPE) ** -0.5 * LOG2E)
    mq_ref[0] = qm.T.astype(jnp.bfloat16)
    ckv = _rms(u[:, Q_LORA:Q_LORA + KV_LORA], gkv_ref[...]).astype(jnp.bfloat16)
    kv = jnp.dot(ckv, wukv_ref[...], preferred_element_type=jnp.float32)
    kr = _rope_chunk(u[:, Q_LORA + KV_LORA:Q_LORA + KV_LORA + LANES], cm, sm, MLA_ROPE // 2, MLA_ROPE)
    kr = jnp.where(band, kr, 0.0)
    kf = jnp.concatenate([kv[:, h * LANES:(h + 1) * LANES] + kr for h in range(MLA_HEADS)], axis=1)
    mk_ref[...] = kf.astype(jnp.bfloat16)
    vt = kv[:, MLA_HEADS * LANES:MLA_HEADS * LANES + MLA_W].T.astype(jnp.bfloat16)
    for c in range(TM // TK):
        mv_ref[c] = vt[:, c * TK:(c + 1) * TK]

    x = srcs[-1][...]
    for sgm in range(n_src - 2, -1, -1):
        x = jnp.where(i < bounds[sgm + 1], srcs[sgm][...], x)
    xn = _rms(x, ga_ref[...]).astype(jnp.bfloat16)
    h1 = x + _swiglu_half(xn, wg_ref, wu_ref, wd_ref)
    h1_ref[...] = h1
    xn2_ref[...] = _rms(h1, gb_ref[...]).astype(jnp.bfloat16)


def _outproj_ffn_kernel(h_ref, ona_ref, oda_ref, om_ref, wo1_ref, wo2_ref, wo3_ref, g_ref,
                        wg_ref, wu_ref, wd_ref, gf_ref, *o_refs, out_bounds):
    h = h_ref[...]
    h = h + jnp.dot(ona_ref[...], wo1_ref[...], preferred_element_type=jnp.float32)
    h = h + jnp.dot(oda_ref[...], wo2_ref[...], preferred_element_type=jnp.float32)
    h = h + jnp.dot(om_ref[...], wo3_ref[...], preferred_element_type=jnp.float32)
    xn = _rms(h, g_ref[...]).astype(jnp.bfloat16)
    h = h + _swiglu_half(xn, wg_ref, wu_ref, wd_ref)
    if out_bounds is None:
        o_refs[0][...] = h
    else:
        h = _rms(h, gf_ref[...])
        i = pl.program_id(0)
        for o_ref, lo, hi in zip(o_refs, out_bounds[:-1], out_bounds[1:]):
            @pl.when((i >= lo) & (i < hi))
            def _(o_ref=o_ref):
                o_ref[...] = h


def _const_spec(shape):
    nd = len(shape)
    return pl.BlockSpec(shape, lambda i: (0,) * nd, pipeline_mode=pl.Buffered(1))


def _ffn_inproj(srcs, bounds, tabs, tab_block, ga, gb, wg, wu, wd, win, gq, gkv, wuq, wukv):
    nt = bounds[-1]
    rows = nt * TM
    prev = lambda i: jnp.maximum(i - 1, 0)
    row_spec = lambda w: pl.BlockSpec((TM, w), lambda i: (prev(i), 0))
    tile_spec = lambda r: pl.BlockSpec((1, r, TM), lambda i: (prev(i), 0, 0))
    bf = jnp.bfloat16
    out_shape = (
        jax.ShapeDtypeStruct((rows, D_MODEL), jnp.float32),
        jax.ShapeDtypeStruct((nt, NA_W, TM), bf),
        jax.ShapeDtypeStruct((rows, NA_W), bf),
        jax.ShapeDtypeStruct((rows // LANES, NA_W, LANES), bf),
        jax.ShapeDtypeStruct((nt, DA_QK_W, TM), bf),
        jax.ShapeDtypeStruct((rows, DA_QK_W), bf),
        jax.ShapeDtypeStruct((rows // TK, DA_W, TK), bf),
        jax.ShapeDtypeStruct((nt, MLA_HEADS * LANES, TM), bf),
        jax.ShapeDtypeStruct((rows, MLA_QK_W), bf),
        jax.ShapeDtypeStruct((rows // TK, MLA_W, TK), bf),
    )
    out_specs = (
        pl.BlockSpec((TM, D_MODEL), lambda i: (jnp.minimum(i, nt - 1), 0)),
        tile_spec(NA_W),
        row_spec(NA_W),
        pl.BlockSpec((TM // LANES, NA_W, LANES), lambda i: (prev(i), 0, 0)),
        tile_spec(DA_QK_W),
        row_spec(DA_QK_W),
        pl.BlockSpec((TM // TK, DA_W, TK), lambda i: (prev(i), 0, 0)),
        tile_spec(MLA_HEADS * LANES),
        row_spec(MLA_QK_W),
        pl.BlockSpec((TM // TK, MLA_W, TK), lambda i: (prev(i), 0, 0)),
    )
    in_specs = _segment_specs(bounds) + [pl.BlockSpec((TM, LANES), lambda i: (tab_block(prev(i)), 0))] * 4 + [
        _const_spec(a.shape) for a in (ga, gb, wg, wu, wd, win, gq, gkv, wuq, wukv)]
    return pl.pallas_call(
        functools.partial(_ffn_inproj_kernel, bounds=bounds),
        grid=(nt + 1,),
        in_specs=in_specs,
        out_specs=out_specs,
        out_shape=out_shape,
        scratch_shapes=[pltpu.VMEM((TM, D_MODEL), jnp.bfloat16)],
        compiler_params=pltpu.CompilerParams(dimension_semantics=("arbitrary",),
                                             vmem_limit_bytes=VMEM_LIMIT_BYTES),
        name="ffn_inproj",
    )(*srcs, *tabs, ga, gb, wg, wu, wd, win, gq, gkv, wuq, wukv)


def _outproj_ffn(h, ona, oda, om, wo1, wo2, wo3, g, wg, wu, wd, gf, out_bounds):
    rows = h.shape[0]
    row_spec = lambda w: pl.BlockSpec((TM, w), lambda i: (i, 0))
    if out_bounds is None:
        out_specs = row_spec(D_MODEL)
        out_shape = jax.ShapeDtypeStruct((rows, D_MODEL), jnp.float32)
    else:
        out_specs = tuple(_segment_specs(out_bounds))
        out_shape = tuple(jax.ShapeDtypeStruct(((hi - lo) * TM, D_MODEL), jnp.float32)
                          for lo, hi in zip(out_bounds[:-1], out_bounds[1:]))
    in_specs = [row_spec(D_MODEL), row_spec(NA_W), row_spec(DA_W), row_spec(MLA_W)] + [
        _const_spec(a.shape) for a in (wo1, wo2, wo3, g, wg, wu, wd, gf)]
    return pl.pallas_call(
        functools.partial(_outproj_ffn_kernel, out_bounds=out_bounds),
        grid=(rows // TM,),
        in_specs=in_specs,
        out_specs=out_specs,
        out_shape=out_shape,
        compiler_params=pltpu.CompilerParams(dimension_semantics=("arbitrary",),
                                             vmem_limit_bytes=VMEM_LIMIT_BYTES),
        name="outproj_ffn",
    )(h, ona, oda, om, wo1, wo2, wo3, g, wg, wu, wd, gf)


def _row_band(block, lo, hi):
    row = lax.broadcasted_iota(jnp.int32, block.shape, 0)
    return jnp.where((row >= lo) & (row < hi), block, jnp.zeros_like(block))


def _with_ones(vt):
    return jnp.concatenate([vt, jnp.ones((ONES_ROWS, vt.shape[1]), vt.dtype)], axis=0)


def _meta_key_mask(b):
    row = lax.broadcasted_iota(jnp.int32, (LANES, 1), 0)
    lo = (b % (LANES // N_META)) * N_META
    return jnp.where((row >= lo) & (row < lo + N_META), 0.0, NEG).astype(jnp.float32)


class FlashCfg(NamedTuple):
    n_soft: int
    group_w: int
    soft_per_group: int
    v_of_soft: tuple
    out_w: int
    unroll: int


DA_CFG = FlashCfg(2 * DA_HEADS, LANES, 4, tuple(i // 2 for i in range(2 * DA_HEADS)), DA_W, 16)
MLA_CFG = FlashCfg(MLA_HEADS, LANES, 1, tuple(range(MLA_HEADS)), MLA_W, 32)
NA_CFG = FlashCfg(NA_HEADS, LANES, 2, tuple(range(NA_HEADS)), NA_W, 1)


def _build_rhs(cfg, qt_ref, rhs_ref):
    for i in range(cfg.n_soft):
        g, j = i // cfg.soft_per_group, i % cfg.soft_per_group
        if cfg is DA_CFG:
            rhs_ref[i] = _row_band(qt_ref[0, g * LANES:(g + 1) * LANES, :], j * DA_HD, (j + 1) * DA_HD)
        else:
            rhs_ref[i] = qt_ref[0, i * LANES:(i + 1) * LANES, :]


def _flash_tiles(cfg, tiles, rhs_ref, m_ref, acc_ref, bounded, first=False):
    nt = len(tiles)
    if bounded:
        work = [(t, i) for i in range(cfg.n_soft) for t in range(nt)]
    else:
        work = [(t, i) for t in range(nt) for i in range(cfg.n_soft)]

    def score(w):
        kblk_of, _, mask = tiles[w[0]]
        s = jnp.dot(kblk_of(w[1] // cfg.soft_per_group), rhs_ref[w[1]], preferred_element_type=jnp.float32)
        return s if mask is None else s + mask

    pending = [score(w) for w in work[:LOOKAHEAD]]
    part = den = None
    for n, (t, i) in enumerate(work):
        vt = tiles[t][1](cfg.v_of_soft[i])
        s = pending.pop(0)
        if n + LOOKAHEAD < len(work):
            pending.append(score(work[n + LOOKAHEAD]))
        if bounded:
            p = jnp.exp2(s)
            pv = jnp.dot(vt, p.astype(jnp.bfloat16), preferred_element_type=jnp.float32)
            ps = jnp.sum(p, axis=0, keepdims=True)
            part, den = (pv, ps) if t == 0 else (part + pv, den + ps)
            if t == nt - 1 and first:
                acc_ref[i, 0:DA_VD, :] = part
                acc_ref[i, DA_VD:DA_VD + 1, :] = den
            elif t == nt - 1:
                acc_ref[i, 0:DA_VD, :] = acc_ref[i, 0:DA_VD, :] + part
                acc_ref[i, DA_VD:DA_VD + 1, :] = acc_ref[i, DA_VD:DA_VD + 1, :] + den
        else:
            vt = _with_ones(vt)
            m_prev = m_ref[i]
            m_new = jnp.maximum(m_prev, jnp.max(s, axis=0, keepdims=True))
            alpha = jnp.exp2(m_prev - m_new)
            p = jnp.exp2(s - m_new).astype(jnp.bfloat16)
            acc_ref[i] = alpha * acc_ref[i] + jnp.dot(vt, p, preferred_element_type=jnp.float32)
            m_ref[i] = m_new


def _feature_indicator(cfg):
    f = lax.broadcasted_iota(jnp.int32, (cfg.group_w, LANES), 0)
    j = lax.broadcasted_iota(jnp.int32, (cfg.group_w, LANES), 1)
    band = cfg.group_w // cfg.soft_per_group
    hit = jnp.right_shift(f, int(math.log2(band))) == j
    return jnp.where(hit & (j < cfg.soft_per_group), 1.0, 0.0).astype(jnp.bfloat16)


def _max_key_norms(cfg, kf_ref, kfm_ref, n_keys, kn_ref):
    gw = cfg.group_w
    n_groups = cfg.n_soft // cfg.soft_per_group
    ind = _feature_indicator(cfg)

    def sq_norms(blk):
        x = blk.astype(jnp.float32)
        n2 = jnp.dot((x * x).astype(jnp.bfloat16), ind, preferred_element_type=jnp.float32)
        return jnp.max(n2, axis=0, keepdims=True)

    def body(c, carry):
        r0 = pl.multiple_of(c * TM, TM)
        return tuple(jnp.maximum(carry[g], sq_norms(kf_ref[pl.ds(r0, TM), g * gw:(g + 1) * gw]))
                     for g in range(n_groups))

    init = tuple(sq_norms(kfm_ref[:, g * gw:(g + 1) * gw]) for g in range(n_groups))
    kmax = lax.fori_loop(0, n_keys // TM, body, init)
    lane = lax.broadcasted_iota(jnp.int32, (1, LANES), 1)
    for i in range(cfg.n_soft):
        g, j = divmod(i, cfg.soft_per_group)
        v = jnp.max(jnp.where(lane == j, kmax[g], 0.0), axis=1, keepdims=True)
        kn_ref[i] = jnp.broadcast_to(v, (1, LANES))


def _scores_are_bounded(cfg, rhs_ref, kn_ref):
    ok = None
    for i in range(cfg.n_soft):
        r = rhs_ref[i].astype(jnp.float32)
        bound2 = jnp.sum(r * r, axis=0, keepdims=True) * kn_ref[i][:, 0:1] * NORM_SLACK
        good = bound2 <= SCORE_BOUND * SCORE_BOUND
        ok = good if ok is None else ok & good
    return jnp.min(jnp.where(ok, 1.0, 0.0)) > 0.5


def _flash_kernel(*refs, cfg, n_keys, lam_init, meta_queries):
    if cfg is DA_CFG:
        (qt_ref, kf_ref, vt_ref, kfm_ref, vtm_ref, lam_ref, gs_ref, o_ref,
         rhs_ref, m_ref, acc_ref, kn_ref, *rest) = refs
    else:
        qt_ref, kf_ref, vt_ref, kfm_ref, vtm_ref, o_ref, rhs_ref, m_ref, acc_ref, kn_ref, *rest = refs
    b = pl.program_id(0)
    tq = qt_ref.shape[2]
    vd = DA_VD
    gw = cfg.group_w
    if meta_queries:
        _max_key_norms(cfg, kf_ref, kfm_ref, n_keys, kn_ref)
    else:
        @pl.when(pl.program_id(1) == 0)
        def _():
            _max_key_norms(cfg, kf_ref, kfm_ref, n_keys, kn_ref)
    _build_rhs(cfg, qt_ref, rhs_ref)

    def all_keys(bounded, unroll):
        if not bounded:
            m_ref[...] = jnp.full(m_ref.shape, NEG, jnp.float32)
            acc_ref[...] = jnp.zeros(acc_ref.shape, jnp.float32)
        _flash_tiles(
            cfg,
            [(lambda g: kfm_ref[:, g * gw:(g + 1) * gw],
              lambda h: vtm_ref[0, h * vd:(h + 1) * vd, :],
              _meta_key_mask(b))],
            rhs_ref, m_ref, acc_ref, bounded, first=True)

        def body(it, carry):
            tiles = []
            for u in range(unroll):
                kt = it * unroll + u
                k0 = pl.multiple_of(kt * TK, TK)
                tiles.append((
                    lambda g, k0=k0: kf_ref[pl.ds(k0, TK), g * gw:(g + 1) * gw],
                    lambda h, kt=kt: vt_ref[kt, h * vd:(h + 1) * vd, :],
                    None))
            _flash_tiles(cfg, tiles, rhs_ref, m_ref, acc_ref, bounded)
            return carry

        lax.fori_loop(0, n_keys // (TK * unroll), body, 0)

    lax.cond(_scores_are_bounded(cfg, rhs_ref, kn_ref),
             lambda: all_keys(True, math.gcd(cfg.unroll, n_keys // TK)),
             lambda: all_keys(False, 1))

    heads = []
    if cfg is DA_CFG:
        lp = lam_ref[...]
        lam = (jnp.exp(jnp.sum(lp[0:1] * lp[1:2], axis=1, keepdims=True))
               - jnp.exp(jnp.sum(lp[2:3] * lp[3:4], axis=1, keepdims=True)) + lam_init)
        for h in range(DA_HEADS):
            a1, a2 = acc_ref[2 * h], acc_ref[2 * h + 1]
            o = a1[0:vd] * (1.0 / a1[vd:vd + 1]) - a2[0:vd] * (lam / a2[vd:vd + 1])
            o = o * lax.rsqrt(jnp.mean(o * o, axis=0, keepdims=True) + EPS) * gs_ref[...]
            heads.append(o * (1.0 - lam_init))
    else:
        for h in range(MLA_HEADS):
            a = acc_ref[h]
            heads.append(a[0:vd] * (1.0 / a[vd:vd + 1]))
    pairs = [jnp.concatenate(heads[2 * g:2 * g + 2], axis=0).T for g in range(len(heads) // 2)]
    out = jnp.concatenate(pairs, axis=1)
    if meta_queries:
        stage_ref = rest[0]
        stage_ref[...] = out
        r0 = pl.multiple_of((b % (LANES // N_META)) * N_META, N_META)
        o_ref[...] = stage_ref[pl.ds(r0, N_META), :].astype(o_ref.dtype)
    else:
        o_ref[...] = out.astype(o_ref.dtype)


def _flash_call(cfg, grp, qt, kf, vt, extra, o_prev, lam_init, meta_queries, name):
    n, bsz = grp.n, grp.batch
    qw, kw, vw = qt.shape[1], kf.shape[1], vt.shape[1]
    tiles_per_seq = n // TM
    meta_blk0 = grp.meta0 // LANES
    per_blk = LANES // N_META
    if meta_queries:
        grid = (bsz,)
        tq = LANES
        q_map = lambda b: ((meta_blk0 + b // per_blk) // (TM // LANES), 0, (meta_blk0 + b // per_blk) % (TM // LANES))
        o_spec = pl.BlockSpec((N_META, cfg.out_w), lambda b: (grp.meta0 // N_META + b, 0))
        fix = lambda f: (lambda b: f(b))
        sem = ("arbitrary",)
    else:
        grid = (bsz, tiles_per_seq)
        tq = TM
        q_map = lambda b, i: (grp.real0 // TM + b * tiles_per_seq + i, 0, 0)
        o_spec = pl.BlockSpec((TM, cfg.out_w), lambda b, i: (grp.real0 // TM + b * tiles_per_seq + i, 0))
        fix = lambda f: (lambda b, i: f(b))
        sem = ("arbitrary", "arbitrary")
    in_specs = [
        pl.BlockSpec((1, qw, tq), q_map),
        pl.BlockSpec((n, kw), fix(lambda b: (grp.real0 // n + b, 0))),
        pl.BlockSpec((n // TK, vw, TK), fix(lambda b: (grp.real0 // n + b, 0, 0))),
        pl.BlockSpec((LANES, kw), fix(lambda b: (meta_blk0 + b // per_blk, 0))),
        pl.BlockSpec((1, vw, LANES), fix(lambda b: ((meta_blk0 + b // per_blk) // (TK // LANES), 0,
                                                     (meta_blk0 + b // per_blk) % (TK // LANES)))),
    ]
    args = [qt, kf, vt, kf, vt]
    for a in extra:
        in_specs.append(pl.BlockSpec(a.shape, fix(lambda b, nd=a.ndim: (0,) * nd)))
        args.append(a)
    in_specs.append(pl.BlockSpec(memory_space=pl.ANY))
    args.append(o_prev)
    scratch = [
        pltpu.VMEM((cfg.n_soft, cfg.group_w, tq), jnp.bfloat16),
        pltpu.VMEM((cfg.n_soft, 1, tq), jnp.float32),
        pltpu.VMEM((cfg.n_soft, DA_VD + ONES_ROWS, tq), jnp.float32),
        pltpu.VMEM((cfg.n_soft, 1, LANES), jnp.float32),
    ]
    if meta_queries:
        scratch.append(pltpu.VMEM((LANES, cfg.out_w), jnp.float32))

    def body(*refs):
        n_in = len(args)
        ins, rest = refs[:n_in - 1], refs[n_in:]
        _flash_kernel(*ins, *rest, cfg=cfg, n_keys=n, lam_init=lam_init, meta_queries=meta_queries)

    return pl.pallas_call(
        body,
        grid=grid,
        in_specs=in_specs,
        out_specs=o_spec,
        out_shape=jax.ShapeDtypeStruct(o_prev.shape, o_prev.dtype),
        scratch_shapes=scratch,
        input_output_aliases={len(args) - 1: 0},
        compiler_params=pltpu.CompilerParams(dimension_semantics=sem, vmem_limit_bytes=VMEM_LIMIT_BYTES),
        name=name,
    )(*args)


def _na_kernel(qt_ref, kf_ref, vt_ref, kfm_ref, vtm_ref, bias_ref, bmax_ref, prev_ref, o_ref, kn_ref, *, grid_rows):
    del prev_ref
    b, i = pl.program_id(0), pl.program_id(1)

    @pl.when(i == 0)
    def _():
        _max_key_norms(NA_CFG, kf_ref, kfm_ref, grid_rows * GRID_W, kn_ref)

    ws = jnp.clip(i * NA_QROWS - WIN_R // 2, 0, grid_rows - NA_KROWS)
    kwin = kf_ref[pl.ds(pl.multiple_of(ws * GRID_W, 2 * GRID_W), NA_KROWS * GRID_W), :]
    vwin = vt_ref[pl.ds(ws // 2, NA_KROWS // 2)]
    kmeta = kfm_ref[...]
    mmask = _meta_key_mask(b)

    def rhs_of(h):
        g, j = h // 2, h % 2
        return _row_band(qt_ref[0, g * LANES:(g + 1) * LANES, :], j * NA_HD, (j + 1) * NA_HD)

    def scores(h):
        g, rhs = h // 2, rhs_of(h)
        s1 = jnp.dot(kwin[:, g * LANES:(g + 1) * LANES], rhs, preferred_element_type=jnp.float32) + bias_ref[0, h]
        s2 = jnp.dot(kmeta[:, g * LANES:(g + 1) * LANES], rhs, preferred_element_type=jnp.float32) + mmask
        return s1, s2

    def run(bounded):
        heads = []
        pending = [scores(h) for h in range(LOOKAHEAD)]
        for h in range(NA_HEADS):
            s1, s2 = pending.pop(0)
            if h + LOOKAHEAD < NA_HEADS:
                pending.append(scores(h + LOOKAHEAD))
            if not bounded:
                m = jnp.maximum(jnp.max(s1, axis=0, keepdims=True), jnp.max(s2, axis=0, keepdims=True))
                s1, s2 = s1 - m, s2 - m
            p1 = jnp.exp2(s1).astype(jnp.bfloat16)
            p2 = jnp.exp2(s2).astype(jnp.bfloat16)
            v1 = jnp.concatenate([vwin[c, h * NA_HD:(h + 1) * NA_HD, :] for c in range(NA_KROWS // 2)], axis=1)
            acc = (jnp.dot(_with_ones(v1), p1, preferred_element_type=jnp.float32)
                   + jnp.dot(_with_ones(vtm_ref[0, h * NA_HD:(h + 1) * NA_HD, :]), p2,
                             preferred_element_type=jnp.float32))
            heads.append(acc[0:NA_HD] * (1.0 / acc[NA_HD:NA_HD + 1]))
        pairs = [jnp.concatenate(heads[2 * g:2 * g + 2], axis=0).T for g in range(NA_HEADS // 2)]
        o_ref[...] = jnp.concatenate(pairs, axis=1).astype(o_ref.dtype)

    limit = SCORE_BOUND - bmax_ref[...]
    ok = None
    for h in range(NA_HEADS):
        r = rhs_of(h).astype(jnp.float32)
        bound2 = jnp.sum(r * r, axis=0, keepdims=True) * kn_ref[h][:, 0:1] * NORM_SLACK
        good = (limit > 0.0) & (bound2 <= limit * limit)
        ok = good if ok is None else ok & good
    lax.cond(jnp.min(jnp.where(ok, 1.0, 0.0)) > 0.5, lambda: run(True), lambda: run(False))


def _na_meta_kernel(qt_ref, kf_ref, vt_ref, prev_ref, o_ref):
    del prev_ref
    kr = lax.broadcasted_iota(jnp.int32, (LANES, LANES), 0) // N_META
    qc = lax.broadcasted_iota(jnp.int32, (LANES, LANES), 1) // N_META
    mask = jnp.where(kr == qc, 0.0, NEG).astype(jnp.float32)
    kf = kf_ref[...]
    heads = []
    for h in range(NA_HEADS):
        g, j = h // 2, h % 2
        rhs = _row_band(qt_ref[0, g * LANES:(g + 1) * LANES, :], j * NA_HD, (j + 1) * NA_HD)
        s = jnp.dot(kf[:, g * LANES:(g + 1) * LANES], rhs, preferred_element_type=jnp.float32) + mask
        p = jnp.exp2(s - jnp.max(s, axis=0, keepdims=True)).astype(jnp.bfloat16)
        acc = jnp.dot(_with_ones(vt_ref[0, h * NA_HD:(h + 1) * NA_HD, :]), p, preferred_element_type=jnp.float32)
        heads.append(acc[0:NA_HD] * (1.0 / acc[NA_HD:NA_HD + 1]))
    pairs = [jnp.concatenate(heads[2 * g:2 * g + 2], axis=0).T for g in range(NA_HEADS // 2)]
    o_ref[...] = jnp.concatenate(pairs, axis=1).astype(o_ref.dtype)


def _na_call(grp, qt, kf, vt, bias, bmax, o_prev):
    n, bsz = grp.n, grp.batch
    grid_rows = n // GRID_W
    steps = grid_rows // NA_QROWS
    tq = NA_QROWS * GRID_W
    meta_blk0 = grp.meta0 // LANES
    per_blk = LANES // N_META
    in_specs = [
        pl.BlockSpec((1, NA_W, tq), lambda b, i: (grp.real0 // TM + b * (n // TM) + i // (TM // tq), 0, i % (TM // tq))),
        pl.BlockSpec((n, NA_W), lambda b, i: (grp.real0 // n + b, 0)),
        pl.BlockSpec((n // LANES, NA_W, LANES), lambda b, i: (grp.real0 // n + b, 0, 0)),
        pl.BlockSpec((LANES, NA_W), lambda b, i: (meta_blk0 + b // per_blk, 0)),
        pl.BlockSpec((1, NA_W, LANES), lambda b, i: (meta_blk0 + b // per_blk, 0, 0)),
        pl.BlockSpec((1, NA_HEADS, NA_KROWS * GRID_W, tq),
                     lambda b, i: (jnp.where(i == 0, 0, jnp.where(i == steps - 1, 2, 1)), 0, 0, 0)),
        pl.BlockSpec((1, 1), lambda b, i: (0, 0)),
        pl.BlockSpec(memory_space=pl.ANY),
    ]
    return pl.pallas_call(
        functools.partial(_na_kernel, grid_rows=grid_rows),
        grid=(bsz, steps),
        in_specs=in_specs,
        out_specs=pl.BlockSpec((tq, NA_W), lambda b, i: (grp.real0 // tq + b * steps + i, 0)),
        out_shape=jax.ShapeDtypeStruct(o_prev.shape, o_prev.dtype),
        scratch_shapes=[pltpu.VMEM((NA_HEADS, 1, LANES), jnp.float32)],
        input_output_aliases={7: 0},
        compiler_params=pltpu.CompilerParams(dimension_semantics=("arbitrary", "arbitrary"),
                                             vmem_limit_bytes=VMEM_LIMIT_BYTES),
        name="na_real",
    )(qt, kf, vt, kf, vt, bias, bmax, o_prev)


def _na_meta_call(layout, qt, kf, vt, o_prev):
    blk0 = layout.groups[0].meta0 // LANES
    nblk = sum(_round_up(g.batch * N_META, LANES) for g in layout.groups) // LANES
    sub = TM // LANES
    return pl.pallas_call(
        _na_meta_kernel,
        grid=(nblk,),
        in_specs=[
            pl.BlockSpec((1, NA_W, LANES), lambda i: ((blk0 + i) // sub, 0, (blk0 + i) % sub)),
            pl.BlockSpec((LANES, NA_W), lambda i: (blk0 + i, 0)),
            pl.BlockSpec((1, NA_W, LANES), lambda i: (blk0 + i, 0, 0)),
            pl.BlockSpec(memory_space=pl.ANY),
        ],
        out_specs=pl.BlockSpec((LANES, NA_W), lambda i: (blk0 + i, 0)),
        out_shape=jax.ShapeDtypeStruct(o_prev.shape, o_prev.dtype),
        input_output_aliases={3: 0},
        compiler_params=pltpu.CompilerParams(dimension_semantics=("arbitrary",),
                                             vmem_limit_bytes=VMEM_LIMIT_BYTES),
        name="na_meta",
    )(qt, kf, vt, o_prev)


def _na_bias_tables(rel_bias):
    kc = np.arange(GRID_W)[:, None]
    qc = np.arange(GRID_W)[None, :]
    cs = np.clip(qc - WIN_C // 2, 0, GRID_W - WIN_C)
    col_ok = (kc >= cs) & (kc < cs + WIN_C)
    cidx = np.clip(kc - qc + WIN_C - 1, 0, 2 * WIN_C - 2)
    planes = jnp.where(col_ok[None, None], rel_bias[:, :, cidx] * LOG2E, NEG)
    masked = 2 * WIN_R - 1
    planes = jnp.concatenate([planes, jnp.full((NA_HEADS, 1, GRID_W, GRID_W), NEG, planes.dtype)], axis=1)
    kr = np.arange(NA_KROWS)[:, None]
    qr = np.arange(NA_QROWS)[None, :]
    ridx = []
    for win_lo, q_off in ((0 * qr, qr), (qr, qr + WIN_R // 2), (0 * qr + NA_KROWS - WIN_R, qr + NA_KROWS - NA_QROWS)):
        row_ok = (kr >= win_lo) & (kr < win_lo + WIN_R)
        ridx.append(np.where(row_ok, kr - q_off + WIN_R - 1, masked))
    ridx = np.stack(ridx)
    assert ridx.min() >= 0 and ridx.max() <= masked
    t = planes[:, ridx]
    t = jnp.transpose(t, (1, 0, 2, 4, 3, 5))
    return t.reshape(3, NA_HEADS, NA_KROWS * GRID_W, NA_QROWS * GRID_W).astype(jnp.float32)


def _rope_tables(layout):
    pos, starts, periods = [], [], []
    for g in layout.groups:
        starts.append(len(pos))
        periods.append(g.n // TM)
        for t in range(g.n // TM):
            pos.append(N_META + t * TM + np.arange(TM))
    tail0 = sum(g.batch * g.n for g in layout.groups)
    starts.append(len(pos))
    for t in range(tail0 // TM, layout.rows // TM):
        r = t * TM + np.arange(TM)
        p = np.zeros((TM,), np.int64)
        for g in layout.groups:
            inside = (r >= g.meta0) & (r < g.meta0 + g.batch * N_META)
            p = np.where(inside, (r - g.meta0) % N_META, p)
        pos.append(p)
    pos = jnp.asarray(np.concatenate(pos).astype(np.float32))
    bounds = [g.real0 // TM for g in layout.groups] + [tail0 // TM]

    def block_of_tile(i):
        blk = starts[-1] + (i - bounds[-1])
        for k in range(len(layout.groups) - 1, -1, -1):
            blk = jnp.where(i < bounds[k + 1], starts[k] + (i - bounds[k]) % periods[k], blk)
        return blk

    def tables(dim, period):
        inv = ROPE_THETA ** (-(jnp.arange(0, dim, 2, dtype=jnp.float32) / dim))
        ang = pos[:, None] * inv[None, :]
        cos, sin = jnp.cos(ang), jnp.sin(ang)
        rest = period - dim
        c = jnp.concatenate([cos, cos, jnp.ones((pos.shape[0], rest), jnp.float32)], axis=1)
        s = jnp.concatenate([-sin, sin, jnp.zeros((pos.shape[0], rest), jnp.float32)], axis=1)
        reps = LANES // period
        return jnp.tile(c, (1, reps)), jnp.tile(s, (1, reps))

    cda, sda = tables(DA_ROT, DA_HD)
    cm, sm = tables(MLA_ROPE, MLA_ROPE)
    return (cda, sda, cm, sm), block_of_tile


def _prep_layer(l, norm_g, ffn_w_gate, ffn_w_up, ffn_w_down, w_in, w_out, mla_q_norm_g, mla_kv_norm_g,
                mla_w_uq, mla_w_ukv, da_subln_g):
    bf = jnp.bfloat16
    kr0 = IN_W - MLA_ROPE
    win = jnp.concatenate([w_in[l, :, :kr0]] + [w_in[l, :, kr0:]] * (LANES // MLA_ROPE), axis=1).astype(bf)
    uq = mla_w_uq[l].reshape(Q_LORA, MLA_HEADS, MLA_NOPE + MLA_ROPE)
    uq = jnp.pad(uq, ((0, 0), (0, 0), (0, LANES - MLA_NOPE - MLA_ROPE))).reshape(Q_LORA, MLA_HEADS * LANES)
    ukv = mla_w_ukv[l].reshape(KV_LORA, MLA_HEADS, MLA_NOPE + MLA_VD)
    uk = jnp.pad(ukv[:, :, :MLA_NOPE], ((0, 0), (0, 0), (0, LANES - MLA_NOPE))).reshape(KV_LORA, MLA_HEADS * LANES)
    ukv = jnp.concatenate([uk, ukv[:, :, MLA_NOPE:].reshape(KV_LORA, -1)], axis=1)
    row = lambda v: v.reshape(1, -1)
    return dict(
        g=[row(norm_g[l, i]) for i in range(3)],
        ffn=[(ffn_w_gate[l, i].astype(bf), ffn_w_up[l, i].astype(bf), ffn_w_down[l, i].astype(bf)) for i in range(2)],
        win=win, uq=uq.astype(bf), ukv=ukv.astype(bf),
        gq=row(mla_q_norm_g[l]), gkv=row(mla_kv_norm_g[l]),
        wo=(w_out[l, :NA_W].astype(bf), w_out[l, NA_W:NA_W + DA_W].astype(bf), w_out[l, NA_W + DA_W:].astype(bf)),
        gs=da_subln_g[l].reshape(DA_VD, 1),
    )


def kernel(x_prompt, x_sample, meta_tokens, norm_g, final_norm_g, ffn_w_gate, ffn_w_up, ffn_w_down, w_in, w_out, na_rel_bias, da_lambda, da_subln_g, mla_q_norm_g, mla_kv_norm_g, mla_w_uq, mla_w_ukv):
    xs = (x_prompt, x_sample)
    layout = _make_layout([(x.shape[0], x.shape[1]) for x in xs])
    real_rows = sum(g.batch * g.n for g in layout.groups)
    tail = []
    for g in layout.groups:
        blk = jnp.tile(meta_tokens.astype(jnp.float32), (g.batch, 1))
        tail.append(jnp.pad(blk, ((0, _round_up(g.batch * N_META, LANES) - g.batch * N_META), (0, 0))))
    tail = jnp.concatenate(tail, axis=0)
    tail = jnp.pad(tail, ((0, layout.rows - real_rows - tail.shape[0]), (0, 0)))
    srcs = [x.reshape(-1, D_MODEL) for x in xs] + [tail]
    real_bounds = tuple(g.real0 // TM for g in layout.groups) + (real_rows // TM,)
    bounds = real_bounds + (layout.rows // TM,)
    tabs, tab_block = _rope_tables(layout)
    gf = final_norm_g.reshape(1, -1)

    o_na = jnp.zeros((layout.rows, NA_W), jnp.bfloat16)
    o_da = jnp.zeros((layout.rows, DA_W), jnp.bfloat16)
    o_m = jnp.zeros((layout.rows, MLA_W), jnp.bfloat16)
    for l in range(DEPTH):
        p = _prep_layer(l, norm_g, ffn_w_gate, ffn_w_up, ffn_w_down, w_in, w_out, mla_q_norm_g, mla_kv_norm_g,
                        mla_w_uq, mla_w_ukv, da_subln_g)
        lam_init = 0.8 - 0.6 * math.exp(-0.3 * l)
        (h, naq, nak, nav, daq, dak, dav, mq, mk, mv) = _ffn_inproj(
            srcs, bounds, tabs, tab_block, p["g"][0], p["g"][1], *p["ffn"][0], p["win"], p["gq"], p["gkv"], p["uq"], p["ukv"])
        bias = _na_bias_tables(na_rel_bias[l])
        bmax = (jnp.max(jnp.abs(na_rel_bias[l])) * LOG2E).reshape(1, 1)
        o_na = _na_meta_call(layout, naq, nak, nav, o_na)
        for gi, grp in enumerate(layout.groups):
            o_na = _na_call(grp, naq, nak, nav, bias, bmax, o_na)
            for meta_q in (False, True):
                tag = f"g{gi}_{'meta' if meta_q else 'real'}"
                o_da = _flash_call(DA_CFG, grp, daq, dak, dav, (da_lambda[l], p["gs"]), o_da, lam_init, meta_q,
                                   "da_" + tag)
                o_m = _flash_call(MLA_CFG, grp, mq, mk, mv, (), o_m, lam_init, meta_q, "mla_" + tag)
        last = l == DEPTH - 1
        h = _outproj_ffn(h, o_na, o_da, o_m, *p["wo"], p["g"][2], *p["ffn"][1], gf, real_bounds if last else None)
        srcs, bounds = [h], (0, layout.rows // TM)

    return tuple(y.reshape(x.shape) for x, y in zip(xs, h))
```
